```python
import math
import jax, jax.numpy as jnp
from jax import lax
import numpy as np

D_MODEL = 2048
BATCH = 2
SEQ = 16384
DEPTH = 4
DEC_BATCH = 2
DEC_SEQ = 8192
PAST_LEN = 128

N_MIXERS = 2
N_RWKV_LAYERS = (DEPTH + N_MIXERS - 1) // N_MIXERS
N_ATTN_LAYERS = DEPTH // N_MIXERS

RWKV_HEAD = 64
RWKV_HEADS = D_MODEL // RWKV_HEAD
DECAY_LORA = 96
ICLR_LORA = 96
VRES_LORA = 64
GATE_LORA = 256
GN_EPS = 64e-5
N_SHIFT_MIX = 6

ATT_HEAD = 128
ATT_HEADS = D_MODEL // ATT_HEAD
DILATED_GROUPS = ((128, 1), (512, 4), (2048, 16))
N_GROUPS = len(DILATED_GROUPS)
ROPE_THETA = 500000.0
ROPE_DIM = ATT_HEAD // 4

D_FF = 5632
NORM_EPS = 1e-6

kernel_name = 'hybrid_rwkv7_dilated_attn_macaron_encoder'


def rms_norm(x, g):
    xf = x.astype(jnp.float32)
    y = xf * lax.rsqrt(jnp.mean(xf * xf, axis=-1, keepdims=True) + NORM_EPS)
    return (y * g.astype(jnp.float32)).astype(x.dtype)


def swiglu(x, w_gate, w_up, w_down):
    return (jax.nn.silu(x @ w_gate) * (x @ w_up)) @ w_down


def centred_shift(x):
    prev = jnp.pad(x[:, :-1], ((0, 0), (1, 0), (0, 0)))
    nxt = jnp.pad(x[:, 1:], ((0, 0), (0, 1), (0, 0)))
    return 0.5 * (prev + nxt) - x


def wkv7_scan(r, w, k, v, a, b, reverse):
    B, S, H, N = r.shape

    def step(state, inp):
        r_t, w_t, k_t, v_t, a_t, b_t = inp
        sa = jnp.einsum('bhvk,bhk->bhv', state, a_t)
        state = (state * w_t[:, :, None, :] + sa[..., None] * b_t[:, :, None, :]
                 + v_t[..., None] * k_t[:, :, None, :])
        return state, jnp.einsum('bhvk,bhk->bhv', state, r_t)

    xs = tuple(jnp.moveaxis(t, 1, 0) for t in (r, w, k, v, a, b))
    state0 = jnp.zeros((B, H, N, N), jnp.float32)
    _, y = lax.scan(step, state0, xs, reverse=reverse)
    return jnp.moveaxis(y, 0, 1)


def rwkv7_mix(x, v_first, mu, w_rkv, w0, w1, w2, a0, a1, a2, g1, g2,
              k_k, k_a, r_k, gn_w, gn_b, w_o, vres):
    B, S, D = x.shape
    H, N = RWKV_HEADS, RWKV_HEAD
    f32 = jnp.float32
    xx = centred_shift(x)
    xr, xw, xk, xv, xa, xg = [x + xx * mu[i] for i in range(N_SHIFT_MIX)]
    r = xr @ w_rkv[0]
    k = xk @ w_rkv[1]
    v = xv @ w_rkv[2]
    if vres is None:
        v_first = v
    else:
        v0, v1, v2 = vres
        v = v + (v_first - v) * jax.nn.sigmoid(v0 + (xv @ v1) @ v2)
    g = jax.nn.sigmoid(xg @ g1) @ g2

    def heads(t):
        return t.astype(f32).reshape(B, S, H, N)

    rh, vh, kh = heads(r), heads(v), heads(k)
    kk = heads(k * k_k)
    kk = kk / jnp.maximum(jnp.linalg.norm(kk, axis=-1, keepdims=True), 1e-12)
    k_a_h = k_a.astype(f32).reshape(H, N)
    ys, kds = [], []
    for d in range(2):
        w_log = -jax.nn.softplus(-(w0[d] + jnp.tanh(xw @ w1[d]) @ w2[d])) - 0.5
        decay = jnp.exp(-jnp.exp(heads(w_log)))
        a = heads(jax.nn.sigmoid(a0[d] + (xa @ a1[d]) @ a2[d]))
        kd = kh * (1.0 + (a - 1.0) * k_a_h)
        ys.append(wkv7_scan(rh, decay, kd, vh, -kk, kk * a, reverse=(d == 1)))
        kds.append(kd)
    y = ys[0] + ys[1]
    mean = jnp.mean(y, axis=-1, keepdims=True)
    var = jnp.mean(jnp.square(y - mean), axis=-1, keepdims=True)
    y = ((y - mean) * lax.rsqrt(var + GN_EPS)).reshape(B, S, D)
    y = y * gn_w.astype(f32) + gn_b.astype(f32)
    k_mean = 0.5 * (kds[0] + kds[1])
    bonus = jnp.sum(rh * k_mean * r_k.astype(f32), axis=-1, keepdims=True) * vh
    o = (y + bonus.reshape(B, S, D)) * g.astype(f32)
    return o.astype(x.dtype) @ w_o, v_first


def rope_partial(x, pos):
    half = ROPE_DIM // 2
    inv = ROPE_THETA ** (-jnp.arange(half, dtype=jnp.float32) / half)
    ang = pos.astype(jnp.float32)[:, None] * inv[None, :]
    cos = jnp.cos(ang)[None, :, None, :]
    sin = jnp.sin(ang)[None, :, None, :]
    xr = x[..., :ROPE_DIM].astype(jnp.float32)
    x1, x2 = xr[..., :half], xr[..., half:]
    rot = jnp.concatenate([x1 * cos - x2 * sin, x1 * sin + x2 * cos], axis=-1)
    return jnp.concatenate([rot.astype(x.dtype), x[..., ROPE_DIM:]], axis=-1)


def banded_attention(q, k, v, half):
    Nn, L, H, dh = q.shape
    nb = -(-L // half)
    Lp = nb * half
    pad = Lp - L
    qb = jnp.pad(q, ((0, 0), (0, pad), (0, 0), (0, 0))).reshape(Nn, nb, half, H, dh)

    def key_windows(t):
        tp = jnp.pad(t, ((0, 0), (half, pad + half), (0, 0), (0, 0)))
        tp = tp.reshape(Nn, nb + 2, half, H, dh)
        return jnp.concatenate([tp[:, :-2], tp[:, 1:-1], tp[:, 2:]], axis=2)

    kw, vw = key_windows(k), key_windows(v)
    qi = jnp.arange(half)[:, None]
    kj = jnp.arange(3 * half)[None, :]
    rel = kj - half - qi
    kpos = jnp.arange(nb)[:, None, None] * half - half + kj[None]
    valid = (jnp.abs(rel)[None] <= half) & (kpos >= 0) & (kpos < L)
    s = jnp.einsum('nbqhd,nbkhd->nbhqk', qb, kw,
                   preferred_element_type=jnp.float32) * (dh ** -0.5)
    s = jnp.where(valid[None, :, None], s, -jnp.inf)
    lse = jax.nn.logsumexp(s, axis=-1)
    p = jnp.exp(s - lse[..., None])
    o = jnp.einsum('nbhqk,nbkhd->nbqhd', p.astype(v.dtype), vw)
    o = o.reshape(Nn, Lp, H, dh)[:, :L]
    lse = lse.transpose(0, 1, 3, 2).reshape(Nn, Lp, H)[:, :L]
    return o, lse


def dilated_group(q, k, v, window, dil):
    B, S, H, dh = q.shape
    half = window // (2 * dil)

    def to_classes(t):
        return t.reshape(B, S // dil, dil, H, dh).transpose(0, 2, 1, 3, 4).reshape(B * dil, S // dil, H, dh)

    o, lse = banded_attention(to_classes(q), to_classes(k), to_classes(v), half)
    o = o.reshape(B, dil, S // dil, H, dh).transpose(0, 2, 1, 3, 4).reshape(B, S, H, dh)
    lse = lse.reshape(B, dil, S // dil, H).transpose(0, 2, 1, 3).reshape(B, S, H)
    return o, lse


def dilated_attention_mix(x, w_qkv, w_o):
    B, S, D = x.shape
    qkv = (x @ w_qkv).reshape(B, S, N_GROUPS, 3, ATT_HEADS, ATT_HEAD)
    pos = jnp.arange(S)
    outs, lses = [], []
    for gi, (window, dil) in enumerate(DILATED_GROUPS):
        q = rope_partial(qkv[:, :, gi, 0], pos)
        k = rope_partial(qkv[:, :, gi, 1], pos)
        o, lse = dilated_group(q, k, qkv[:, :, gi, 2], window, dil)
        outs.append(o)
        lses.append(lse)
    wts = jax.nn.softmax(jnp.stack(lses, axis=0), axis=0)
    o = jnp.einsum('gbsh,gbshd->bshd', wts, jnp.stack(outs, axis=0).astype(jnp.float32))
    return o.reshape(B, S, D).astype(x.dtype) @ w_o


def encoder_trunk(x, norm_pre, norm_post, ffn_w_gate, ffn_w_up, ffn_w_down,
                  rwkv_mu, rwkv_w_rkv, rwkv_w0, rwkv_w1, rwkv_w2, rwkv_a0, rwkv_a1, rwkv_a2,
                  rwkv_v0, rwkv_v1, rwkv_v2, rwkv_g1, rwkv_g2, rwkv_k_k, rwkv_k_a, rwkv_r_k,
                  rwkv_gn_w, rwkv_gn_b, rwkv_w_o, attn_w_qkv, attn_w_o):
    v_first = None
    for layer in range(DEPTH):
        h = swiglu(rms_norm(x, norm_pre[layer, 0]), ffn_w_gate[layer, 0], ffn_w_up[layer, 0], ffn_w_down[layer, 0])
        x = x + 0.5 * rms_norm(h, norm_post[layer, 0])
        h = rms_norm(x, norm_pre[layer, 1])
        j = layer // N_MIXERS
        if layer % N_MIXERS == 0:
            vres = None if j == 0 else (rwkv_v0[j - 1], rwkv_v1[j - 1], rwkv_v2[j - 1])
            h, v_first = rwkv7_mix(h, v_first, rwkv_mu[j], rwkv_w_rkv[j], rwkv_w0[j], rwkv_w1[j], rwkv_w2[j],
                                   rwkv_a0[j], rwkv_a1[j], rwkv_a2[j], rwkv_g1[j], rwkv_g2[j],
                                   rwkv_k_k[j], rwkv_k_a[j], rwkv_r_k[j], rwkv_gn_w[j], rwkv_gn_b[j],
                                   rwkv_w_o[j], vres)
        else:
            h = dilated_attention_mix(h, attn_w_qkv[j], attn_w_o[j])
        x = x + rms_norm(h, norm_post[layer, 1])
        h = swiglu(rms_norm(x, norm_pre[layer, 2]), ffn_w_gate[layer, 1], ffn_w_up[layer, 1], ffn_w_down[layer, 1])
        x = x + 0.5 * rms_norm(h, norm_post[layer, 2])
    return x


def setup_inputs(seed: int = 0) -> dict:
    key = jax.random.key(seed)
    keys = list(jax.random.split(key, 40))
    D, F, H, N = D_MODEL, D_FF, RWKV_HEADS, RWKV_HEAD
    NA, NB = N_RWKV_LAYERS, N_ATTN_LAYERS

    def nrm(shape, scale):
        return scale * jax.random.normal(keys.pop(), shape, jnp.float32)

    def unif(shape, lo, hi):
        return jax.random.uniform(keys.pop(), shape, jnp.float32, lo, hi)

    return {
        'x_prompt': nrm((BATCH, SEQ, D), 1.0),
        'x_sample': nrm((DEC_BATCH, DEC_SEQ, D), 1.0),
        'norm_pre': 1.0 + nrm((DEPTH, 3, D), 0.05),
        'norm_post': 1.0 + nrm((DEPTH, 3, D), 0.05),
        'ffn_w_gate': nrm((DEPTH, 2, D, F), D ** -0.5),
        'ffn_w_up': nrm((DEPTH, 2, D, F), D ** -0.5),
        'ffn_w_down': nrm((DEPTH, 2, F, D), F ** -0.5),
        'rwkv_mu': unif((NA, N_SHIFT_MIX, D), 0.0, 1.0),
        'rwkv_w_rkv': nrm((NA, 3, D, D), D ** -0.5),
        'rwkv_w0': unif((NA, 2, D), -6.0, -1.0),
        'rwkv_w1': nrm((NA, 2, D, DECAY_LORA), D ** -0.5),
        'rwkv_w2': nrm((NA, 2, DECAY_LORA, D), 0.3 * DECAY_LORA ** -0.5),
        'rwkv_a0': nrm((NA, 2, D), 0.5),
        'rwkv_a1': nrm((NA, 2, D, ICLR_LORA), D ** -0.5),
        'rwkv_a2': nrm((NA, 2, ICLR_LORA, D), 0.5 * ICLR_LORA ** -0.5),
        'rwkv_v0': nrm((NA - 1, D), 0.5),
        'rwkv_v1': nrm((NA - 1, D, VRES_LORA), D ** -0.5),
        'rwkv_v2': nrm((NA - 1, VRES_LORA, D), 0.5 * VRES_LORA ** -0.5),
        'rwkv_g1': nrm((NA, D, GATE_LORA), D ** -0.5),
        'rwkv_g2': nrm((NA, GATE_LORA, D), GATE_LORA ** -0.5),
        'rwkv_k_k': 1.0 + nrm((NA, D), 0.1),
        'rwkv_k_a': 1.0 + nrm((NA, D), 0.1),
        'rwkv_r_k': nrm((NA, H, N), 0.1),
        'rwkv_gn_w': 1.0 + nrm((NA, D), 0.05),
        'rwkv_gn_b': nrm((NA, D), 0.01),
        'rwkv_w_o': nrm((NA, D, D), D ** -0.5),
        'attn_w_qkv': nrm((NB, D, N_GROUPS * 3 * ATT_HEADS * ATT_HEAD), D ** -0.5),
        'attn_w_o': nrm((NB, ATT_HEADS * ATT_HEAD, D), (ATT_HEADS * ATT_HEAD) ** -0.5),
    }


def reference(x_prompt, x_sample, norm_pre, norm_post, ffn_w_gate, ffn_w_up, ffn_w_down,
              rwkv_mu, rwkv_w_rkv, rwkv_w0, rwkv_w1, rwkv_w2, rwkv_a0, rwkv_a1, rwkv_a2,
              rwkv_v0, rwkv_v1, rwkv_v2, rwkv_g1, rwkv_g2, rwkv_k_k, rwkv_k_a, rwkv_r_k,
              rwkv_gn_w, rwkv_gn_b, rwkv_w_o, attn_w_qkv, attn_w_o):
    weights = (norm_pre, norm_post, ffn_w_gate, ffn_w_up, ffn_w_down,
               rwkv_mu, rwkv_w_rkv, rwkv_w0, rwkv_w1, rwkv_w2, rwkv_a0, rwkv_a1, rwkv_a2,
               rwkv_v0, rwkv_v1, rwkv_v2, rwkv_g1, rwkv_g2, rwkv_k_k, rwkv_k_a, rwkv_r_k,
               rwkv_gn_w, rwkv_gn_b, rwkv_w_o, attn_w_qkv, attn_w_o)
    y_prompt = encoder_trunk(x_prompt, *weights)
    y_sample = encoder_trunk(x_sample, *weights)
    return (y_prompt, y_sample)
```

```python
import functools
import math

import jax
import jax.numpy as jnp
from jax import lax
from jax.experimental import pallas as pl
from jax.experimental.pallas import tpu as pltpu

F32 = jnp.float32
BF16 = jnp.bfloat16

NORM_EPS = 1e-6
GN_EPS = 64e-5
RWKV_HEAD = 64
ATT_HEAD = 128
ROPE_DIM = ATT_HEAD // 4
ROPE_THETA = 500000.0
DILATED_GROUPS = ((128, 1), (512, 4), (2048, 16))
N_GROUPS = len(DILATED_GROUPS)
ATT_HALF = 64
LANES = 128
WKV_CHUNK = 64
VMEM_LIMIT = 56 * 1024 * 1024

_NT = (((1,), (1,)), ((), ()))
_TN = (((0,), (0,)), ((), ()))


def _tile(n, pref, mult=8):
    if n <= pref:
        return n
    t = (pref // mult) * mult
    while t >= mult:
        if n % t == 0:
            return t
        t -= mult
    return n


def _in_list(v, lst):
    r = v == lst[0]
    for s in lst[1:]:
        r = r | (v == s)
    return r


def _rms(x, g):
    return x * lax.rsqrt(jnp.mean(x * x, axis=-1, keepdims=True) + NORM_EPS) * g


def _bdot(a, b):
    return jnp.dot(a.astype(BF16), b.astype(BF16), preferred_element_type=F32)


def _params(sem):
    return pltpu.CompilerParams(dimension_semantics=sem, vmem_limit_bytes=VMEM_LIMIT)


def _ffn_kernel(x_ref, gpre_ref, wg_ref, wu_ref, wd_ref, gpost_ref, o_ref, xn_ref, acc_ref):
    j = pl.program_id(1)

    @pl.when(j == 0)
    def _():
        xn_ref[...] = _rms(x_ref[...], gpre_ref[...]).astype(BF16)
        acc_ref[...] = jnp.zeros_like(acc_ref)

    xn = xn_ref[...]
    g = jnp.dot(xn, wg_ref[...], preferred_element_type=F32)
    u = jnp.dot(xn, wu_ref[...], preferred_element_type=F32)
    h = (g * jax.nn.sigmoid(g)) * u
    acc_ref[...] += jnp.dot(h.astype(BF16), wd_ref[...], preferred_element_type=F32)

    @pl.when(j == pl.num_programs(1) - 1)
    def _():
        o_ref[...] = x_ref[...] + 0.5 * _rms(acc_ref[...], gpost_ref[...])


def _ffn(x, gpre, wg, wu, wd, gpost):
    T, D = x.shape
    F = wg.shape[1]
    tm = _tile(T, 512)
    tf = _tile(F, 512, LANES)
    return pl.pallas_call(
        _ffn_kernel,
        grid=(T // tm, F // tf),
        in_specs=[
            pl.BlockSpec((tm, D), lambda i, j: (i, 0)),
            pl.BlockSpec((1, D), lambda i, j: (0, 0)),
            pl.BlockSpec((D, tf), lambda i, j: (0, j)),
            pl.BlockSpec((D, tf), lambda i, j: (0, j)),
            pl.BlockSpec((tf, D), lambda i, j: (j, 0)),
            pl.BlockSpec((1, D), lambda i, j: (0, 0)),
        ],
        out_specs=pl.BlockSpec((tm, D), lambda i, j: (i, 0)),
        out_shape=jax.ShapeDtypeStruct((T, D), F32),
        scratch_shapes=[pltpu.VMEM((tm, D), BF16), pltpu.VMEM((tm, D), F32)],
        compiler_params=_params(("parallel", "arbitrary")),
    )(x, gpre, wg, wu, wd, gpost)


def _mm_kernel(a_ref, w_ref, o_ref):
    o_ref[...] = jnp.dot(a_ref[...], w_ref[...], preferred_element_type=F32)


def _mm(a, w):
    M, K = a.shape
    N = w.shape[1]
    tm = _tile(M, 1024)
    tn = _tile(N, 512, LANES)
    return pl.pallas_call(
        _mm_kernel,
        grid=(M // tm, N // tn),
        in_specs=[
            pl.BlockSpec((tm, K), lambda i, j: (i, 0)),
            pl.BlockSpec((K, tn), lambda i, j: (0, j)),
        ],
        out_specs=pl.BlockSpec((tm, tn), lambda i, j: (i, j)),
        out_shape=jax.ShapeDtypeStruct((M, N), F32),
        compiler_params=_params(("parallel", "arbitrary")),
    )(a, w)


def _softplus(z):
    return jnp.maximum(z, 0.0) + jnp.log(1.0 + jnp.exp(-jnp.abs(z)))


def _rwkv_pre_kernel(seq_starts, seq_ends, has_vres, tm, *refs):
    (x_ref, xp_ref, xn_ref, gpre_ref, mu_ref, w0_ref, w1_ref, w2_ref,
     a0_ref, a1_ref, a2_ref, g1_ref, g2_ref) = refs[:13]
    refs = refs[13:]
    if has_vres:
        v0_ref, v1_ref, v2_ref = refs[:3]
        refs = refs[3:]
    xr_ref, xk_ref, xv_ref, ld_ref, a_ref, g_ref = refs[:6]

    t0 = pl.program_id(0) * tm
    gpre = gpre_ref[...]
    h = _rms(x_ref[...], gpre)
    keep_p = 1.0 - _in_list(t0, seq_starts).astype(F32)
    keep_n = 1.0 - _in_list(t0 + tm, seq_ends).astype(F32)
    hp = _rms(xp_ref[7:8, :], gpre) * keep_p
    hn = _rms(xn_ref[0:1, :], gpre) * keep_n
    row = lax.broadcasted_iota(jnp.int32, h.shape, 0)
    prev = jnp.where(row == 0, hp, pltpu.roll(h, 1, 0))
    nxt = jnp.where(row == tm - 1, hn, pltpu.roll(h, tm - 1, 0))
    xx = 0.5 * (prev + nxt) - h

    def mix(i):
        return (h + xx * mu_ref[i:i + 1, :]).astype(BF16)

    xr_ref[...] = mix(0)
    xw = mix(1)
    xk_ref[...] = mix(2)
    xv = mix(3)
    xv_ref[...] = xv
    xa = mix(4)
    xg = mix(5)
    for d in range(2):
        tw = jnp.tanh(jnp.dot(xw, w1_ref[d], preferred_element_type=F32))
        z = w0_ref[d:d + 1, :] + _bdot(tw, w2_ref[d])
        ld_ref[d] = -jnp.exp(-_softplus(-z) - 0.5)
        ta = jnp.dot(xa, a1_ref[d], preferred_element_type=F32)
        a_ref[d] = jax.nn.sigmoid(a0_ref[d:d + 1, :] + _bdot(ta, a2_ref[d]))
    tg = jax.nn.sigmoid(jnp.dot(xg, g1_ref[...], preferred_element_type=F32))
    g_ref[...] = _bdot(tg, g2_ref[...])
    if has_vres:
        vg_ref = refs[6]
        tv = jnp.dot(xv, v1_ref[...], preferred_element_type=F32)
        vg_ref[...] = jax.nn.sigmoid(v0_ref[...] + _bdot(tv, v2_ref[...]))


def _pad_lora(w_in, w_out):
    r = w_in.shape[-1]
    rp = -(-r // LANES) * LANES
    pin = [(0, 0)] * (w_in.ndim - 1) + [(0, rp - r)]
    pout = [(0, 0)] * (w_out.ndim - 2) + [(0, rp - r), (0, 0)]
    return jnp.pad(w_in, pin).astype(BF16), jnp.pad(w_out, pout).astype(BF16)


def _rwkv_pre(x, seq_starts, seq_ends, gpre, mu, w0, w1, w2, a0, a1, a2, g1, g2, vres):
    T, D = x.shape
    tm = _tile(T, 128)
    nb8 = T // 8
    has_vres = vres is not None
    w1p, w2p = _pad_lora(w1, w2)
    a1p, a2p = _pad_lora(a1, a2)
    g1p, g2p = _pad_lora(g1, g2)

    def full(a):
        nd = a.ndim
        return pl.BlockSpec(a.shape, lambda i, _nd=nd: (0,) * _nd)

    args = [x, x, x, gpre, mu, w0, w1p, w2p, a0, a1p, a2p, g1p, g2p]
    in_specs = [
        pl.BlockSpec((tm, D), lambda i: (i, 0)),
        pl.BlockSpec((8, D), lambda i: (jnp.maximum(i * (tm // 8) - 1, 0), 0)),
        pl.BlockSpec((8, D), lambda i: (jnp.minimum((i + 1) * (tm // 8), nb8 - 1), 0)),
    ] + [full(a) for a in args[3:]]
    if has_vres:
        v0, v1, v2 = vres
        v1p, v2p = _pad_lora(v1, v2)
        extra = [v0.reshape(1, D), v1p, v2p]
        args += extra
        in_specs += [full(a) for a in extra]
    tok = pl.BlockSpec((tm, D), lambda i: (i, 0))
    tok2 = pl.BlockSpec((2, tm, D), lambda i: (0, i, 0))
    out_specs = [tok, tok, tok, tok2, tok2, tok]
    out_shape = [jax.ShapeDtypeStruct((T, D), BF16)] * 3 + [
        jax.ShapeDtypeStruct((2, T, D), F32), jax.ShapeDtypeStruct((2, T, D), F32),
        jax.ShapeDtypeStruct((T, D), F32)]
    if has_vres:
        out_specs.append(tok)
        out_shape.append(jax.ShapeDtypeStruct((T, D), F32))
    return pl.pallas_call(
        functools.partial(_rwkv_pre_kernel, seq_starts, seq_ends, has_vres, tm),
        grid=(T // tm,),
        in_specs=in_specs,
        out_specs=out_specs,
        out_shape=out_shape,
        compiler_params=_params(("parallel",)),
    )(*args)


def _wkv_kernel(L, G, nc, resets_f, resets_r, has_vres, *refs):
    r_ref, k_ref, v_ref, ld_ref, a_ref, kk_ref, ka_ref = refs[:7]
    refs = refs[7:]
    if has_vres:
        vf_ref, vg_ref = refs[:2]
        refs = refs[2:]
    y_ref, s_ref = refs

    d = pl.program_id(0)
    c = pl.program_id(2)
    ci = c + d * (nc - 1 - 2 * c)
    reset = ((d == 0) & _in_list(ci, resets_f)) | ((d == 1) & _in_list(ci, resets_r))

    @pl.when(reset)
    def _():
        s_ref[...] = jnp.zeros_like(s_ref)

    sgn = 1 - 2 * d
    tt = lax.broadcasted_iota(jnp.int32, (L, L), 0)
    ss = lax.broadcasted_iota(jnp.int32, (L, L), 1)
    delta = sgn * (tt - ss)
    incl = delta >= 0
    strict = delta > 0
    tri = incl.astype(F32)
    eye = (tt == ss).astype(F32)
    lane = lax.broadcasted_iota(jnp.int32, (1, LANES), 1)
    head_masks = [(lane < RWKV_HEAD).astype(F32), (lane >= RWKV_HEAD).astype(F32)]
    bi = lax.broadcasted_iota(jnp.int32, (LANES, LANES), 0) // RWKV_HEAD
    bj = lax.broadcasted_iota(jnp.int32, (LANES, LANES), 1) // RWKV_HEAD
    same_head = bi == bj
    head_ones = same_head.astype(BF16)
    n_double = int(math.log2(L)) - 1

    for p in range(G):
        sl = slice(p * LANES, (p + 1) * LANES)
        r_ = r_ref[:, sl]
        k_ = k_ref[:, sl]
        v_ = v_ref[:, sl]
        ld_ = ld_ref[:, sl]
        a_ = a_ref[:, sl]
        if has_vres:
            v_ = v_ + (vf_ref[:, sl] - v_) * vg_ref[:, sl]
        kkc = k_ * kk_ref[:, sl]
        n2 = jnp.dot((kkc * kkc).astype(BF16), head_ones, preferred_element_type=F32)
        kkn = kkc / jnp.maximum(jnp.sqrt(n2), 1e-12)
        kd = k_ * (1.0 + (a_ - 1.0) * ka_ref[:, sl])
        cs = jnp.dot(tri, ld_, preferred_element_type=F32, precision=lax.Precision.HIGHEST)
        w_in = jnp.exp(cs)
        w_ex = jnp.exp(cs - ld_)
        w_inv = jnp.exp(-cs)
        at = -kkn * w_ex
        rt = r_ * w_in
        bt = kkn * a_ * w_inv
        kt = kd * w_inv
        btb = bt.astype(BF16)
        ktb = kt.astype(BF16)
        wend = jnp.exp(jnp.sum(ld_, axis=0, keepdims=True))
        st = s_ref[p]
        stb = st.astype(BF16)
        u_tot = jnp.zeros((L, LANES), F32)
        y_tot = jnp.zeros((L, LANES), F32)
        for hm in head_masks:
            atm = (at * hm).astype(BF16)
            rtm = (rt * hm).astype(BF16)
            vm = (v_ * hm).astype(BF16)
            a_ab = jnp.where(strict, lax.dot_general(atm, btb, _NT, preferred_element_type=F32), 0.0)
            a_ak = jnp.where(strict, lax.dot_general(atm, ktb, _NT, preferred_element_type=F32), 0.0)
            a_rb = jnp.where(incl, lax.dot_general(rtm, btb, _NT, preferred_element_type=F32), 0.0)
            a_rk = jnp.where(incl, lax.dot_general(rtm, ktb, _NT, preferred_element_type=F32), 0.0)
            tinv = eye + a_ab
            ak = a_ab
            for _ in range(n_double):
                ak = _bdot(ak, ak)
                tinv = tinv + _bdot(ak, tinv)
            rhs = lax.dot_general(atm, stb, _NT, preferred_element_type=F32) + _bdot(a_ak, vm)
            u = _bdot(tinv, rhs)
            y = (lax.dot_general(rtm, stb, _NT, preferred_element_type=F32)
                 + _bdot(a_rb, u) + _bdot(a_rk, vm))
            u_tot = u_tot + u
            y_tot = y_tot + y
        y_ref[:, sl] = y_tot
        uv = jnp.concatenate([u_tot, v_], axis=0).astype(BF16)
        bk = jnp.concatenate([bt, kt], axis=0).astype(BF16)
        upd = lax.dot_general(uv, bk, _TN, preferred_element_type=F32)
        s_ref[p] = (st + jnp.where(same_head, upd, 0.0)) * wend


def _wkv(r, k, v, ld, a, k_k, k_a, vres, seq_starts, seq_ends):
    T, D = r.shape
    L = WKV_CHUNK
    npairs = D // LANES
    G = 4 if npairs % 4 == 0 else (2 if npairs % 2 == 0 else 1)
    nc = T // L
    W = G * LANES
    resets_f = tuple(s // L for s in seq_starts)
    resets_r = tuple(e // L - 1 for e in seq_ends)
    has_vres = vres is not None

    def cidx(d, c):
        return c + d * (nc - 1 - 2 * c)

    tok = pl.BlockSpec((L, W), lambda d, p, c: (cidx(d, c), p))
    tokd = pl.BlockSpec((None, L, W), lambda d, p, c: (d, cidx(d, c), p))
    par = pl.BlockSpec((1, W), lambda d, p, c: (0, p))
    args = [r, k, v, ld, a, k_k, k_a]
    in_specs = [tok, tok, tok, tokd, tokd, par, par]
    if has_vres:
        args += list(vres)
        in_specs += [tok, tok]
    return pl.pallas_call(
        functools.partial(_wkv_kernel, L, G, nc, resets_f, resets_r, has_vres),
        grid=(2, npairs // G, nc),
        in_specs=in_specs,
        out_specs=tokd,
        out_shape=jax.ShapeDtypeStruct((2, T, D), F32),
        scratch_shapes=[pltpu.VMEM((G, LANES, LANES), F32)],
        compiler_params=_params(("arbitrary", "arbitrary", "arbitrary")),
    )(*args)


def _rwkv_out_kernel(has_vres, *refs):
    (x_ref, y_ref, r_ref, k_ref, v_ref, a_ref, g_ref, ka_ref, rk_ref, gnw_ref, gnb_ref,
     wo_ref, gpost_ref) = refs[:13]
    refs = refs[13:]
    if has_vres:
        vf_ref, vg_ref = refs[:2]
        refs = refs[2:]
    o_ref, act_ref = refs
    D = x_ref.shape[1]

    bi = lax.broadcasted_iota(jnp.int32, (LANES, LANES), 0) // RWKV_HEAD
    bj = lax.broadcasted_iota(jnp.int32, (LANES, LANES), 1) // RWKV_HEAD
    head_ones = (bi == bj).astype(BF16)
    inv_n = 1.0 / RWKV_HEAD

    for s in range(D // LANES):
        sl = slice(s * LANES, (s + 1) * LANES)
        y = y_ref[0, :, sl] + y_ref[1, :, sl]
        mean = _bdot(y, head_ones) * inv_n
        dy = y - mean
        var = _bdot(dy * dy, head_ones) * inv_n
        yn = dy * lax.rsqrt(var + GN_EPS) * gnw_ref[:, sl] + gnb_ref[:, sl]
        k_ = k_ref[:, sl]
        v_ = v_ref[:, sl]
        if has_vres:
            v_ = v_ + (vf_ref[:, sl] - v_) * vg_ref[:, sl]
        a_mean = 0.5 * (a_ref[0, :, sl] + a_ref[1, :, sl])
        k_mean = k_ * (1.0 + (a_mean - 1.0) * ka_ref[:, sl])
        bonus = _bdot(r_ref[:, sl] * k_mean * rk_ref[:, sl], head_ones) * v_
        act_ref[:, sl] = ((yn + bonus) * g_ref[:, sl]).astype(BF16)
    h = jnp.dot(act_ref[...], wo_ref[...], preferred_element_type=F32)
    o_ref[...] = x_ref[...] + _rms(h, gpost_ref[...])


def _rwkv_out(x, y, r, k, v, a, g, k_a, r_k, gn_w, gn_b, w_o, gpost, vres):
    T, D = x.shape
    tm = _tile(T, 128)
    has_vres = vres is not None
    tok = pl.BlockSpec((tm, D), lambda i: (i, 0))
    tok2 = pl.BlockSpec((2, tm, D), lambda i: (0, i, 0))
    par = pl.BlockSpec((1, D), lambda i: (0, 0))
    args = [x, y, r, k, v, a, g, k_a, r_k, gn_w, gn_b, w_o, gpost]
    in_specs = [tok, tok2, tok, tok, tok, tok2, tok, par, par, par, par,
                pl.BlockSpec((D, D), lambda i: (0, 0)), par]
    if has_vres:
        args += list(vres)
        in_specs += [tok, tok]
    return pl.pallas_call(
        functools.partial(_rwkv_out_kernel, has_vres),
        grid=(T // tm,),
        in_specs=in_specs,
        out_specs=tok,
        out_shape=jax.ShapeDtypeStruct((T, D), F32),
        scratch_shapes=[pltpu.VMEM((tm, D), BF16)],
        compiler_params=_params(("parallel",)),
    )(*args)


def _qkv_kernel(D, tn, x_ref, gpre_ref, w_ref, cos_ref, sa_ref, sb_ref, o_ref, xn_ref):
    j = pl.program_id(1)

    @pl.when(j == 0)
    def _():
        xn_ref[...] = _rms(x_ref[...], gpre_ref[...]).astype(BF16)

    acc = jnp.dot(xn_ref[...], w_ref[...], preferred_element_type=F32)
    kind = ((j * tn) // D) % 3

    @pl.when(kind < 2)
    def _():
        cos = cos_ref[...]
        sa = sa_ref[...]
        sb = sb_ref[...]
        for s in range(tn // ATT_HEAD):
            sl = slice(s * ATT_HEAD, (s + 1) * ATT_HEAD)
            t = acc[:, sl]
            half = ROPE_DIM // 2
            rot = (t * cos + pltpu.roll(t, half, 1) * sa
                   + pltpu.roll(t, ATT_HEAD - half, 1) * sb)
            o_ref[:, sl] = rot.astype(BF16)

    @pl.when(kind == 2)
    def _():
        o_ref[...] = acc.astype(BF16)


def _rope_tables(pos):
    half = ROPE_DIM // 2
    inv = ROPE_THETA ** (-jnp.arange(half, dtype=F32) / half)
    ang = pos.astype(F32)[:, None] * inv[None, :]
    cos, sin = jnp.cos(ang), jnp.sin(ang)
    T = pos.shape[0]
    ones = jnp.ones((T, ATT_HEAD - ROPE_DIM), F32)
    zeros = jnp.zeros((T, ATT_HEAD - ROPE_DIM), F32)
    zh = jnp.zeros((T, half), F32)
    cos_t = jnp.concatenate([cos, cos, ones], axis=1)
    sa_t = jnp.concatenate([zh, sin, zeros], axis=1)
    sb_t = jnp.concatenate([-sin, zh, zeros], axis=1)
    return cos_t, sa_t, sb_t


def _qkv(x, gpre, w, tables):
    T, D = x.shape
    N = w.shape[1]
    tm = _tile(T, 1024)
    tn = _tile(D, 512, LANES)
    tab = pl.BlockSpec((tm, ATT_HEAD), lambda i, j: (i, 0))
    return pl.pallas_call(
        functools.partial(_qkv_kernel, D, tn),
        grid=(T // tm, N // tn),
        in_specs=[
            pl.BlockSpec((tm, D), lambda i, j: (i, 0)),
            pl.BlockSpec((1, D), lambda i, j: (0, 0)),
            pl.BlockSpec((D, tn), lambda i, j: (0, j)),
            tab, tab, tab,
        ],
        out_specs=pl.BlockSpec((tm, tn), lambda i, j: (i, j)),
        out_shape=jax.ShapeDtypeStruct((T, N), BF16),
        scratch_shapes=[pltpu.VMEM((tm, D), BF16)],
        compiler_params=_params(("parallel", "arbitrary")),
    )(x, gpre, w, *tables)


def _attn_kernel(bq, n_heads, bounds, q_ref, kp_ref, kc_ref, kn_ref, vp_ref, vc_ref, vn_ref,
                 o_ref, lse_ref):
    row0 = pl.program_id(0) * bq
    lo = jnp.int32(bounds[0])
    hi = jnp.int32(bounds[-1])
    for b in bounds[1:-1]:
        lo = jnp.where(row0 >= b, b, lo)
    for b in reversed(bounds[1:-1]):
        hi = jnp.where(row0 < b, b, hi)
    nk = bq + 2 * ATT_HALF
    qrow = row0 + lax.broadcasted_iota(jnp.int32, (bq, nk), 0)
    krow = row0 - ATT_HALF + lax.broadcasted_iota(jnp.int32, (bq, nk), 1)
    valid = (jnp.abs(krow - qrow) <= ATT_HALF) & (krow >= lo) & (krow < hi)
    scale = ATT_HEAD ** -0.5
    for h in range(n_heads):
        sl = slice(h * ATT_HEAD, (h + 1) * ATT_HEAD)
        q = q_ref[:, sl]
        kcat = jnp.concatenate([kp_ref[bq - ATT_HALF:, sl], kc_ref[:, sl], kn_ref[:ATT_HALF, sl]], axis=0)
        vcat = jnp.concatenate([vp_ref[bq - ATT_HALF:, sl], vc_ref[:, sl], vn_ref[:ATT_HALF, sl]], axis=0)
        s = lax.dot_general(q, kcat, _NT, preferred_element_type=F32) * scale
        s = jnp.where(valid, s, -1e30)
        m = jnp.max(s, axis=-1, keepdims=True)
        p = jnp.exp(s - m)
        l = jnp.sum(p, axis=-1, keepdims=True)
        o = jnp.dot(p.astype(BF16), vcat, preferred_element_type=F32) / l
        o_ref[:, sl] = o
        lse_ref[:, sl] = jnp.broadcast_to(m + jnp.log(l), (bq, ATT_HEAD))


def _attn_group(qkv, gi, dil, D, seq_starts, seq_ends):
    T = qkv.shape[0]
    rows = T // dil
    ncol = qkv.shape[1] // D
    view = qkv.reshape(rows, dil * qkv.shape[1])
    bounds = tuple(s // dil for s in seq_starts) + (seq_ends[-1] // dil,)
    seg = [b - a for a, b in zip(bounds[:-1], bounds[1:])]
    bq = 128
    for s in seg:
        bq = math.gcd(bq, s)
    assert bq >= ATT_HALF, "sequence too short for the dilated band blocks"
    nb = rows // bq

    def spec(kind, shift):
        def imap(i, c):
            blk = jnp.clip(i + shift, 0, nb - 1)
            return (blk, c * ncol + gi * 3 + kind)
        return pl.BlockSpec((bq, D), imap)

    out = pl.BlockSpec((bq, D), lambda i, c: (i, c))
    o, lse = pl.pallas_call(
        functools.partial(_attn_kernel, bq, D // ATT_HEAD, bounds),
        grid=(nb, dil),
        in_specs=[spec(0, 0), spec(1, -1), spec(1, 0), spec(1, 1),
                  spec(2, -1), spec(2, 0), spec(2, 1)],
        out_specs=[out, out],
        out_shape=[jax.ShapeDtypeStruct((rows, dil * D), F32)] * 2,
        compiler_params=_params(("parallel", "arbitrary")),
    )(view, view, view, view, view, view, view)
    return o.reshape(T, D), lse.reshape(T, D)


def _attn_out_kernel(x_ref, o0_ref, o1_ref, o2_ref, l0_ref, l1_ref, l2_ref, wo_ref, gpost_ref, out_ref):
    l0, l1, l2 = l0_ref[...], l1_ref[...], l2_ref[...]
    m = jnp.maximum(jnp.maximum(l0, l1), l2)
    e0, e1, e2 = jnp.exp(l0 - m), jnp.exp(l1 - m), jnp.exp(l2 - m)
    o = (e0 * o0_ref[...] + e1 * o1_ref[...] + e2 * o2_ref[...]) / (e0 + e1 + e2)
    h = jnp.dot(o.astype(BF16), wo_ref[...], preferred_element_type=F32)
    out_ref[...] = x_ref[...] + _rms(h, gpost_ref[...])


def _attn_out(x, os_, lses, w_o, gpost):
    T, D = x.shape
    tm = _tile(T, 128)
    tok = pl.BlockSpec((tm, D), lambda i: (i, 0))
    return pl.pallas_call(
        _attn_out_kernel,
        grid=(T // tm,),
        in_specs=[tok] * 7 + [pl.BlockSpec((D, D), lambda i: (0, 0)),
                              pl.BlockSpec((1, D), lambda i: (0, 0))],
        out_specs=tok,
        out_shape=jax.ShapeDtypeStruct((T, D), F32),
        compiler_params=_params(("parallel",)),
    )(x, *os_, *lses, w_o, gpost)


def kernel(x_prompt, x_sample, norm_pre, norm_post, ffn_w_gate, ffn_w_up, ffn_w_down, rwkv_mu, rwkv_w_rkv, rwkv_w0, rwkv_w1, rwkv_w2, rwkv_a0, rwkv_a1, rwkv_a2, rwkv_v0, rwkv_v1, rwkv_v2, rwkv_g1, rwkv_g2, rwkv_k_k, rwkv_k_a, rwkv_r_k, rwkv_gn_w, rwkv_gn_b, rwkv_w_o, attn_w_qkv, attn_w_o):
    D = x_prompt.shape[-1]
    depth = norm_pre.shape[0]
    seqs = [x_prompt.shape[1]] * x_prompt.shape[0] + [x_sample.shape[1]] * x_sample.shape[0]
    seq_starts, seq_ends, t = [], [], 0
    for s in seqs:
        seq_starts.append(t)
        t += s
        seq_ends.append(t)
    seq_starts, seq_ends = tuple(seq_starts), tuple(seq_ends)
    x = jnp.concatenate([x_prompt.reshape(-1, D), x_sample.reshape(-1, D)], axis=0)
    pos = jnp.concatenate([jnp.arange(s, dtype=jnp.int32) for s in seqs])
    tables = _rope_tables(pos)

    def vec(a):
        return a.reshape(1, D)

    v_first = None
    for layer in range(depth):
        x = _ffn(x, vec(norm_pre[layer, 0]), ffn_w_gate[layer, 0].astype(BF16),
                 ffn_w_up[layer, 0].astype(BF16), ffn_w_down[layer, 0].astype(BF16),
                 vec(norm_post[layer, 0]))
        j = layer // 2
        if layer % 2 == 0:
            vres_w = None if j == 0 else (rwkv_v0[j - 1], rwkv_v1[j - 1], rwkv_v2[j - 1])
            pre = _rwkv_pre(x, seq_starts, seq_ends, vec(norm_pre[layer, 1]), rwkv_mu[j],
                            rwkv_w0[j], rwkv_w1[j], rwkv_w2[j], rwkv_a0[j], rwkv_a1[j], rwkv_a2[j],
                            rwkv_g1[j], rwkv_g2[j], vres_w)
            xr, xk, xv, ld, a, g = pre[:6]
            r = _mm(xr, rwkv_w_rkv[j, 0].astype(BF16))
            k = _mm(xk, rwkv_w_rkv[j, 1].astype(BF16))
            v = _mm(xv, rwkv_w_rkv[j, 2].astype(BF16))
            if j == 0:
                v_first, vres = v, None
            else:
                vres = (v_first, pre[6])
            y = _wkv(r, k, v, ld, a, vec(rwkv_k_k[j]), vec(rwkv_k_a[j]), vres, seq_starts, seq_ends)
            x = _rwkv_out(x, y, r, k, v, a, g, vec(rwkv_k_a[j]), vec(rwkv_r_k[j]),
                          vec(rwkv_gn_w[j]), vec(rwkv_gn_b[j]), rwkv_w_o[j].astype(BF16),
                          vec(norm_post[layer, 1]), vres)
        else:
            qkv = _qkv(x, vec(norm_pre[layer, 1]), attn_w_qkv[j].astype(BF16), tables)
            os_, lses = [], []
            for gi, (_, dil) in enumerate(DILATED_GROUPS):
                o, lse = _attn_group(qkv, gi, dil, D, seq_starts, seq_ends)
                os_.append(o)
                lses.append(lse)
            x = _attn_out(x, os_, lses, attn_w_o[j].astype(BF16), vec(norm_post[layer, 1]))
        x = _ffn(x, vec(norm_pre[layer, 2]), ffn_w_gate[layer, 1].astype(BF16),
                 ffn_w_up[layer, 1].astype(BF16), ffn_w_down[layer, 1].astype(BF16),
                 vec(norm_post[layer, 2]))
    n_p = x_prompt.shape[0] * x_prompt.shape[1]
    return (x[:n_p].reshape(x_prompt.shape), x[n_p:].reshape(x_sample.shape))
```

```python
import functools
import math

import jax
import jax.numpy as jnp
from jax import lax
from jax.experimental import pallas as pl
from jax.experimental.pallas import tpu as pltpu

F32 = jnp.float32
BF16 = jnp.bfloat16

NORM_EPS = 1e-6
GN_EPS = 64e-5
RWKV_HEAD = 64
ATT_HEAD = 128
ROPE_DIM = ATT_HEAD // 4
ROPE_THETA = 500000.0
DILATED_GROUPS = ((128, 1), (512, 4), (2048, 16))
N_GROUPS = len(DILATED_GROUPS)
ATT_HALF = 64
LANES = 128
WKV_CHUNK = 64
VMEM_LIMIT = 56 * 1024 * 1024

_NT = (((1,), (1,)), ((), ()))
_TN = (((0,), (0,)), ((), ()))


def _tile(n, pref, mult=8):
    if n <= pref:
        return n
    t = (pref // mult) * mult
    while t >= mult:
        if n % t == 0:
            return t
        t -= mult
    return n


def _in_list(v, lst):
    r = v == lst[0]
    for s in lst[1:]:
        r = r | (v == s)
    return r


def _rms(x, g):
    return x * lax.rsqrt(jnp.mean(x * x, axis=-1, keepdims=True) + NORM_EPS) * g


def _bdot(a, b):
    return jnp.dot(a.astype(BF16), b.astype(BF16), preferred_element_type=F32)


def _ntdot(a, b):
    return lax.dot_general(a, b, _NT, preferred_element_type=F32)


def _params(sem):
    return pltpu.CompilerParams(dimension_semantics=sem, vmem_limit_bytes=VMEM_LIMIT)


def _ffn_kernel(x_ref, gpre_ref, wg_ref, wu_ref, wd_ref, gpost_ref, o_ref, xn_ref, acc_ref):
    j = pl.program_id(1)

    @pl.when(j == 0)
    def _():
        xn_ref[...] = _rms(x_ref[...], gpre_ref[...]).astype(BF16)
        acc_ref[...] = jnp.zeros_like(acc_ref)

    xn = xn_ref[...]
    g = jnp.dot(xn, wg_ref[...], preferred_element_type=F32)
    u = jnp.dot(xn, wu_ref[...], preferred_element_type=F32)
    h = (g * jax.nn.sigmoid(g)) * u
    acc_ref[...] += jnp.dot(h.astype(BF16), wd_ref[...], preferred_element_type=F32)

    @pl.when(j == pl.num_programs(1) - 1)
    def _():
        o_ref[...] = x_ref[...] + 0.5 * _rms(acc_ref[...], gpost_ref[...])


def _ffn(x, gpre, wg, wu, wd, gpost):
    T, D = x.shape
    F = wg.shape[1]
    tm = _tile(T, 512)
    tf = _tile(F, 512, LANES)
    return pl.pallas_call(
        _ffn_kernel,
        grid=(T // tm, F // tf),
        in_specs=[
            pl.BlockSpec((tm, D), lambda i, j: (i, 0)),
            pl.BlockSpec((1, D), lambda i, j: (0, 0)),
            pl.BlockSpec((D, tf), lambda i, j: (0, j)),
            pl.BlockSpec((D, tf), lambda i, j: (0, j)),
            pl.BlockSpec((tf, D), lambda i, j: (j, 0)),
            pl.BlockSpec((1, D), lambda i, j: (0, 0)),
        ],
        out_specs=pl.BlockSpec((tm, D), lambda i, j: (i, 0)),
        out_shape=jax.ShapeDtypeStruct((T, D), F32),
        scratch_shapes=[pltpu.VMEM((tm, D), BF16), pltpu.VMEM((tm, D), F32)],
        compiler_params=_params(("parallel", "arbitrary")),
    )(x, gpre, wg, wu, wd, gpost)


def _mm_kernel(a_ref, w_ref, o_ref):
    o_ref[...] = jnp.dot(a_ref[...], w_ref[...], preferred_element_type=F32)


def _mm(a, w):
    M, K = a.shape
    N = w.shape[1]
    tm = _tile(M, 1024)
    tn = _tile(N, 512, LANES)
    return pl.pallas_call(
        _mm_kernel,
        grid=(M // tm, N // tn),
        in_specs=[
            pl.BlockSpec((tm, K), lambda i, j: (i, 0)),
            pl.BlockSpec((K, tn), lambda i, j: (0, j)),
        ],
        out_specs=pl.BlockSpec((tm, tn), lambda i, j: (i, j)),
        out_shape=jax.ShapeDtypeStruct((M, N), F32),
        compiler_params=_params(("parallel", "arbitrary")),
    )(a, w)


def _softplus(z):
    return jnp.maximum(z, 0.0) + jnp.log(1.0 + jnp.exp(-jnp.abs(z)))


def _rwkv_pre_kernel(seq_starts, seq_ends, has_vres, tm, *refs):
    (x_ref, xp_ref, xn_ref, gpre_ref, mu_ref, w0_ref, w1_ref, w2_ref,
     a0_ref, a1_ref, a2_ref, g1_ref, g2_ref) = refs[:13]
    refs = refs[13:]
    if has_vres:
        v0_ref, v1_ref, v2_ref = refs[:3]
        refs = refs[3:]
    xr_ref, xk_ref, xv_ref, ld_ref, a_ref, g_ref = refs[:6]

    t0 = pl.program_id(0) * tm
    gpre = gpre_ref[...]
    h = _rms(x_ref[...], gpre)
    keep_p = 1.0 - _in_list(t0, seq_starts).astype(F32)
    keep_n = 1.0 - _in_list(t0 + tm, seq_ends).astype(F32)
    hp = _rms(xp_ref[7:8, :], gpre) * keep_p
    hn = _rms(xn_ref[0:1, :], gpre) * keep_n
    row = lax.broadcasted_iota(jnp.int32, h.shape, 0)
    prev = jnp.where(row == 0, hp, pltpu.roll(h, 1, 0))
    nxt = jnp.where(row == tm - 1, hn, pltpu.roll(h, tm - 1, 0))
    xx = 0.5 * (prev + nxt) - h

    def mix(i):
        return (h + xx * mu_ref[i:i + 1, :]).astype(BF16)

    xr_ref[...] = mix(0)
    xw = mix(1)
    xk_ref[...] = mix(2)
    xv = mix(3)
    xv_ref[...] = xv
    xa = mix(4)
    xg = mix(5)
    for d in range(2):
        tw = jnp.tanh(jnp.dot(xw, w1_ref[d], preferred_element_type=F32))
        z = w0_ref[d:d + 1, :] + _bdot(tw, w2_ref[d])
        ld_ref[d] = -jnp.exp(-_softplus(-z) - 0.5)
        ta = jnp.dot(xa, a1_ref[d], preferred_element_type=F32)
        a_ref[d] = jax.nn.sigmoid(a0_ref[d:d + 1, :] + _bdot(ta, a2_ref[d]))
    tg = jax.nn.sigmoid(jnp.dot(xg, g1_ref[...], preferred_element_type=F32))
    g_ref[...] = _bdot(tg, g2_ref[...])
    if has_vres:
        vg_ref = refs[6]
        tv = jnp.dot(xv, v1_ref[...], preferred_element_type=F32)
        vg_ref[...] = jax.nn.sigmoid(v0_ref[...] + _bdot(tv, v2_ref[...]))


def _pad_lora(w_in, w_out):
    r = w_in.shape[-1]
    rp = -(-r // LANES) * LANES
    pin = [(0, 0)] * (w_in.ndim - 1) + [(0, rp - r)]
    pout = [(0, 0)] * (w_out.ndim - 2) + [(0, rp - r), (0, 0)]
    return jnp.pad(w_in, pin).astype(BF16), jnp.pad(w_out, pout).astype(BF16)


def _rwkv_pre(x, seq_starts, seq_ends, gpre, mu, w0, w1, w2, a0, a1, a2, g1, g2, vres):
    T, D = x.shape
    tm = _tile(T, 128)
    nb8 = T // 8
    has_vres = vres is not None
    w1p, w2p = _pad_lora(w1, w2)
    a1p, a2p = _pad_lora(a1, a2)
    g1p, g2p = _pad_lora(g1, g2)

    def full(a):
        nd = a.ndim
        return pl.BlockSpec(a.shape, lambda i, _nd=nd: (0,) * _nd)

    args = [x, x, x, gpre, mu, w0, w1p, w2p, a0, a1p, a2p, g1p, g2p]
    in_specs = [
        pl.BlockSpec((tm, D), lambda i: (i, 0)),
        pl.BlockSpec((8, D), lambda i: (jnp.maximum(i * (tm // 8) - 1, 0), 0)),
        pl.BlockSpec((8, D), lambda i: (jnp.minimum((i + 1) * (tm // 8), nb8 - 1), 0)),
    ] + [full(a) for a in args[3:]]
    if has_vres:
        v0, v1, v2 = vres
        v1p, v2p = _pad_lora(v1, v2)
        extra = [v0.reshape(1, D), v1p, v2p]
        args += extra
        in_specs += [full(a) for a in extra]
    tok = pl.BlockSpec((tm, D), lambda i: (i, 0))
    tok2 = pl.BlockSpec((2, tm, D), lambda i: (0, i, 0))
    out_specs = [tok, tok, tok, tok2, tok2, tok]
    out_shape = [jax.ShapeDtypeStruct((T, D), BF16)] * 3 + [
        jax.ShapeDtypeStruct((2, T, D), F32), jax.ShapeDtypeStruct((2, T, D), F32),
        jax.ShapeDtypeStruct((T, D), F32)]
    if has_vres:
        out_specs.append(tok)
        out_shape.append(jax.ShapeDtypeStruct((T, D), F32))
    return pl.pallas_call(
        functools.partial(_rwkv_pre_kernel, seq_starts, seq_ends, has_vres, tm),
        grid=(T // tm,),
        in_specs=in_specs,
        out_specs=out_specs,
        out_shape=out_shape,
        compiler_params=_params(("parallel",)),
    )(*args)


def _wkv_kernel(L, G, nc, resets_f, resets_r, has_vres, *refs):
    r_ref, k_ref, v_ref, ld_ref, a_ref, kk_ref, ka_ref = refs[:7]
    refs = refs[7:]
    if has_vres:
        vf_ref, vg_ref = refs[:2]
        refs = refs[2:]
    y_ref, s_ref = refs

    d = pl.program_id(0)
    c = pl.program_id(2)
    ci = c + d * (nc - 1 - 2 * c)
    reset = ((d == 0) & _in_list(ci, resets_f)) | ((d == 1) & _in_list(ci, resets_r))

    @pl.when(reset)
    def _():
        s_ref[...] = jnp.zeros_like(s_ref)

    sgn = 1 - 2 * d
    tt = lax.broadcasted_iota(jnp.int32, (L, L), 0)
    ss = lax.broadcasted_iota(jnp.int32, (L, L), 1)
    delta = sgn * (tt - ss)
    incl = delta >= 0
    strict = delta > 0
    tri = incl.astype(F32)
    eye = (tt == ss).astype(F32)
    lane = lax.broadcasted_iota(jnp.int32, (1, LANES), 1)
    head_masks = [(lane < RWKV_HEAD).astype(F32), (lane >= RWKV_HEAD).astype(F32)]
    bi = lax.broadcasted_iota(jnp.int32, (LANES, LANES), 0) // RWKV_HEAD
    bj = lax.broadcasted_iota(jnp.int32, (LANES, LANES), 1) // RWKV_HEAD
    same_head = bi == bj
    head_ones = same_head.astype(BF16)
    n_double = int(math.log2(L)) - 1

    pairs = []
    for p in range(G):
        sl = slice(p * LANES, (p + 1) * LANES)
        r_ = r_ref[:, sl]
        k_ = k_ref[:, sl]
        v_ = v_ref[:, sl]
        ld_ = ld_ref[:, sl]
        a_ = a_ref[:, sl]
        if has_vres:
            v_ = v_ + (vf_ref[:, sl] - v_) * vg_ref[:, sl]
        kkc = k_ * kk_ref[:, sl]
        n2 = jnp.dot((kkc * kkc).astype(BF16), head_ones, preferred_element_type=F32)
        kkn = kkc / jnp.maximum(jnp.sqrt(n2), 1e-12)
        kd = k_ * (1.0 + (a_ - 1.0) * ka_ref[:, sl])
        cs = jnp.dot(tri, ld_, preferred_element_type=F32, precision=lax.Precision.HIGHEST)
        w_inv = jnp.exp(-cs)
        pairs.append(dict(
            sl=sl, v=v_, at=-kkn * jnp.exp(cs - ld_), rt=r_ * jnp.exp(cs),
            bt=(kkn * a_ * w_inv).astype(BF16), kt=(kd * w_inv).astype(BF16),
            wend=jnp.exp(jnp.sum(ld_, axis=0, keepdims=True)), st=s_ref[p]))

    heads = []
    for pr in pairs:
        stb = pr["st"].astype(BF16)
        for hm in head_masks:
            heads.append(dict(pr=pr, stb=stb, atm=(pr["at"] * hm).astype(BF16),
                              rtm=(pr["rt"] * hm).astype(BF16), vm=(pr["v"] * hm).astype(BF16)))
    for hd in heads:
        pr = hd["pr"]
        hd["a_ab"] = jnp.where(strict, _ntdot(hd["atm"], pr["bt"]), 0.0)
        hd["a_ak"] = jnp.where(strict, _ntdot(hd["atm"], pr["kt"]), 0.0)
        hd["a_rb"] = jnp.where(incl, _ntdot(hd["rtm"], pr["bt"]), 0.0)
        hd["a_rk"] = jnp.where(incl, _ntdot(hd["rtm"], pr["kt"]), 0.0)
        hd["tinv"] = eye + hd["a_ab"]
        hd["ak"] = hd["a_ab"]
    for _ in range(n_double):
        for hd in heads:
            hd["ak"] = _bdot(hd["ak"], hd["ak"])
        for hd in heads:
            hd["tinv"] = hd["tinv"] + _bdot(hd["ak"], hd["tinv"])
    for hd in heads:
        hd["rhs"] = _ntdot(hd["atm"], hd["stb"]) + _bdot(hd["a_ak"], hd["vm"])
        hd["y"] = _ntdot(hd["rtm"], hd["stb"]) + _bdot(hd["a_rk"], hd["vm"])
    for hd in heads:
        hd["u"] = _bdot(hd["tinv"], hd["rhs"])
    for hd in heads:
        hd["y"] = hd["y"] + _bdot(hd["a_rb"], hd["u"])
    for p, pr in enumerate(pairs):
        h0, h1 = heads[2 * p], heads[2 * p + 1]
        y_ref[:, pr["sl"]] = h0["y"] + h1["y"]
        uv = jnp.concatenate([h0["u"] + h1["u"], pr["v"]], axis=0).astype(BF16)
        bk = jnp.concatenate([pr["bt"], pr["kt"]], axis=0)
        upd = lax.dot_general(uv, bk, _TN, preferred_element_type=F32)
        s_ref[p] = (pr["st"] + jnp.where(same_head, upd, 0.0)) * pr["wend"]


def _wkv(r, k, v, ld, a, k_k, k_a, vres, seq_starts, seq_ends):
    T, D = r.shape
    L = WKV_CHUNK
    npairs = D // LANES
    G = 4 if npairs % 4 == 0 else (2 if npairs % 2 == 0 else 1)
    nc = T // L
    W = G * LANES
    resets_f = tuple(s // L for s in seq_starts)
    resets_r = tuple(e // L - 1 for e in seq_ends)
    has_vres = vres is not None

    def cidx(d, c):
        return c + d * (nc - 1 - 2 * c)

    tok = pl.BlockSpec((L, W), lambda d, p, c: (cidx(d, c), p))
    tokd = pl.BlockSpec((None, L, W), lambda d, p, c: (d, cidx(d, c), p))
    par = pl.BlockSpec((1, W), lambda d, p, c: (0, p))
    args = [r, k, v, ld, a, k_k, k_a]
    in_specs = [tok, tok, tok, tokd, tokd, par, par]
    if has_vres:
        args += list(vres)
        in_specs += [tok, tok]
    return pl.pallas_call(
        functools.partial(_wkv_kernel, L, G, nc, resets_f, resets_r, has_vres),
        grid=(2, npairs // G, nc),
        in_specs=in_specs,
        out_specs=tokd,
        out_shape=jax.ShapeDtypeStruct((2, T, D), F32),
        scratch_shapes=[pltpu.VMEM((G, LANES, LANES), F32)],
        compiler_params=_params(("arbitrary", "arbitrary", "arbitrary")),
    )(*args)


def _rwkv_out_kernel(has_vres, *refs):
    (x_ref, y_ref, r_ref, k_ref, v_ref, a_ref, g_ref, ka_ref, rk_ref, gnw_ref, gnb_ref,
     wo_ref, gpost_ref) = refs[:13]
    refs = refs[13:]
    if has_vres:
        vf_ref, vg_ref = refs[:2]
        refs = refs[2:]
    o_ref, act_ref = refs
    D = x_ref.shape[1]

    bi = lax.broadcasted_iota(jnp.int32, (LANES, LANES), 0) // RWKV_HEAD
    bj = lax.broadcasted_iota(jnp.int32, (LANES, LANES), 1) // RWKV_HEAD
    head_ones = (bi == bj).astype(BF16)
    inv_n = 1.0 / RWKV_HEAD

    for s in range(D // LANES):
        sl = slice(s * LANES, (s + 1) * LANES)
        y = y_ref[0, :, sl] + y_ref[1, :, sl]
        mean = _bdot(y, head_ones) * inv_n
        dy = y - mean
        var = _bdot(dy * dy, head_ones) * inv_n
        yn = dy * lax.rsqrt(var + GN_EPS) * gnw_ref[:, sl] + gnb_ref[:, sl]
        k_ = k_ref[:, sl]
        v_ = v_ref[:, sl]
        if has_vres:
            v_ = v_ + (vf_ref[:, sl] - v_) * vg_ref[:, sl]
        a_mean = 0.5 * (a_ref[0, :, sl] + a_ref[1, :, sl])
        k_mean = k_ * (1.0 + (a_mean - 1.0) * ka_ref[:, sl])
        bonus = _bdot(r_ref[:, sl] * k_mean * rk_ref[:, sl], head_ones) * v_
        act_ref[:, sl] = ((yn + bonus) * g_ref[:, sl]).astype(BF16)
    h = jnp.dot(act_ref[...], wo_ref[...], preferred_element_type=F32)
    o_ref[...] = x_ref[...] + _rms(h, gpost_ref[...])


def _rwkv_out(x, y, r, k, v, a, g, k_a, r_k, gn_w, gn_b, w_o, gpost, vres):
    T, D = x.shape
    tm = _tile(T, 128)
    has_vres = vres is not None
    tok = pl.BlockSpec((tm, D), lambda i: (i, 0))
    tok2 = pl.BlockSpec((2, tm, D), lambda i: (0, i, 0))
    par = pl.BlockSpec((1, D), lambda i: (0, 0))
    args = [x, y, r, k, v, a, g, k_a, r_k, gn_w, gn_b, w_o, gpost]
    in_specs = [tok, tok2, tok, tok, tok, tok2, tok, par, par, par, par,
                pl.BlockSpec((D, D), lambda i: (0, 0)), par]
    if has_vres:
        args += list(vres)
        in_specs += [tok, tok]
    return pl.pallas_call(
        functools.partial(_rwkv_out_kernel, has_vres),
        grid=(T // tm,),
        in_specs=in_specs,
        out_specs=tok,
        out_shape=jax.ShapeDtypeStruct((T, D), F32),
        scratch_shapes=[pltpu.VMEM((tm, D), BF16)],
        compiler_params=_params(("parallel",)),
    )(*args)


def _qkv_kernel(D, tm, tn, dil, x_ref, gpre_ref, w_ref, cos_ref, sa_ref, sb_ref, o_ref, xn_ref, acc_ref):
    j = pl.program_id(1)

    @pl.when(j == 0)
    def _():
        xn_ref[...] = _rms(x_ref[...], gpre_ref[...]).astype(BF16)

    acc = jnp.dot(xn_ref[...], w_ref[...], preferred_element_type=F32)
    kind = (j * tn) // D

    heads = [slice(s * ATT_HEAD, (s + 1) * ATT_HEAD) for s in range(tn // ATT_HEAD)]

    @pl.when(kind < 2)
    def _():
        cos = cos_ref[...]
        sa = sa_ref[...]
        sb = sb_ref[...]
        half = ROPE_DIM // 2
        for s, sl in enumerate(heads):
            t = acc[:, sl]
            acc_ref[s] = (t * cos + pltpu.roll(t, half, 1) * sa
                          + pltpu.roll(t, ATT_HEAD - half, 1) * sb)

    @pl.when(kind == 2)
    def _():
        for s, sl in enumerate(heads):
            acc_ref[s] = acc[:, sl]

    for s, sl in enumerate(heads):
        if dil == 1:
            o_ref[0, :, sl] = acc_ref[s].astype(BF16)
        else:
            for c in range(dil):
                o_ref[c, :, sl] = acc_ref[s, pl.ds(c, tm // dil, stride=dil), :].astype(BF16)


def _rope_tables(pos):
    half = ROPE_DIM // 2
    inv = ROPE_THETA ** (-jnp.arange(half, dtype=F32) / half)
    ang = pos.astype(F32)[:, None] * inv[None, :]
    cos, sin = jnp.cos(ang), jnp.sin(ang)
    T = pos.shape[0]
    ones = jnp.ones((T, ATT_HEAD - ROPE_DIM), F32)
    zeros = jnp.zeros((T, ATT_HEAD - ROPE_DIM), F32)
    zh = jnp.zeros((T, half), F32)
    cos_t = jnp.concatenate([cos, cos, ones], axis=1)
    sa_t = jnp.concatenate([zh, sin, zeros], axis=1)
    sb_t = jnp.concatenate([-sin, zh, zeros], axis=1)
    return cos_t, sa_t, sb_t


def _qkv(x, gpre, w, tables, gi, dil):
    T, D = x.shape
    tm = _tile(T, 1024, 16 * dil)
    tn = _tile(D, 512, LANES)
    nj = 3 * D // tn
    tab = pl.BlockSpec((tm, ATT_HEAD), lambda i, j: (i, 0))
    return pl.pallas_call(
        functools.partial(_qkv_kernel, D, tm, tn, dil),
        grid=(T // tm, nj),
        in_specs=[
            pl.BlockSpec((tm, D), lambda i, j: (i, 0)),
            pl.BlockSpec((1, D), lambda i, j: (0, 0)),
            pl.BlockSpec((D, tn), lambda i, j: (0, gi * nj + j)),
            tab, tab, tab,
        ],
        out_specs=pl.BlockSpec((dil, tm // dil, tn), lambda i, j: (0, i, j)),
        out_shape=jax.ShapeDtypeStruct((dil, T // dil, 3 * D), BF16),
        scratch_shapes=[pltpu.VMEM((tm, D), BF16), pltpu.VMEM((tn // ATT_HEAD, tm, ATT_HEAD), F32)],
        compiler_params=_params(("parallel", "arbitrary")),
    )(x, gpre, w, *tables)


def _attn_kernel(bq, n_heads, bounds, q_ref, kp_ref, kc_ref, kn_ref, vp_ref, vc_ref, vn_ref,
                 o_ref, lse_ref):
    row0 = pl.program_id(0) * bq
    lo = jnp.int32(bounds[0])
    hi = jnp.int32(bounds[-1])
    for b in bounds[1:-1]:
        lo = jnp.where(row0 >= b, b, lo)
    for b in reversed(bounds[1:-1]):
        hi = jnp.where(row0 < b, b, hi)
    nk = bq + 2 * ATT_HALF
    qrow = row0 + lax.broadcasted_iota(jnp.int32, (bq, nk), 0)
    krow = row0 - ATT_HALF + lax.broadcasted_iota(jnp.int32, (bq, nk), 1)
    valid = (jnp.abs(krow - qrow) <= ATT_HALF) & (krow >= lo) & (krow < hi)
    scale = ATT_HEAD ** -0.5
    for h in range(n_heads):
        sl = slice(h * ATT_HEAD, (h + 1) * ATT_HEAD)
        q = q_ref[:, sl]
        kcat = jnp.concatenate([kp_ref[bq - ATT_HALF:, sl], kc_ref[:, sl], kn_ref[:ATT_HALF, sl]], axis=0)
        vcat = jnp.concatenate([vp_ref[bq - ATT_HALF:, sl], vc_ref[:, sl], vn_ref[:ATT_HALF, sl]], axis=0)
        s = lax.dot_general(q, kcat, _NT, preferred_element_type=F32) * scale
        s = jnp.where(valid, s, -1e30)
        m = jnp.max(s, axis=-1, keepdims=True)
        p = jnp.exp(s - m)
        l = jnp.sum(p, axis=-1, keepdims=True)
        o = jnp.dot(p.astype(BF16), vcat, preferred_element_type=F32) / l
        o_ref[:, sl] = o
        lse_ref[:, sl] = jnp.broadcast_to(m + jnp.log(l), (bq, ATT_HEAD))


def _attn_group(qkv, D, seq_starts, seq_ends):
    dil, rows, _ = qkv.shape
    bounds = tuple(s // dil for s in seq_starts) + (seq_ends[-1] // dil,)
    seg = [b - a for a, b in zip(bounds[:-1], bounds[1:])]
    bq = 128
    for s in seg:
        bq = math.gcd(bq, s)
    assert bq >= ATT_HALF, "sequence too short for the dilated band blocks"
    nb = rows // bq

    def spec(kind, shift):
        def imap(i, c):
            return (c, jnp.clip(i + shift, 0, nb - 1), kind)
        return pl.BlockSpec((None, bq, D), imap)

    out = pl.BlockSpec((None, bq, D), lambda i, c: (c, i, 0))
    return pl.pallas_call(
        functools.partial(_attn_kernel, bq, D // ATT_HEAD, bounds),
        grid=(nb, dil),
        in_specs=[spec(0, 0), spec(1, -1), spec(1, 0), spec(1, 1),
                  spec(2, -1), spec(2, 0), spec(2, 1)],
        out_specs=[out, out],
        out_shape=[jax.ShapeDtypeStruct((dil, rows, D), F32)] * 2,
        compiler_params=_params(("parallel", "arbitrary")),
    )(qkv, qkv, qkv, qkv, qkv, qkv, qkv)


def _attn_out_kernel(tm, x_ref, o0_ref, o1_ref, o2_ref, l0_ref, l1_ref, l2_ref, wo_ref, gpost_ref,
                     out_ref, act_ref, *scratch):
    D = x_ref.shape[1]

    def token_major(ref, sl, s):
        dil = ref.shape[0]
        if dil == 1:
            return ref[0, :, sl]
        for c in range(dil):
            s[pl.ds(c, tm // dil, stride=dil), :] = ref[c, :, sl]
        return s[...]

    for k in range(D // LANES):
        sl = slice(k * LANES, (k + 1) * LANES)
        free = list(scratch)
        l0, l1, l2 = [token_major(r, sl, free.pop() if r.shape[0] > 1 else None)
                      for r in (l0_ref, l1_ref, l2_ref)]
        o0, o1, o2 = [token_major(r, sl, free.pop() if r.shape[0] > 1 else None)
                      for r in (o0_ref, o1_ref, o2_ref)]
        m = jnp.maximum(jnp.maximum(l0, l1), l2)
        e0, e1, e2 = jnp.exp(l0 - m), jnp.exp(l1 - m), jnp.exp(l2 - m)
        act_ref[:, sl] = ((e0 * o0 + e1 * o1 + e2 * o2) / (e0 + e1 + e2)).astype(BF16)
    h = jnp.dot(act_ref[...], wo_ref[...], preferred_element_type=F32)
    out_ref[...] = x_ref[...] + _rms(h, gpost_ref[...])


def _attn_out(x, os_, lses, w_o, gpost):
    T, D = x.shape
    tm = _tile(T, 128, 8 * max(o.shape[0] for o in os_))
    tok = pl.BlockSpec((tm, D), lambda i: (i, 0))

    def planes(a):
        dil = a.shape[0]
        return pl.BlockSpec((dil, tm // dil, D), lambda i: (0, i, 0))

    n_scratch = sum(2 for o in os_ if o.shape[0] > 1)
    return pl.pallas_call(
        functools.partial(_attn_out_kernel, tm),
        grid=(T // tm,),
        in_specs=[tok] + [planes(a) for a in os_] + [planes(a) for a in lses]
        + [pl.BlockSpec((D, D), lambda i: (0, 0)), pl.BlockSpec((1, D), lambda i: (0, 0))],
        out_specs=tok,
        out_shape=jax.ShapeDtypeStruct((T, D), F32),
        scratch_shapes=[pltpu.VMEM((tm, D), BF16)] + [pltpu.VMEM((tm, LANES), F32)] * n_scratch,
        compiler_params=_params(("parallel",)),
    )(x, *os_, *lses, w_o, gpost)


def kernel(x_prompt, x_sample, norm_pre, norm_post, ffn_w_gate, ffn_w_up, ffn_w_down, rwkv_mu, rwkv_w_rkv, rwkv_w0, rwkv_w1, rwkv_w2, rwkv_a0, rwkv_a1, rwkv_a2, rwkv_v0, rwkv_v1, rwkv_v2, rwkv_g1, rwkv_g2, rwkv_k_k, rwkv_k_a, rwkv_r_k, rwkv_gn_w, rwkv_gn_b, rwkv_w_o, attn_w_qkv, attn_w_o):
    D = x_prompt.shape[-1]
    depth = norm_pre.shape[0]
    seqs = [x_prompt.shape[1]] * x_prompt.shape[0] + [x_sample.shape[1]] * x_sample.shape[0]
    seq_starts, seq_ends, t = [], [], 0
    for s in seqs:
        seq_starts.append(t)
        t += s
        seq_ends.append(t)
    seq_starts, seq_ends = tuple(seq_starts), tuple(seq_ends)
    x = jnp.concatenate([x_prompt.reshape(-1, D), x_sample.reshape(-1, D)], axis=0)
    pos = jnp.concatenate([jnp.arange(s, dtype=jnp.int32) for s in seqs])
    tables = _rope_tables(pos)

    def vec(a):
        return a.reshape(1, D)

    v_first = None
    for layer in range(depth):
        x = _ffn(x, vec(norm_pre[layer, 0]), ffn_w_gate[layer, 0].astype(BF16),
                 ffn_w_up[layer, 0].astype(BF16), ffn_w_down[layer, 0].astype(BF16),
                 vec(norm_post[layer, 0]))
        j = layer // 2
        if layer % 2 == 0:
            vres_w = None if j == 0 else (rwkv_v0[j - 1], rwkv_v1[j - 1], rwkv_v2[j - 1])
            pre = _rwkv_pre(x, seq_starts, seq_ends, vec(norm_pre[layer, 1]), rwkv_mu[j],
                            rwkv_w0[j], rwkv_w1[j], rwkv_w2[j], rwkv_a0[j], rwkv_a1[j], rwkv_a2[j],
                            rwkv_g1[j], rwkv_g2[j], vres_w)
            xr, xk, xv, ld, a, g = pre[:6]
            r = _mm(xr, rwkv_w_rkv[j, 0].astype(BF16))
            k = _mm(xk, rwkv_w_rkv[j, 1].astype(BF16))
            v = _mm(xv, rwkv_w_rkv[j, 2].astype(BF16))
            if j == 0:
                v_first, vres = v, None
            else:
                vres = (v_first, pre[6])
            y = _wkv(r, k, v, ld, a, vec(rwkv_k_k[j]), vec(rwkv_k_a[j]), vres, seq_starts, seq_ends)
            x = _rwkv_out(x, y, r, k, v, a, g, vec(rwkv_k_a[j]), vec(rwkv_r_k[j]),
                          vec(rwkv_gn_w[j]), vec(rwkv_gn_b[j]), rwkv_w_o[j].astype(BF16),
                          vec(norm_post[layer, 1]), vres)
        else:
            w_qkv = attn_w_qkv[j].astype(BF16)
            os_, lses = [], []
            for gi, (_, dil) in enumerate(DILATED_GROUPS):
                qkv = _qkv(x, vec(norm_pre[layer, 1]), w_qkv, tables, gi, dil)
                o, lse = _attn_group(qkv, D, seq_starts, seq_ends)
                os_.append(o)
                lses.append(lse)
            x = _attn_out(x, os_, lses, attn_w_o[j].astype(BF16), vec(norm_post[layer, 1]))
        x = _ffn(x, vec(norm_pre[layer, 2]), ffn_w_gate[layer, 1].astype(BF16),
                 ffn_w_up[layer, 1].astype(BF16), ffn_w_down[layer, 1].astype(BF16),
                 vec(norm_post[layer, 2]))
    n_p = x_prompt.shape[0] * x_prompt.shape[1]
    return (x[:n_p].reshape(x_prompt.shape), x[n_p:].reshape(x_sample.shape))
```

```python
import functools
import math

import jax
import jax.numpy as jnp
from jax import lax
from jax.experimental import pallas as pl
from jax.experimental.pallas import tpu as pltpu

F32 = jnp.float32
BF16 = jnp.bfloat16

NORM_EPS = 1e-6
GN_EPS = 64e-5
RWKV_HEAD = 64
ATT_HEAD = 128
ROPE_DIM = ATT_HEAD // 4
ROPE_THETA = 500000.0
DILATED_GROUPS = ((128, 1), (512, 4), (2048, 16))
N_GROUPS = len(DILATED_GROUPS)
ATT_HALF = 64
LANES = 128
MXU_COLS = 256
WKV_CHUNK = 64
WKV_PAIRS_PER_STEP = 16
VMEM_LIMIT = 56 * 1024 * 1024

_NT = (((1,), (1,)), ((), ()))
_TN = (((0,), (0,)), ((), ()))


def _tile(n, pref, mult=8):
    if n <= pref:
        return n
    t = (pref // mult) * mult
    while t >= mult:
        if n % t == 0:
            return t
        t -= mult
    return n


def _in_list(v, lst):
    r = v == lst[0]
    for s in lst[1:]:
        r = r | (v == s)
    return r


def _rms(x, g):
    return x * lax.rsqrt(jnp.mean(x * x, axis=-1, keepdims=True) + NORM_EPS) * g


def _bdot(a, b):
    return jnp.dot(a.astype(BF16), b.astype(BF16), preferred_element_type=F32)


def _ntdot(a, b):
    return lax.dot_general(a, b, _NT, preferred_element_type=F32)


def _params(sem):
    return pltpu.CompilerParams(dimension_semantics=sem, vmem_limit_bytes=VMEM_LIMIT)


def _ffn_kernel(x_ref, gpre_ref, wg_ref, wu_ref, wd_ref, gpost_ref, o_ref, xn_ref, acc_ref):
    j = pl.program_id(1)

    @pl.when(j == 0)
    def _():
        xn_ref[...] = _rms(x_ref[...], gpre_ref[...]).astype(BF16)
        acc_ref[...] = jnp.zeros_like(acc_ref)

    xn = xn_ref[...]
    g = jnp.dot(xn, wg_ref[...], preferred_element_type=F32)
    u = jnp.dot(xn, wu_ref[...], preferred_element_type=F32)
    h = (g * jax.nn.sigmoid(g)) * u
    acc_ref[...] += jnp.dot(h.astype(BF16), wd_ref[...], preferred_element_type=F32)

    @pl.when(j == pl.num_programs(1) - 1)
    def _():
        o_ref[...] = x_ref[...] + 0.5 * _rms(acc_ref[...], gpost_ref[...])


def _ffn(x, gpre, wg, wu, wd, gpost):
    T, D = x.shape
    F = wg.shape[1]
    tm = _tile(T, 512)
    tf = _tile(F, 512, LANES)
    return pl.pallas_call(
        _ffn_kernel,
        grid=(T // tm, F // tf),
        in_specs=[
            pl.BlockSpec((tm, D), lambda i, j: (i, 0)),
            pl.BlockSpec((1, D), lambda i, j: (0, 0)),
            pl.BlockSpec((D, tf), lambda i, j: (0, j)),
            pl.BlockSpec((D, tf), lambda i, j: (0, j)),
            pl.BlockSpec((tf, D), lambda i, j: (j, 0)),
            pl.BlockSpec((1, D), lambda i, j: (0, 0)),
        ],
        out_specs=pl.BlockSpec((tm, D), lambda i, j: (i, 0)),
        out_shape=jax.ShapeDtypeStruct((T, D), F32),
        scratch_shapes=[pltpu.VMEM((tm, D), BF16), pltpu.VMEM((tm, D), F32)],
        compiler_params=_params(("parallel", "arbitrary")),
    )(x, gpre, wg, wu, wd, gpost)


def _mm_kernel(a_ref, w_ref, o_ref):
    o_ref[...] = jnp.dot(a_ref[...], w_ref[...], preferred_element_type=F32)


def _mm(a, w):
    M, K = a.shape
    N = w.shape[1]
    tm = _tile(M, 1024)
    tn = _tile(N, 512, LANES)
    return pl.pallas_call(
        _mm_kernel,
        grid=(M // tm, N // tn),
        in_specs=[
            pl.BlockSpec((tm, K), lambda i, j: (i, 0)),
            pl.BlockSpec((K, tn), lambda i, j: (0, j)),
        ],
        out_specs=pl.BlockSpec((tm, tn), lambda i, j: (i, j)),
        out_shape=jax.ShapeDtypeStruct((M, N), F32),
        compiler_params=_params(("parallel", "arbitrary")),
    )(a, w)


def _softplus(z):
    return jnp.maximum(z, 0.0) + jnp.log(1.0 + jnp.exp(-jnp.abs(z)))


def _rwkv_pre_kernel(seq_starts, seq_ends, has_vres, tm, *refs):
    (x_ref, xp_ref, xn_ref, gpre_ref, mu_ref, w0_ref, w1_ref, w2_ref,
     a0_ref, a1_ref, a2_ref, g1_ref, g2_ref) = refs[:13]
    refs = refs[13:]
    if has_vres:
        v0_ref, v1_ref, v2_ref = refs[:3]
        refs = refs[3:]
    xr_ref, xk_ref, xv_ref, ld_ref, a_ref, g_ref = refs[:6]

    t0 = pl.program_id(0) * tm
    gpre = gpre_ref[...]
    h = _rms(x_ref[...], gpre)
    keep_p = 1.0 - _in_list(t0, seq_starts).astype(F32)
    keep_n = 1.0 - _in_list(t0 + tm, seq_ends).astype(F32)
    hp = _rms(xp_ref[7:8, :], gpre) * keep_p
    hn = _rms(xn_ref[0:1, :], gpre) * keep_n
    row = lax.broadcasted_iota(jnp.int32, h.shape, 0)
    prev = jnp.where(row == 0, hp, pltpu.roll(h, 1, 0))
    nxt = jnp.where(row == tm - 1, hn, pltpu.roll(h, tm - 1, 0))
    xx = 0.5 * (prev + nxt) - h

    def mix(i):
        return (h + xx * mu_ref[i:i + 1, :]).astype(BF16)

    xr_ref[...] = mix(0)
    xw = mix(1)
    xk_ref[...] = mix(2)
    xv = mix(3)
    xv_ref[...] = xv
    xa = mix(4)
    xg = mix(5)
    for d in range(2):
        tw = jnp.tanh(jnp.dot(xw, w1_ref[d], preferred_element_type=F32))
        z = w0_ref[d:d + 1, :] + _bdot(tw, w2_ref[d])
        ld_ref[d] = -jnp.exp(-_softplus(-z) - 0.5)
        ta = jnp.dot(xa, a1_ref[d], preferred_element_type=F32)
        a_ref[d] = jax.nn.sigmoid(a0_ref[d:d + 1, :] + _bdot(ta, a2_ref[d]))
    tg = jax.nn.sigmoid(jnp.dot(xg, g1_ref[...], preferred_element_type=F32))
    g_ref[...] = _bdot(tg, g2_ref[...])
    if has_vres:
        vg_ref = refs[6]
        tv = jnp.dot(xv, v1_ref[...], preferred_element_type=F32)
        vg_ref[...] = jax.nn.sigmoid(v0_ref[...] + _bdot(tv, v2_ref[...]))


def _pad_lora(w_in, w_out):
    r = w_in.shape[-1]
    rp = -(-r // LANES) * LANES
    pin = [(0, 0)] * (w_in.ndim - 1) + [(0, rp - r)]
    pout = [(0, 0)] * (w_out.ndim - 2) + [(0, rp - r), (0, 0)]
    return jnp.pad(w_in, pin).astype(BF16), jnp.pad(w_out, pout).astype(BF16)


def _rwkv_pre(x, seq_starts, seq_ends, gpre, mu, w0, w1, w2, a0, a1, a2, g1, g2, vres):
    T, D = x.shape
    tm = _tile(T, 128)
    nb8 = T // 8
    has_vres = vres is not None
    w1p, w2p = _pad_lora(w1, w2)
    a1p, a2p = _pad_lora(a1, a2)
    g1p, g2p = _pad_lora(g1, g2)

    def full(a):
        nd = a.ndim
        return pl.BlockSpec(a.shape, lambda i, _nd=nd: (0,) * _nd)

    args = [x, x, x, gpre, mu, w0, w1p, w2p, a0, a1p, a2p, g1p, g2p]
    in_specs = [
        pl.BlockSpec((tm, D), lambda i: (i, 0)),
        pl.BlockSpec((8, D), lambda i: (jnp.maximum(i * (tm // 8) - 1, 0), 0)),
        pl.BlockSpec((8, D), lambda i: (jnp.minimum((i + 1) * (tm // 8), nb8 - 1), 0)),
    ] + [full(a) for a in args[3:]]
    if has_vres:
        v0, v1, v2 = vres
        v1p, v2p = _pad_lora(v1, v2)
        extra = [v0.reshape(1, D), v1p, v2p]
        args += extra
        in_specs += [full(a) for a in extra]
    tok = pl.BlockSpec((tm, D), lambda i: (i, 0))
    tok2 = pl.BlockSpec((2, tm, D), lambda i: (0, i, 0))
    out_specs = [tok, tok, tok, tok2, tok2, tok]
    out_shape = [jax.ShapeDtypeStruct((T, D), BF16)] * 3 + [
        jax.ShapeDtypeStruct((2, T, D), F32), jax.ShapeDtypeStruct((2, T, D), F32),
        jax.ShapeDtypeStruct((T, D), F32)]
    if has_vres:
        out_specs.append(tok)
        out_shape.append(jax.ShapeDtypeStruct((T, D), F32))
    return pl.pallas_call(
        functools.partial(_rwkv_pre_kernel, seq_starts, seq_ends, has_vres, tm),
        grid=(T // tm,),
        in_specs=in_specs,
        out_specs=out_specs,
        out_shape=out_shape,
        compiler_params=_params(("parallel",)),
    )(*args)


def _wkv_kernel(L, G, nc, resets_f, resets_r, has_vres, *refs):
    r_ref, k_ref, v_ref, ld_ref, a_ref, kk_ref, ka_ref = refs[:7]
    refs = refs[7:]
    if has_vres:
        vf_ref, vg_ref = refs[:2]
        refs = refs[2:]
    y_ref, s_ref = refs

    d = pl.program_id(0)
    c = pl.program_id(2)
    ci = c + d * (nc - 1 - 2 * c)
    reset = ((d == 0) & _in_list(ci, resets_f)) | ((d == 1) & _in_list(ci, resets_r))

    @pl.when(reset)
    def _():
        s_ref[...] = jnp.zeros_like(s_ref)

    sgn = 1 - 2 * d
    tt = lax.broadcasted_iota(jnp.int32, (L, LANES), 0)
    ss = lax.broadcasted_iota(jnp.int32, (L, LANES), 1) % L
    delta = sgn * (tt - ss)
    incl = delta >= 0
    strict = delta > 0
    eye = (tt == ss).astype(F32)
    tri = (sgn * (lax.broadcasted_iota(jnp.int32, (L, L), 0)
                  - lax.broadcasted_iota(jnp.int32, (L, L), 1)) >= 0).astype(BF16)
    first_head = lax.broadcasted_iota(jnp.int32, (L, LANES), 1) < RWKV_HEAD
    bi = lax.broadcasted_iota(jnp.int32, (LANES, LANES), 0) // RWKV_HEAD
    bj = lax.broadcasted_iota(jnp.int32, (LANES, LANES), 1) // RWKV_HEAD
    same_head = bi == bj
    head_ones = same_head.astype(BF16)
    n_double = int(math.log2(L)) - 1

    def bd(x):
        xb = x.astype(BF16)
        zero = jnp.zeros_like(xb)
        return jnp.concatenate([jnp.where(first_head, xb, zero), jnp.where(first_head, zero, xb)], axis=0)

    pairs = []
    for p in range(G):
        sl = slice(p * LANES, (p + 1) * LANES)
        r_ = r_ref[:, sl]
        k_ = k_ref[:, sl]
        v_ = v_ref[:, sl]
        ld_ = ld_ref[:, sl]
        a_ = a_ref[:, sl]
        if has_vres:
            v_ = v_ + (vf_ref[:, sl] - v_) * vg_ref[:, sl]
        kkc = k_ * kk_ref[:, sl]
        n2 = jnp.dot((kkc * kkc).astype(BF16), head_ones, preferred_element_type=F32)
        kkn = kkc / jnp.maximum(jnp.sqrt(n2), 1e-12)
        kd = k_ * (1.0 + (a_ - 1.0) * ka_ref[:, sl])
        ld_hi = ld_.astype(BF16)
        ld_lo = (ld_ - ld_hi.astype(F32)).astype(BF16)
        cs2 = jnp.dot(tri, jnp.concatenate([ld_hi, ld_lo], axis=1), preferred_element_type=F32)
        cs = cs2[:, :LANES] + cs2[:, LANES:]
        w_inv = jnp.exp(-cs)
        bt = kkn * a_ * w_inv
        kt = kd * w_inv
        pairs.append(dict(
            sl=sl, v=v_, bt=bt, kt=kt, st=s_ref[p],
            lhs=jnp.concatenate([-kkn * jnp.exp(cs - ld_), r_ * jnp.exp(cs)], axis=0).astype(BF16),
            wend=jnp.exp(jnp.sum(ld_, axis=0, keepdims=True))))

    for pr in pairs:
        rhs_t = jnp.concatenate([bd(pr["bt"]), bd(pr["kt"]), pr["st"].astype(BF16)], axis=0)
        prod = _ntdot(pr["lhs"], rhs_t)
        pr["a_ab"] = jnp.where(strict, prod[:L, :LANES], 0.0)
        pr["a_rb"] = jnp.where(incl, prod[L:, :LANES], 0.0)
        a_ak = jnp.where(strict, prod[:L, LANES:2 * LANES], 0.0)
        a_rk = jnp.where(incl, prod[L:, LANES:2 * LANES], 0.0)
        pr["sp"] = prod[:, 2 * LANES:]
        pr["a_k"] = jnp.concatenate([a_ak, a_rk], axis=0)
        pr["tinv"] = eye + pr["a_ab"]
        pr["ak"] = pr["a_ab"]
    for pr in pairs:
        pr["spv"] = pr["sp"] + _bdot(pr["a_k"], bd(pr["v"]))
    for _ in range(n_double):
        for pr in pairs:
            pr["ak"] = _bdot(pr["ak"], bd(pr["ak"]))
        for pr in pairs:
            pr["tinv"] = pr["tinv"] + _bdot(pr["ak"], bd(pr["tinv"]))
    for pr in pairs:
        pr["u"] = _bdot(pr["tinv"], bd(pr["spv"][:L]))
    for pr in pairs:
        pr["y"] = pr["spv"][L:] + _bdot(pr["a_rb"], bd(pr["u"]))
    for p, pr in enumerate(pairs):
        y_ref[:, pr["sl"]] = pr["y"]
        uv = jnp.concatenate([pr["u"], pr["v"]], axis=0).astype(BF16)
        bk = jnp.concatenate([pr["bt"], pr["kt"]], axis=0).astype(BF16)
        upd = lax.dot_general(uv, bk, _TN, preferred_element_type=F32)
        s_ref[p] = (pr["st"] + jnp.where(same_head, upd, 0.0)) * pr["wend"]


def _wkv(r, k, v, ld, a, k_k, k_a, vres, seq_starts, seq_ends):
    T, D = r.shape
    L = WKV_CHUNK
    npairs = D // LANES
    assert 2 * L == LANES and RWKV_HEAD == L
    G = math.gcd(npairs, WKV_PAIRS_PER_STEP)
    nc = T // L
    W = G * LANES
    resets_f = tuple(s // L for s in seq_starts)
    resets_r = tuple(e // L - 1 for e in seq_ends)
    has_vres = vres is not None

    def cidx(d, c):
        return c + d * (nc - 1 - 2 * c)

    tok = pl.BlockSpec((L, W), lambda d, p, c: (cidx(d, c), p))
    tokd = pl.BlockSpec((None, L, W), lambda d, p, c: (d, cidx(d, c), p))
    par = pl.BlockSpec((1, W), lambda d, p, c: (0, p))
    args = [r, k, v, ld, a, k_k, k_a]
    in_specs = [tok, tok, tok, tokd, tokd, par, par]
    if has_vres:
        args += list(vres)
        in_specs += [tok, tok]
    return pl.pallas_call(
        functools.partial(_wkv_kernel, L, G, nc, resets_f, resets_r, has_vres),
        grid=(2, npairs // G, nc),
        in_specs=in_specs,
        out_specs=tokd,
        out_shape=jax.ShapeDtypeStruct((2, T, D), F32),
        scratch_shapes=[pltpu.VMEM((G, LANES, LANES), F32)],
        compiler_params=_params(("arbitrary", "arbitrary", "arbitrary")),
    )(*args)


def _rwkv_out_kernel(has_vres, *refs):
    (x_ref, y_ref, r_ref, k_ref, v_ref, a_ref, g_ref, ka_ref, rk_ref, gnw_ref, gnb_ref,
     wo_ref, gpost_ref) = refs[:13]
    refs = refs[13:]
    if has_vres:
        vf_ref, vg_ref = refs[:2]
        refs = refs[2:]
    o_ref, act_ref = refs
    D = x_ref.shape[1]

    bi = lax.broadcasted_iota(jnp.int32, (LANES, LANES), 0) // RWKV_HEAD
    bj = lax.broadcasted_iota(jnp.int32, (LANES, LANES), 1) // RWKV_HEAD
    head_ones = (bi == bj).astype(BF16)
    inv_n = 1.0 / RWKV_HEAD

    for s in range(D // LANES):
        sl = slice(s * LANES, (s + 1) * LANES)
        y = y_ref[0, :, sl] + y_ref[1, :, sl]
        mean = _bdot(y, head_ones) * inv_n
        dy = y - mean
        var = _bdot(dy * dy, head_ones) * inv_n
        yn = dy * lax.rsqrt(var + GN_EPS) * gnw_ref[:, sl] + gnb_ref[:, sl]
        k_ = k_ref[:, sl]
        v_ = v_ref[:, sl]
        if has_vres:
            v_ = v_ + (vf_ref[:, sl] - v_) * vg_ref[:, sl]
        a_mean = 0.5 * (a_ref[0, :, sl] + a_ref[1, :, sl])
        k_mean = k_ * (1.0 + (a_mean - 1.0) * ka_ref[:, sl])
        bonus = _bdot(r_ref[:, sl] * k_mean * rk_ref[:, sl], head_ones) * v_
        act_ref[:, sl] = ((yn + bonus) * g_ref[:, sl]).astype(BF16)
    h = jnp.dot(act_ref[...], wo_ref[...], preferred_element_type=F32)
    o_ref[...] = x_ref[...] + _rms(h, gpost_ref[...])


def _rwkv_out(x, y, r, k, v, a, g, k_a, r_k, gn_w, gn_b, w_o, gpost, vres):
    T, D = x.shape
    tm = _tile(T, 128)
    has_vres = vres is not None
    tok = pl.BlockSpec((tm, D), lambda i: (i, 0))
    tok2 = pl.BlockSpec((2, tm, D), lambda i: (0, i, 0))
    par = pl.BlockSpec((1, D), lambda i: (0, 0))
    args = [x, y, r, k, v, a, g, k_a, r_k, gn_w, gn_b, w_o, gpost]
    in_specs = [tok, tok2, tok, tok, tok, tok2, tok, par, par, par, par,
                pl.BlockSpec((D, D), lambda i: (0, 0)), par]
    if has_vres:
        args += list(vres)
        in_specs += [tok, tok]
    return pl.pallas_call(
        functools.partial(_rwkv_out_kernel, has_vres),
        grid=(T // tm,),
        in_specs=in_specs,
        out_specs=tok,
        out_shape=jax.ShapeDtypeStruct((T, D), F32),
        scratch_shapes=[pltpu.VMEM((tm, D), BF16)],
        compiler_params=_params(("parallel",)),
    )(*args)


def _qkv_kernel(tm, tn, dil, x_ref, gpre_ref, w_ref, cos_ref, sa_ref, sb_ref, o_ref, xn_ref, acc_ref):
    j = pl.program_id(1)

    @pl.when(j == 0)
    def _():
        xn_ref[...] = _rms(x_ref[...], gpre_ref[...]).astype(BF16)

    cos = cos_ref[...]
    sa = sa_ref[...]
    sb = sb_ref[...]
    half = ROPE_DIM // 2
    xn = xn_ref[...]
    per_dot = MXU_COLS // ATT_HEAD
    for s in range(tn // ATT_HEAD):
        sl = slice(s * ATT_HEAD, (s + 1) * ATT_HEAD)
        if s % per_dot == 0:
            acc = jnp.dot(xn, w_ref[:, s * ATT_HEAD:s * ATT_HEAD + MXU_COLS],
                          preferred_element_type=F32)
        t = acc[:, (s % per_dot) * ATT_HEAD:(s % per_dot + 1) * ATT_HEAD]
        rot = t * cos + pltpu.roll(t, half, 1) * sa + pltpu.roll(t, ATT_HEAD - half, 1) * sb
        if dil == 1:
            o_ref[0, :, sl] = rot.astype(BF16)
        else:
            acc_ref[s] = rot
            for c in range(dil):
                o_ref[c, :, sl] = acc_ref[s, pl.ds(c, tm // dil, stride=dil), :].astype(BF16)


def _rope_tables(pos):
    half = ROPE_DIM // 2
    inv = ROPE_THETA ** (-jnp.arange(half, dtype=F32) / half)
    ang = pos.astype(F32)[:, None] * inv[None, :]
    cos, sin = jnp.cos(ang), jnp.sin(ang)
    T = pos.shape[0]
    ones = jnp.ones((T, ATT_HEAD - ROPE_DIM), F32)
    zeros = jnp.zeros((T, ATT_HEAD - ROPE_DIM), F32)
    zh = jnp.zeros((T, half), F32)
    cos_t = jnp.concatenate([cos, cos, ones], axis=1)
    sa_t = jnp.concatenate([zh, sin, zeros], axis=1)
    sb_t = jnp.concatenate([-sin, zh, zeros], axis=1)
    return (jnp.stack([cos_t, jnp.ones_like(cos_t)]), jnp.stack([sa_t, jnp.zeros_like(sa_t)]),
            jnp.stack([sb_t, jnp.zeros_like(sb_t)]))


def _qkv(x, gpre, w, tables, gi, dil):
    T, D = x.shape
    tm = _tile(T, 1024, 16 * dil)
    tn = _tile(D, 1024, MXU_COLS)
    nj = 3 * D // tn
    tab = pl.BlockSpec((None, tm, ATT_HEAD), lambda i, j: ((j * tn) // (2 * D), i, 0))
    return pl.pallas_call(
        functools.partial(_qkv_kernel, tm, tn, dil),
        grid=(T // tm, nj),
        in_specs=[
            pl.BlockSpec((tm, D), lambda i, j: (i, 0)),
            pl.BlockSpec((1, D), lambda i, j: (0, 0)),
            pl.BlockSpec((D, tn), lambda i, j: (0, gi * nj + j)),
            tab, tab, tab,
        ],
        out_specs=pl.BlockSpec((dil, tm // dil, tn), lambda i, j: (0, i, j)),
        out_shape=jax.ShapeDtypeStruct((dil, T // dil, 3 * D), BF16),
        scratch_shapes=[pltpu.VMEM((tm, D), BF16), pltpu.VMEM((tn // ATT_HEAD, tm, ATT_HEAD), F32)],
        compiler_params=_params(("parallel", "arbitrary")),
    )(x, gpre, w, *tables)


def _attn_kernel(bq, n_heads, bounds, q_ref, kp_ref, kc_ref, kn_ref, vp_ref, vc_ref, vn_ref,
                 o_ref, lse_ref):
    row0 = pl.program_id(0) * bq
    lo = jnp.int32(bounds[0])
    hi = jnp.int32(bounds[-1])
    for b in bounds[1:-1]:
        lo = jnp.where(row0 >= b, b, lo)
    for b in reversed(bounds[1:-1]):
        hi = jnp.where(row0 < b, b, hi)
    nk = bq + 2 * ATT_HALF
    qrow = row0 + lax.broadcasted_iota(jnp.int32, (bq, nk), 0)
    krow = row0 - ATT_HALF + lax.broadcasted_iota(jnp.int32, (bq, nk), 1)
    valid = (jnp.abs(krow - qrow) <= ATT_HALF) & (krow >= lo) & (krow < hi)
    scale = ATT_HEAD ** -0.5
    for h in range(n_heads):
        sl = slice(h * ATT_HEAD, (h + 1) * ATT_HEAD)
        q = q_ref[:, sl]
        kcat = jnp.concatenate([kp_ref[bq - ATT_HALF:, sl], kc_ref[:, sl], kn_ref[:ATT_HALF, sl]], axis=0)
        vcat = jnp.concatenate([vp_ref[bq - ATT_HALF:, sl], vc_ref[:, sl], vn_ref[:ATT_HALF, sl]], axis=0)
        s = lax.dot_general(q, kcat, _NT, preferred_element_type=F32) * scale
        s = jnp.where(valid, s, -1e30)
        m = jnp.max(s, axis=-1, keepdims=True)
        p = jnp.exp(s - m)
        l = jnp.sum(p, axis=-1, keepdims=True)
        o = jnp.dot(p.astype(BF16), vcat, preferred_element_type=F32) / l
        o_ref[:, sl] = o
        lse_ref[:, sl] = jnp.broadcast_to(m + jnp.log(l), (bq, ATT_HEAD))


def _attn_group(qkv, D, seq_starts, seq_ends):
    dil, rows, _ = qkv.shape
    bounds = tuple(s // dil for s in seq_starts) + (seq_ends[-1] // dil,)
    seg = [b - a for a, b in zip(bounds[:-1], bounds[1:])]
    bq = 128
    for s in seg:
        bq = math.gcd(bq, s)
    assert bq >= ATT_HALF, "sequence too short for the dilated band blocks"
    nb = rows // bq

    def spec(kind, shift):
        def imap(i, c):
            return (c, jnp.clip(i + shift, 0, nb - 1), kind)
        return pl.BlockSpec((None, bq, D), imap)

    out = pl.BlockSpec((None, bq, D), lambda i, c: (c, i, 0))
    return pl.pallas_call(
        functools.partial(_attn_kernel, bq, D // ATT_HEAD, bounds),
        grid=(nb, dil),
        in_specs=[spec(0, 0), spec(1, -1), spec(1, 0), spec(1, 1),
                  spec(2, -1), spec(2, 0), spec(2, 1)],
        out_specs=[out, out],
        out_shape=[jax.ShapeDtypeStruct((dil, rows, D), F32)] * 2,
        compiler_params=_params(("parallel", "arbitrary")),
    )(qkv, qkv, qkv, qkv, qkv, qkv, qkv)


def _attn_out_kernel(tm, x_ref, o0_ref, o1_ref, o2_ref, l0_ref, l1_ref, l2_ref, wo_ref, gpost_ref,
                     out_ref, act_ref, *scratch):
    D = x_ref.shape[1]

    def token_major(ref, sl, s):
        dil = ref.shape[0]
        if dil == 1:
            return ref[0, :, sl]
        for c in range(dil):
            s[pl.ds(c, tm // dil, stride=dil), :] = ref[c, :, sl]
        return s[...]

    for k in range(D // LANES):
        sl = slice(k * LANES, (k + 1) * LANES)
        free = list(scratch)
        l0, l1, l2 = [token_major(r, sl, free.pop() if r.shape[0] > 1 else None)
                      for r in (l0_ref, l1_ref, l2_ref)]
        o0, o1, o2 = [token_major(r, sl, free.pop() if r.shape[0] > 1 else None)
                      for r in (o0_ref, o1_ref, o2_ref)]
        m = jnp.maximum(jnp.maximum(l0, l1), l2)
        e0, e1, e2 = jnp.exp(l0 - m), jnp.exp(l1 - m), jnp.exp(l2 - m)
        act_ref[:, sl] = ((e0 * o0 + e1 * o1 + e2 * o2) / (e0 + e1 + e2)).astype(BF16)
    h = jnp.dot(act_ref[...], wo_ref[...], preferred_element_type=F32)
    out_ref[...] = x_ref[...] + _rms(h, gpost_ref[...])


def _attn_out(x, os_, lses, w_o, gpost):
    T, D = x.shape
    tm = _tile(T, 128, 8 * max(o.shape[0] for o in os_))
    tok = pl.BlockSpec((tm, D), lambda i: (i, 0))

    def planes(a):
        dil = a.shape[0]
        return pl.BlockSpec((dil, tm // dil, D), lambda i: (0, i, 0))

    n_scratch = sum(2 for o in os_ if o.shape[0] > 1)
    return pl.pallas_call(
        functools.partial(_attn_out_kernel, tm),
        grid=(T // tm,),
        in_specs=[tok] + [planes(a) for a in os_] + [planes(a) for a in lses]
        + [pl.BlockSpec((D, D), lambda i: (0, 0)), pl.BlockSpec((1, D), lambda i: (0, 0))],
        out_specs=tok,
        out_shape=jax.ShapeDtypeStruct((T, D), F32),
        scratch_shapes=[pltpu.VMEM((tm, D), BF16)] + [pltpu.VMEM((tm, LANES), F32)] * n_scratch,
        compiler_params=_params(("parallel",)),
    )(x, *os_, *lses, w_o, gpost)


def kernel(x_prompt, x_sample, norm_pre, norm_post, ffn_w_gate, ffn_w_up, ffn_w_down, rwkv_mu, rwkv_w_rkv, rwkv_w0, rwkv_w1, rwkv_w2, rwkv_a0, rwkv_a1, rwkv_a2, rwkv_v0, rwkv_v1, rwkv_v2, rwkv_g1, rwkv_g2, rwkv_k_k, rwkv_k_a, rwkv_r_k, rwkv_gn_w, rwkv_gn_b, rwkv_w_o, attn_w_qkv, attn_w_o):
    D = x_prompt.shape[-1]
    depth = norm_pre.shape[0]
    seqs = [x_prompt.shape[1]] * x_prompt.shape[0] + [x_sample.shape[1]] * x_sample.shape[0]
    seq_starts, seq_ends, t = [], [], 0
    for s in seqs:
        seq_starts.append(t)
        t += s
        seq_ends.append(t)
    seq_starts, seq_ends = tuple(seq_starts), tuple(seq_ends)
    x = jnp.concatenate([x_prompt.reshape(-1, D), x_sample.reshape(-1, D)], axis=0)
    pos = jnp.concatenate([jnp.arange(s, dtype=jnp.int32) for s in seqs])
    tables = _rope_tables(pos)

    def vec(a):
        return a.reshape(1, D)

    v_first = None
    for layer in range(depth):
        x = _ffn(x, vec(norm_pre[layer, 0]), ffn_w_gate[layer, 0].astype(BF16),
                 ffn_w_up[layer, 0].astype(BF16), ffn_w_down[layer, 0].astype(BF16),
                 vec(norm_post[layer, 0]))
        j = layer // 2
        if layer % 2 == 0:
            vres_w = None if j == 0 else (rwkv_v0[j - 1], rwkv_v1[j - 1], rwkv_v2[j - 1])
            pre = _rwkv_pre(x, seq_starts, seq_ends, vec(norm_pre[layer, 1]), rwkv_mu[j],
                            rwkv_w0[j], rwkv_w1[j], rwkv_w2[j], rwkv_a0[j], rwkv_a1[j], rwkv_a2[j],
                            rwkv_g1[j], rwkv_g2[j], vres_w)
            xr, xk, xv, ld, a, g = pre[:6]
            r = _mm(xr, rwkv_w_rkv[j, 0].astype(BF16))
            k = _mm(xk, rwkv_w_rkv[j, 1].astype(BF16))
            v = _mm(xv, rwkv_w_rkv[j, 2].astype(BF16))
            if j == 0:
                v_first, vres = v, None
            else:
                vres = (v_first, pre[6])
            y = _wkv(r, k, v, ld, a, vec(rwkv_k_k[j]), vec(rwkv_k_a[j]), vres, seq_starts, seq_ends)
            x = _rwkv_out(x, y, r, k, v, a, g, vec(rwkv_k_a[j]), vec(rwkv_r_k[j]),
                          vec(rwkv_gn_w[j]), vec(rwkv_gn_b[j]), rwkv_w_o[j].astype(BF16),
                          vec(norm_post[layer, 1]), vres)
        else:
            w_qkv = attn_w_qkv[j].astype(BF16)
            os_, lses = [], []
            for gi, (_, dil) in enumerate(DILATED_GROUPS):
                qkv = _qkv(x, vec(norm_pre[layer, 1]), w_qkv, tables, gi, dil)
                o, lse = _attn_group(qkv, D, seq_starts, seq_ends)
                os_.append(o)
                lses.append(lse)
            x = _attn_out(x, os_, lses, attn_w_o[j].astype(BF16), vec(norm_post[layer, 1]))
        x = _ffn(x, vec(norm_pre[layer, 2]), ffn_w_gate[layer, 1].astype(BF16),
                 ffn_w_up[layer, 1].astype(BF16), ffn_w_down[layer, 1].astype(BF16),
                 vec(norm_post[layer, 2]))
    n_p = x_prompt.shape[0] * x_prompt.shape[1]
    return (x[:n_p].reshape(x_prompt.shape), x[n_p:].reshape(x_sample.shape))
```

```python
import functools
import math

import jax
import jax.numpy as jnp
from jax import lax
from jax.experimental import pallas as pl
from jax.experimental.pallas import tpu as pltpu

F32 = jnp.float32
BF16 = jnp.bfloat16

NORM_EPS = 1e-6
GN_EPS = 64e-5
RWKV_HEAD = 64
ATT_HEAD = 128
ROPE_DIM = ATT_HEAD // 4
ROPE_THETA = 500000.0
DILATED_GROUPS = ((128, 1), (512, 4), (2048, 16))
N_GROUPS = len(DILATED_GROUPS)
ATT_HALF = 64
LANES = 128
MXU_COLS = 256
WKV_CHUNK = 64
WKV_PAIRS_PER_STEP = 16
VMEM_LIMIT = 56 * 1024 * 1024

_NT = (((1,), (1,)), ((), ()))
_TN = (((0,), (0,)), ((), ()))


def _tile(n, pref, mult=8):
    if n <= pref:
        return n
    t = (pref // mult) * mult
    while t >= mult:
        if n % t == 0:
            return t
        t -= mult
    return n


def _in_list(v, lst):
    r = v == lst[0]
    for s in lst[1:]:
        r = r | (v == s)
    return r


def _rms(x, g):
    return x * lax.rsqrt(jnp.mean(x * x, axis=-1, keepdims=True) + NORM_EPS) * g


def _bdot(a, b):
    return jnp.dot(a.astype(BF16), b.astype(BF16), preferred_element_type=F32)


def _ntdot(a, b):
    return lax.dot_general(a, b, _NT, preferred_element_type=F32)


def _params(sem):
    return pltpu.CompilerParams(dimension_semantics=sem, vmem_limit_bytes=VMEM_LIMIT)


def _ffn_kernel(x_ref, gpre_ref, wg_ref, wu_ref, wd_ref, gpost_ref, o_ref, xn_ref, acc_ref):
    j = pl.program_id(1)

    @pl.when(j == 0)
    def _():
        xn_ref[...] = _rms(x_ref[...], gpre_ref[...]).astype(BF16)
        acc_ref[...] = jnp.zeros_like(acc_ref)

    xn = xn_ref[...]
    g = jnp.dot(xn, wg_ref[...], preferred_element_type=F32)
    u = jnp.dot(xn, wu_ref[...], preferred_element_type=F32)
    h = (g * jax.nn.sigmoid(g)) * u
    acc_ref[...] += jnp.dot(h.astype(BF16), wd_ref[...], preferred_element_type=F32)

    @pl.when(j == pl.num_programs(1) - 1)
    def _():
        o_ref[...] = x_ref[...] + 0.5 * _rms(acc_ref[...], gpost_ref[...])


def _ffn(x, gpre, wg, wu, wd, gpost):
    T, D = x.shape
    F = wg.shape[1]
    tm = _tile(T, 512)
    tf = _tile(F, 512, LANES)
    return pl.pallas_call(
        _ffn_kernel,
        grid=(T // tm, F // tf),
        in_specs=[
            pl.BlockSpec((tm, D), lambda i, j: (i, 0)),
            pl.BlockSpec((1, D), lambda i, j: (0, 0)),
            pl.BlockSpec((D, tf), lambda i, j: (0, j)),
            pl.BlockSpec((D, tf), lambda i, j: (0, j)),
            pl.BlockSpec((tf, D), lambda i, j: (j, 0)),
            pl.BlockSpec((1, D), lambda i, j: (0, 0)),
        ],
        out_specs=pl.BlockSpec((tm, D), lambda i, j: (i, 0)),
        out_shape=jax.ShapeDtypeStruct((T, D), F32),
        scratch_shapes=[pltpu.VMEM((tm, D), BF16), pltpu.VMEM((tm, D), F32)],
        compiler_params=_params(("parallel", "arbitrary")),
    )(x, gpre, wg, wu, wd, gpost)


def _mm_kernel(a_ref, w_ref, o_ref):
    a = a_ref[...]
    for c in range(0, o_ref.shape[1], MXU_COLS):
        o_ref[:, c:c + MXU_COLS] = jnp.dot(
            a, w_ref[:, c:c + MXU_COLS], preferred_element_type=F32).astype(o_ref.dtype)


def _mm(a, w):
    M, K = a.shape
    N = w.shape[1]
    tm = _tile(M, 1024)
    tn = _tile(N, 1024, MXU_COLS)
    return pl.pallas_call(
        _mm_kernel,
        grid=(M // tm, N // tn),
        in_specs=[
            pl.BlockSpec((tm, K), lambda i, j: (i, 0)),
            pl.BlockSpec((K, tn), lambda i, j: (0, j)),
        ],
        out_specs=pl.BlockSpec((tm, tn), lambda i, j: (i, j)),
        out_shape=jax.ShapeDtypeStruct((M, N), BF16),
        compiler_params=_params(("parallel", "arbitrary")),
    )(a, w)


def _softplus(z):
    return jnp.maximum(z, 0.0) + jnp.log(1.0 + jnp.exp(-jnp.abs(z)))


def _rwkv_pre_kernel(seq_starts, seq_ends, has_vres, tm, *refs):
    (x_ref, xp_ref, xn_ref, gpre_ref, mu_ref, w0_ref, w1_ref, w2_ref,
     a0_ref, a1_ref, a2_ref, g1_ref, g2_ref) = refs[:13]
    refs = refs[13:]
    if has_vres:
        v0_ref, v1_ref, v2_ref = refs[:3]
        refs = refs[3:]
    xr_ref, xk_ref, xv_ref, ld_ref, a_ref, g_ref = refs[:6]

    t0 = pl.program_id(0) * tm
    gpre = gpre_ref[...]
    h = _rms(x_ref[...], gpre)
    keep_p = 1.0 - _in_list(t0, seq_starts).astype(F32)
    keep_n = 1.0 - _in_list(t0 + tm, seq_ends).astype(F32)
    hp = _rms(xp_ref[7:8, :], gpre) * keep_p
    hn = _rms(xn_ref[0:1, :], gpre) * keep_n
    row = lax.broadcasted_iota(jnp.int32, h.shape, 0)
    prev = jnp.where(row == 0, hp, pltpu.roll(h, 1, 0))
    nxt = jnp.where(row == tm - 1, hn, pltpu.roll(h, tm - 1, 0))
    xx = 0.5 * (prev + nxt) - h

    def mix(i):
        return (h + xx * mu_ref[i:i + 1, :]).astype(BF16)

    xr_ref[...] = mix(0)
    xw = mix(1)
    xk_ref[...] = mix(2)
    xv = mix(3)
    xv_ref[...] = xv
    xa = mix(4)
    xg = mix(5)
    for d in range(2):
        tw = jnp.tanh(jnp.dot(xw, w1_ref[d], preferred_element_type=F32))
        z = w0_ref[d:d + 1, :] + _bdot(tw, w2_ref[d])
        ld_ref[d] = (-jnp.exp(-_softplus(-z) - 0.5)).astype(BF16)
        ta = jnp.dot(xa, a1_ref[d], preferred_element_type=F32)
        a_ref[d] = jax.nn.sigmoid(a0_ref[d:d + 1, :] + _bdot(ta, a2_ref[d])).astype(BF16)
    tg = jax.nn.sigmoid(jnp.dot(xg, g1_ref[...], preferred_element_type=F32))
    g_ref[...] = _bdot(tg, g2_ref[...]).astype(BF16)
    if has_vres:
        vg_ref = refs[6]
        tv = jnp.dot(xv, v1_ref[...], preferred_element_type=F32)
        vg_ref[...] = jax.nn.sigmoid(v0_ref[...] + _bdot(tv, v2_ref[...])).astype(BF16)


def _pad_lora(w_in, w_out):
    r = w_in.shape[-1]
    rp = -(-r // LANES) * LANES
    pin = [(0, 0)] * (w_in.ndim - 1) + [(0, rp - r)]
    pout = [(0, 0)] * (w_out.ndim - 2) + [(0, rp - r), (0, 0)]
    return jnp.pad(w_in, pin).astype(BF16), jnp.pad(w_out, pout).astype(BF16)


def _rwkv_pre(x, seq_starts, seq_ends, gpre, mu, w0, w1, w2, a0, a1, a2, g1, g2, vres):
    T, D = x.shape
    tm = _tile(T, 128)
    nb8 = T // 8
    has_vres = vres is not None
    w1p, w2p = _pad_lora(w1, w2)
    a1p, a2p = _pad_lora(a1, a2)
    g1p, g2p = _pad_lora(g1, g2)

    def full(a):
        nd = a.ndim
        return pl.BlockSpec(a.shape, lambda i, _nd=nd: (0,) * _nd)

    args = [x, x, x, gpre, mu, w0, w1p, w2p, a0, a1p, a2p, g1p, g2p]
    in_specs = [
        pl.BlockSpec((tm, D), lambda i: (i, 0)),
        pl.BlockSpec((8, D), lambda i: (jnp.maximum(i * (tm // 8) - 1, 0), 0)),
        pl.BlockSpec((8, D), lambda i: (jnp.minimum((i + 1) * (tm // 8), nb8 - 1), 0)),
    ] + [full(a) for a in args[3:]]
    if has_vres:
        v0, v1, v2 = vres
        v1p, v2p = _pad_lora(v1, v2)
        extra = [v0.reshape(1, D), v1p, v2p]
        args += extra
        in_specs += [full(a) for a in extra]
    tok = pl.BlockSpec((tm, D), lambda i: (i, 0))
    tok2 = pl.BlockSpec((2, tm, D), lambda i: (0, i, 0))
    out_specs = [tok, tok, tok, tok2, tok2, tok]
    out_shape = [jax.ShapeDtypeStruct((T, D), BF16)] * 3 + [
        jax.ShapeDtypeStruct((2, T, D), BF16), jax.ShapeDtypeStruct((2, T, D), BF16),
        jax.ShapeDtypeStruct((T, D), BF16)]
    if has_vres:
        out_specs.append(tok)
        out_shape.append(jax.ShapeDtypeStruct((T, D), BF16))
    return pl.pallas_call(
        functools.partial(_rwkv_pre_kernel, seq_starts, seq_ends, has_vres, tm),
        grid=(T // tm,),
        in_specs=in_specs,
        out_specs=out_specs,
        out_shape=out_shape,
        compiler_params=_params(("parallel",)),
    )(*args)


def _wkv_kernel(L, G, nc, resets_f, resets_r, has_vres, *refs):
    r_ref, k_ref, v_ref, ld_ref, a_ref, kk_ref, ka_ref = refs[:7]
    refs = refs[7:]
    if has_vres:
        vf_ref, vg_ref = refs[:2]
        refs = refs[2:]
    y_ref, s_ref = refs

    d = pl.program_id(0)
    c = pl.program_id(2)
    ci = c + d * (nc - 1 - 2 * c)
    reset = ((d == 0) & _in_list(ci, resets_f)) | ((d == 1) & _in_list(ci, resets_r))

    @pl.when(reset)
    def _():
        s_ref[...] = jnp.zeros_like(s_ref)

    sgn = 1 - 2 * d
    tt = lax.broadcasted_iota(jnp.int32, (L, LANES), 0)
    ss = lax.broadcasted_iota(jnp.int32, (L, LANES), 1) % L
    delta = sgn * (tt - ss)
    incl = delta >= 0
    strict = delta > 0
    eye = (tt == ss).astype(F32)
    tri = (sgn * (lax.broadcasted_iota(jnp.int32, (L, L), 0)
                  - lax.broadcasted_iota(jnp.int32, (L, L), 1)) >= 0).astype(BF16)
    first_head = lax.broadcasted_iota(jnp.int32, (L, LANES), 1) < RWKV_HEAD
    bi = lax.broadcasted_iota(jnp.int32, (LANES, LANES), 0) // RWKV_HEAD
    bj = lax.broadcasted_iota(jnp.int32, (LANES, LANES), 1) // RWKV_HEAD
    same_head = bi == bj
    head_ones = same_head.astype(BF16)
    n_double = int(math.log2(L)) - 1

    def bd(x):
        xb = x.astype(BF16)
        zero = jnp.zeros_like(xb)
        return jnp.concatenate([jnp.where(first_head, xb, zero), jnp.where(first_head, zero, xb)], axis=0)

    pairs = []
    for p in range(G):
        sl = slice(p * LANES, (p + 1) * LANES)
        r_ = r_ref[:, sl].astype(F32)
        k_ = k_ref[:, sl].astype(F32)
        v_ = v_ref[:, sl].astype(F32)
        ld_b = ld_ref[:, sl]
        ld_ = ld_b.astype(F32)
        a_ = a_ref[:, sl].astype(F32)
        if has_vres:
            v_ = v_ + (vf_ref[:, sl].astype(F32) - v_) * vg_ref[:, sl].astype(F32)
        kkc = k_ * kk_ref[:, sl]
        n2 = jnp.dot((kkc * kkc).astype(BF16), head_ones, preferred_element_type=F32)
        kkn = kkc / jnp.maximum(jnp.sqrt(n2), 1e-12)
        kd = k_ * (1.0 + (a_ - 1.0) * ka_ref[:, sl])
        cs = jnp.dot(tri, ld_b, preferred_element_type=F32)
        w_inv = jnp.exp(-cs)
        bt = kkn * a_ * w_inv
        kt = kd * w_inv
        pairs.append(dict(
            sl=sl, v=v_, bt=bt, kt=kt, st=s_ref[p],
            lhs=jnp.concatenate([-kkn * jnp.exp(cs - ld_), r_ * jnp.exp(cs)], axis=0).astype(BF16),
            wend=jnp.exp(jnp.sum(ld_, axis=0, keepdims=True))))

    for pr in pairs:
        rhs_t = jnp.concatenate([bd(pr["bt"]), bd(pr["kt"]), pr["st"].astype(BF16)], axis=0)
        prod = _ntdot(pr["lhs"], rhs_t)
        pr["a_ab"] = jnp.where(strict, prod[:L, :LANES], 0.0)
        pr["a_rb"] = jnp.where(incl, prod[L:, :LANES], 0.0)
        a_ak = jnp.where(strict, prod[:L, LANES:2 * LANES], 0.0)
        a_rk = jnp.where(incl, prod[L:, LANES:2 * LANES], 0.0)
        pr["sp"] = prod[:, 2 * LANES:]
        pr["a_k"] = jnp.concatenate([a_ak, a_rk], axis=0)
        pr["tinv"] = eye + pr["a_ab"]
        pr["ak"] = pr["a_ab"]
    for pr in pairs:
        pr["spv"] = pr["sp"] + _bdot(pr["a_k"], bd(pr["v"]))
    for _ in range(n_double):
        for pr in pairs:
            pr["ak"] = _bdot(pr["ak"], bd(pr["ak"]))
        for pr in pairs:
            pr["tinv"] = pr["tinv"] + _bdot(pr["ak"], bd(pr["tinv"]))
    for pr in pairs:
        pr["u"] = _bdot(pr["tinv"], bd(pr["spv"][:L]))
    for pr in pairs:
        pr["y"] = pr["spv"][L:] + _bdot(pr["a_rb"], bd(pr["u"]))
    for p, pr in enumerate(pairs):
        y_ref[:, pr["sl"]] = pr["y"].astype(BF16)
        uv = jnp.concatenate([pr["u"], pr["v"]], axis=0).astype(BF16)
        bk = jnp.concatenate([pr["bt"], pr["kt"]], axis=0).astype(BF16)
        upd = lax.dot_general(uv, bk, _TN, preferred_element_type=F32)
        s_ref[p] = (pr["st"] + jnp.where(same_head, upd, 0.0)) * pr["wend"]


def _wkv(r, k, v, ld, a, k_k, k_a, vres, seq_starts, seq_ends):
    T, D = r.shape
    L = WKV_CHUNK
    npairs = D // LANES
    assert 2 * L == LANES and RWKV_HEAD == L
    G = math.gcd(npairs, WKV_PAIRS_PER_STEP)
    nc = T // L
    W = G * LANES
    resets_f = tuple(s // L for s in seq_starts)
    resets_r = tuple(e // L - 1 for e in seq_ends)
    has_vres = vres is not None

    def cidx(d, c):
        return c + d * (nc - 1 - 2 * c)

    tok = pl.BlockSpec((L, W), lambda d, p, c: (cidx(d, c), p))
    tokd = pl.BlockSpec((None, L, W), lambda d, p, c: (d, cidx(d, c), p))
    par = pl.BlockSpec((1, W), lambda d, p, c: (0, p))
    args = [r, k, v, ld, a, k_k, k_a]
    in_specs = [tok, tok, tok, tokd, tokd, par, par]
    if has_vres:
        args += list(vres)
        in_specs += [tok, tok]
    return pl.pallas_call(
        functools.partial(_wkv_kernel, L, G, nc, resets_f, resets_r, has_vres),
        grid=(2, npairs // G, nc),
        in_specs=in_specs,
        out_specs=tokd,
        out_shape=jax.ShapeDtypeStruct((2, T, D), BF16),
        scratch_shapes=[pltpu.VMEM((G, LANES, LANES), F32)],
        compiler_params=_params(("arbitrary", "arbitrary", "arbitrary")),
    )(*args)


def _rwkv_out_kernel(has_vres, *refs):
    (x_ref, y_ref, r_ref, k_ref, v_ref, a_ref, g_ref, ka_ref, rk_ref, gnw_ref, gnb_ref,
     wo_ref, gpost_ref) = refs[:13]
    refs = refs[13:]
    if has_vres:
        vf_ref, vg_ref = refs[:2]
        refs = refs[2:]
    o_ref, act_ref = refs
    D = x_ref.shape[1]

    bi = lax.broadcasted_iota(jnp.int32, (LANES, LANES), 0) // RWKV_HEAD
    bj = lax.broadcasted_iota(jnp.int32, (LANES, LANES), 1) // RWKV_HEAD
    head_ones = (bi == bj).astype(BF16)
    inv_n = 1.0 / RWKV_HEAD

    for s in range(D // LANES):
        sl = slice(s * LANES, (s + 1) * LANES)
        y = y_ref[0, :, sl].astype(F32) + y_ref[1, :, sl].astype(F32)
        mean = _bdot(y, head_ones) * inv_n
        dy = y - mean
        var = _bdot(dy * dy, head_ones) * inv_n
        yn = dy * lax.rsqrt(var + GN_EPS) * gnw_ref[:, sl] + gnb_ref[:, sl]
        k_ = k_ref[:, sl].astype(F32)
        v_ = v_ref[:, sl].astype(F32)
        if has_vres:
            v_ = v_ + (vf_ref[:, sl].astype(F32) - v_) * vg_ref[:, sl].astype(F32)
        a_mean = 0.5 * (a_ref[0, :, sl].astype(F32) + a_ref[1, :, sl].astype(F32))
        k_mean = k_ * (1.0 + (a_mean - 1.0) * ka_ref[:, sl])
        bonus = _bdot(r_ref[:, sl].astype(F32) * k_mean * rk_ref[:, sl], head_ones) * v_
        act_ref[:, sl] = ((yn + bonus) * g_ref[:, sl].astype(F32)).astype(BF16)
    h = jnp.dot(act_ref[...], wo_ref[...], preferred_element_type=F32)
    o_ref[...] = x_ref[...] + _rms(h, gpost_ref[...])


def _rwkv_out(x, y, r, k, v, a, g, k_a, r_k, gn_w, gn_b, w_o, gpost, vres):
    T, D = x.shape
    tm = _tile(T, 256, 16)
    has_vres = vres is not None
    tok = pl.BlockSpec((tm, D), lambda i: (i, 0))
    tok2 = pl.BlockSpec((2, tm, D), lambda i: (0, i, 0))
    par = pl.BlockSpec((1, D), lambda i: (0, 0))
    args = [x, y, r, k, v, a, g, k_a, r_k, gn_w, gn_b, w_o, gpost]
    in_specs = [tok, tok2, tok, tok, tok, tok2, tok, par, par, par, par,
                pl.BlockSpec((D, D), lambda i: (0, 0)), par]
    if has_vres:
        args += list(vres)
        in_specs += [tok, tok]
    return pl.pallas_call(
        functools.partial(_rwkv_out_kernel, has_vres),
        grid=(T // tm,),
        in_specs=in_specs,
        out_specs=tok,
        out_shape=jax.ShapeDtypeStruct((T, D), F32),
        scratch_shapes=[pltpu.VMEM((tm, D), BF16)],
        compiler_params=_params(("parallel",)),
    )(*args)


def _qkv_kernel(tm, tn, dil, x_ref, gpre_ref, w_ref, cos_ref, sa_ref, sb_ref, o_ref, xn_ref, acc_ref):
    j = pl.program_id(1)

    @pl.when(j == 0)
    def _():
        xn_ref[...] = _rms(x_ref[...], gpre_ref[...]).astype(BF16)

    cos = cos_ref[...]
    sa = sa_ref[...]
    sb = sb_ref[...]
    half = ROPE_DIM // 2
    xn = xn_ref[...]
    per_dot = MXU_COLS // ATT_HEAD
    for s in range(tn // ATT_HEAD):
        sl = slice(s * ATT_HEAD, (s + 1) * ATT_HEAD)
        if s % per_dot == 0:
            acc = jnp.dot(xn, w_ref[:, s * ATT_HEAD:s * ATT_HEAD + MXU_COLS],
                          preferred_element_type=F32)
        t = acc[:, (s % per_dot) * ATT_HEAD:(s % per_dot + 1) * ATT_HEAD]
        rot = t * cos + pltpu.roll(t, half, 1) * sa + pltpu.roll(t, ATT_HEAD - half, 1) * sb
        if dil == 1:
            o_ref[0, :, sl] = rot.astype(BF16)
        else:
            acc_ref[s] = rot
            for c in range(dil):
                o_ref[c, :, sl] = acc_ref[s, pl.ds(c, tm // dil, stride=dil), :].astype(BF16)


def _rope_tables(pos):
    half = ROPE_DIM // 2
    inv = ROPE_THETA ** (-jnp.arange(half, dtype=F32) / half)
    ang = pos.astype(F32)[:, None] * inv[None, :]
    cos, sin = jnp.cos(ang), jnp.sin(ang)
    T = pos.shape[0]
    ones = jnp.ones((T, ATT_HEAD - ROPE_DIM), F32)
    zeros = jnp.zeros((T, ATT_HEAD - ROPE_DIM), F32)
    zh = jnp.zeros((T, half), F32)
    cos_t = jnp.concatenate([cos, cos, ones], axis=1)
    sa_t = jnp.concatenate([zh, sin, zeros], axis=1)
    sb_t = jnp.concatenate([-sin, zh, zeros], axis=1)
    return (jnp.stack([cos_t, jnp.ones_like(cos_t)]), jnp.stack([sa_t, jnp.zeros_like(sa_t)]),
            jnp.stack([sb_t, jnp.zeros_like(sb_t)]))


def _qkv(x, gpre, w, tables, gi, dil):
    T, D = x.shape
    tm = _tile(T, 1024, 16 * dil)
    tn = _tile(D, 1024, MXU_COLS)
    nj = 3 * D // tn
    tab = pl.BlockSpec((None, tm, ATT_HEAD), lambda i, j: ((j * tn) // (2 * D), i, 0))
    return pl.pallas_call(
        functools.partial(_qkv_kernel, tm, tn, dil),
        grid=(T // tm, nj),
        in_specs=[
            pl.BlockSpec((tm, D), lambda i, j: (i, 0)),
            pl.BlockSpec((1, D), lambda i, j: (0, 0)),
            pl.BlockSpec((D, tn), lambda i, j: (0, gi * nj + j)),
            tab, tab, tab,
        ],
        out_specs=pl.BlockSpec((dil, tm // dil, tn), lambda i, j: (0, i, j)),
        out_shape=jax.ShapeDtypeStruct((dil, T // dil, 3 * D), BF16),
        scratch_shapes=[pltpu.VMEM((tm, D), BF16), pltpu.VMEM((tn // ATT_HEAD, tm, ATT_HEAD), F32)],
        compiler_params=_params(("parallel", "arbitrary")),
    )(x, gpre, w, *tables)


def _attn_kernel(bq, n_heads, bounds, q_ref, kp_ref, kc_ref, kn_ref, vp_ref, vc_ref, vn_ref,
                 o_ref, lse_ref):
    row0 = pl.program_id(0) * bq
    lo = jnp.int32(bounds[0])
    hi = jnp.int32(bounds[-1])
    for b in bounds[1:-1]:
        lo = jnp.where(row0 >= b, b, lo)
    for b in reversed(bounds[1:-1]):
        hi = jnp.where(row0 < b, b, hi)
    nk = bq + 2 * ATT_HALF
    qrow = row0 + lax.broadcasted_iota(jnp.int32, (bq, nk), 0)
    krow = row0 - ATT_HALF + lax.broadcasted_iota(jnp.int32, (bq, nk), 1)
    valid = (jnp.abs(krow - qrow) <= ATT_HALF) & (krow >= lo) & (krow < hi)
    scale = ATT_HEAD ** -0.5
    lane = lax.broadcasted_iota(jnp.int32, (bq, LANES), 1)
    lse_all = jnp.zeros((bq, LANES), F32)
    for h in range(n_heads):
        sl = slice(h * ATT_HEAD, (h + 1) * ATT_HEAD)
        q = q_ref[:, sl]
        kcat = jnp.concatenate([kp_ref[:, sl], kc_ref[:, sl], kn_ref[:, sl]], axis=0)
        vcat = jnp.concatenate([vp_ref[:, sl], vc_ref[:, sl], vn_ref[:, sl]], axis=0)
        s = lax.dot_general(q, kcat, _NT, preferred_element_type=F32) * scale
        s = jnp.where(valid, s, -1e30)
        m = jnp.max(s, axis=-1, keepdims=True)
        p = jnp.exp(s - m)
        l = jnp.sum(p, axis=-1, keepdims=True)
        o = jnp.dot(p.astype(BF16), vcat, preferred_element_type=F32) / l
        o_ref[:, sl] = o.astype(BF16)
        lse_all = jnp.where(lane == h, m + jnp.log(l), lse_all)
    lse_ref[...] = lse_all


def _attn_group(qkv, D, seq_starts, seq_ends):
    dil, rows, _ = qkv.shape
    bounds = tuple(s // dil for s in seq_starts) + (seq_ends[-1] // dil,)
    seg = [b - a for a, b in zip(bounds[:-1], bounds[1:])]
    bq = 128
    for s in seg:
        bq = math.gcd(bq, s)
    assert bq >= ATT_HALF, "sequence too short for the dilated band blocks"
    nb = rows // bq

    per = bq // ATT_HALF
    nh = rows // ATT_HALF

    def cur(kind):
        return pl.BlockSpec((None, bq, D), lambda i, c: (c, i, kind))

    def halo(kind, after):
        def imap(i, c):
            blk = (i + 1) * per if after else i * per - 1
            return (c, jnp.clip(blk, 0, nh - 1), kind)
        return pl.BlockSpec((None, ATT_HALF, D), imap)

    return pl.pallas_call(
        functools.partial(_attn_kernel, bq, D // ATT_HEAD, bounds),
        grid=(nb, dil),
        in_specs=[cur(0), halo(1, False), cur(1), halo(1, True),
                  halo(2, False), cur(2), halo(2, True)],
        out_specs=[pl.BlockSpec((None, bq, D), lambda i, c: (c, i, 0)),
                   pl.BlockSpec((None, bq, LANES), lambda i, c: (c, i, 0))],
        out_shape=[jax.ShapeDtypeStruct((dil, rows, D), BF16),
                   jax.ShapeDtypeStruct((dil, rows, LANES), F32)],
        compiler_params=_params(("parallel", "arbitrary")),
    )(qkv, qkv, qkv, qkv, qkv, qkv, qkv)


def _attn_out_kernel(tm, x_ref, o0_ref, o1_ref, o2_ref, l0_ref, l1_ref, l2_ref, wo_ref, gpost_ref,
                     out_ref, act_ref, *scratch):
    D = x_ref.shape[1]

    def token_major(ref, sl, s):
        dil = ref.shape[0]
        if dil == 1:
            return ref[0, :, sl].astype(F32)
        for c in range(dil):
            s[pl.ds(c, tm // dil, stride=dil), :] = ref[c, :, sl].astype(F32)
        return s[...]

    free = list(scratch)
    l0, l1, l2 = [token_major(r, slice(0, LANES), free.pop() if r.shape[0] > 1 else None)
                  for r in (l0_ref, l1_ref, l2_ref)]
    m = jnp.maximum(jnp.maximum(l0, l1), l2)
    e0, e1, e2 = jnp.exp(l0 - m), jnp.exp(l1 - m), jnp.exp(l2 - m)
    inv = 1.0 / (e0 + e1 + e2)
    w0, w1, w2 = e0 * inv, e1 * inv, e2 * inv
    for k in range(D // ATT_HEAD):
        sl = slice(k * ATT_HEAD, (k + 1) * ATT_HEAD)
        pool = list(free)
        o0, o1, o2 = [token_major(r, sl, pool.pop() if r.shape[0] > 1 else None)
                      for r in (o0_ref, o1_ref, o2_ref)]
        act_ref[:, sl] = (w0[:, k:k + 1] * o0 + w1[:, k:k + 1] * o1
                          + w2[:, k:k + 1] * o2).astype(BF16)
    h = jnp.dot(act_ref[...], wo_ref[...], preferred_element_type=F32)
    out_ref[...] = x_ref[...] + _rms(h, gpost_ref[...])


def _attn_out(x, os_, lses, w_o, gpost):
    T, D = x.shape
    tm = _tile(T, 256, 16 * max(o.shape[0] for o in os_))
    tok = pl.BlockSpec((tm, D), lambda i: (i, 0))

    def planes(a):
        dil = a.shape[0]
        return pl.BlockSpec((dil, tm // dil, a.shape[2]), lambda i: (0, i, 0))

    n_scratch = sum(2 for o in os_ if o.shape[0] > 1)
    return pl.pallas_call(
        functools.partial(_attn_out_kernel, tm),
        grid=(T // tm,),
        in_specs=[tok] + [planes(a) for a in os_] + [planes(a) for a in lses]
        + [pl.BlockSpec((D, D), lambda i: (0, 0)), pl.BlockSpec((1, D), lambda i: (0, 0))],
        out_specs=tok,
        out_shape=jax.ShapeDtypeStruct((T, D), F32),
        scratch_shapes=[pltpu.VMEM((tm, D), BF16)] + [pltpu.VMEM((tm, LANES), F32)] * n_scratch,
        compiler_params=_params(("parallel",)),
    )(x, *os_, *lses, w_o, gpost)


def kernel(x_prompt, x_sample, norm_pre, norm_post, ffn_w_gate, ffn_w_up, ffn_w_down, rwkv_mu, rwkv_w_rkv, rwkv_w0, rwkv_w1, rwkv_w2, rwkv_a0, rwkv_a1, rwkv_a2, rwkv_v0, rwkv_v1, rwkv_v2, rwkv_g1, rwkv_g2, rwkv_k_k, rwkv_k_a, rwkv_r_k, rwkv_gn_w, rwkv_gn_b, rwkv_w_o, attn_w_qkv, attn_w_o):
    D = x_prompt.shape[-1]
    depth = norm_pre.shape[0]
    seqs = [x_prompt.shape[1]] * x_prompt.shape[0] + [x_sample.shape[1]] * x_sample.shape[0]
    seq_starts, seq_ends, t = [], [], 0
    for s in seqs:
        seq_starts.append(t)
        t += s
        seq_ends.append(t)
    seq_starts, seq_ends = tuple(seq_starts), tuple(seq_ends)
    x = jnp.concatenate([x_prompt.reshape(-1, D), x_sample.reshape(-1, D)], axis=0)
    pos = jnp.concatenate([jnp.arange(s, dtype=jnp.int32) for s in seqs])
    tables = _rope_tables(pos)

    def vec(a):
        return a.reshape(1, D)

    v_first = None
    for layer in range(depth):
        x = _ffn(x, vec(norm_pre[layer, 0]), ffn_w_gate[layer, 0].astype(BF16),
                 ffn_w_up[layer, 0].astype(BF16), ffn_w_down[layer, 0].astype(BF16),
                 vec(norm_post[layer, 0]))
        j = layer // 2
        if layer % 2 == 0:
            vres_w = None if j == 0 else (rwkv_v0[j - 1], rwkv_v1[j - 1], rwkv_v2[j - 1])
            pre = _rwkv_pre(x, seq_starts, seq_ends, vec(norm_pre[layer, 1]), rwkv_mu[j],
                            rwkv_w0[j], rwkv_w1[j], rwkv_w2[j], rwkv_a0[j], rwkv_a1[j], rwkv_a2[j],
                            rwkv_g1[j], rwkv_g2[j], vres_w)
            xr, xk, xv, ld, a, g = pre[:6]
            r = _mm(xr, rwkv_w_rkv[j, 0].astype(BF16))
            k = _mm(xk, rwkv_w_rkv[j, 1].astype(BF16))
            v = _mm(xv, rwkv_w_rkv[j, 2].astype(BF16))
            if j == 0:
                v_first, vres = v, None
            else:
                vres = (v_first, pre[6])
            y = _wkv(r, k, v, ld, a, vec(rwkv_k_k[j]), vec(rwkv_k_a[j]), vres, seq_starts, seq_ends)
            x = _rwkv_out(x, y, r, k, v, a, g, vec(rwkv_k_a[j]), vec(rwkv_r_k[j]),
                          vec(rwkv_gn_w[j]), vec(rwkv_gn_b[j]), rwkv_w_o[j].astype(BF16),
                          vec(norm_post[layer, 1]), vres)
        else:
            w_qkv = attn_w_qkv[j].astype(BF16)
            os_, lses = [], []
            for gi, (_, dil) in enumerate(DILATED_GROUPS):
                qkv = _qkv(x, vec(norm_pre[layer, 1]), w_qkv, tables, gi, dil)
                o, lse = _attn_group(qkv, D, seq_starts, seq_ends)
                os_.append(o)
                lses.append(lse)
            x = _attn_out(x, os_, lses, attn_w_o[j].astype(BF16), vec(norm_post[layer, 1]))
        x = _ffn(x, vec(norm_pre[layer, 2]), ffn_w_gate[layer, 1].astype(BF16),
                 ffn_w_up[layer, 1].astype(BF16), ffn_w_down[layer, 1].astype(BF16),
                 vec(norm_post[layer, 2]))
    n_p = x_prompt.shape[0] * x_prompt.shape[1]
    return (x[:n_p].reshape(x_prompt.shape), x[n_p:].reshape(x_sample.shape))
```

```python
import functools
import math

import jax
import jax.numpy as jnp
from jax import lax
from jax.experimental import pallas as pl
from jax.experimental.pallas import tpu as pltpu

F32 = jnp.float32
BF16 = jnp.bfloat16

NORM_EPS = 1e-6
GN_EPS = 64e-5
RWKV_HEAD = 64
ATT_HEAD = 128
ROPE_DIM = ATT_HEAD // 4
ROPE_THETA = 500000.0
DILATED_GROUPS = ((128, 1), (512, 4), (2048, 16))
N_GROUPS = len(DILATED_GROUPS)
ATT_HALF = 64
LANES = 128
MXU_COLS = 256
WKV_CHUNK = 64
WKV_PAIRS_PER_STEP = 16
VMEM_LIMIT = 56 * 1024 * 1024

_NT = (((1,), (1,)), ((), ()))
_TN = (((0,), (0,)), ((), ()))


def _tile(n, pref, mult=8):
    if n <= pref:
        return n
    t = (pref // mult) * mult
    while t >= mult:
        if n % t == 0:
            return t
        t -= mult
    return n


def _in_list(v, lst):
    r = v == lst[0]
    for s in lst[1:]:
        r = r | (v == s)
    return r


def _rms(x, g):
    return x * lax.rsqrt(jnp.mean(x * x, axis=-1, keepdims=True) + NORM_EPS) * g


def _bdot(a, b):
    return jnp.dot(a.astype(BF16), b.astype(BF16), preferred_element_type=F32)


def _ntdot(a, b):
    return lax.dot_general(a, b, _NT, preferred_element_type=F32)


def _params(sem):
    return pltpu.CompilerParams(dimension_semantics=sem, vmem_limit_bytes=VMEM_LIMIT)


def _ffn_kernel(x_ref, gpre_ref, wg_ref, wu_ref, wd_ref, gpost_ref, o_ref, xn_ref, acc_ref):
    j = pl.program_id(1)

    @pl.when(j == 0)
    def _():
        xn_ref[...] = _rms(x_ref[...], gpre_ref[...]).astype(BF16)
        acc_ref[...] = jnp.zeros_like(acc_ref)

    xn = xn_ref[...]
    g = jnp.dot(xn, wg_ref[...], preferred_element_type=F32)
    u = jnp.dot(xn, wu_ref[...], preferred_element_type=F32)
    h = (g * jax.nn.sigmoid(g)) * u
    acc_ref[...] += jnp.dot(h.astype(BF16), wd_ref[...], preferred_element_type=F32)

    @pl.when(j == pl.num_programs(1) - 1)
    def _():
        o_ref[...] = x_ref[...] + 0.5 * _rms(acc_ref[...], gpost_ref[...])


def _ffn(x, gpre, wg, wu, wd, gpost):
    T, D = x.shape
    F = wg.shape[1]
    tm = _tile(T, 768, MXU_COLS)
    tf = _tile(F, 512, LANES)
    return pl.pallas_call(
        _ffn_kernel,
        grid=(T // tm, F // tf),
        in_specs=[
            pl.BlockSpec((tm, D), lambda i, j: (i, 0)),
            pl.BlockSpec((1, D), lambda i, j: (0, 0)),
            pl.BlockSpec((D, tf), lambda i, j: (0, j)),
            pl.BlockSpec((D, tf), lambda i, j: (0, j)),
            pl.BlockSpec((tf, D), lambda i, j: (j, 0)),
            pl.BlockSpec((1, D), lambda i, j: (0, 0)),
        ],
        out_specs=pl.BlockSpec((tm, D), lambda i, j: (i, 0)),
        out_shape=jax.ShapeDtypeStruct((T, D), F32),
        scratch_shapes=[pltpu.VMEM((tm, D), BF16), pltpu.VMEM((tm, D), F32)],
        compiler_params=_params(("parallel", "arbitrary")),
    )(x, gpre, wg, wu, wd, gpost)


def _mm_kernel(a_ref, w_ref, o_ref):
    a = a_ref[...]
    for c in range(0, o_ref.shape[1], MXU_COLS):
        o_ref[:, c:c + MXU_COLS] = jnp.dot(
            a, w_ref[:, c:c + MXU_COLS], preferred_element_type=F32).astype(o_ref.dtype)


def _mm(a, w):
    M, K = a.shape
    N = w.shape[1]
    tm = _tile(M, 1024)
    tn = _tile(N, 1024, MXU_COLS)
    return pl.pallas_call(
        _mm_kernel,
        grid=(M // tm, N // tn),
        in_specs=[
            pl.BlockSpec((tm, K), lambda i, j: (i, 0)),
            pl.BlockSpec((K, tn), lambda i, j: (0, j)),
        ],
        out_specs=pl.BlockSpec((tm, tn), lambda i, j: (i, j)),
        out_shape=jax.ShapeDtypeStruct((M, N), BF16),
        compiler_params=_params(("parallel", "arbitrary")),
    )(a, w)


def _rwkv_pre_kernel(seq_starts, seq_ends, has_vres, tm, *refs):
    (x_ref, xp_ref, xn_ref, gpre_ref, mu_ref, w0_ref, w1_ref, w2_ref,
     a0_ref, a1_ref, a2_ref, g1_ref, g2_ref) = refs[:13]
    refs = refs[13:]
    if has_vres:
        v0_ref, v1_ref, v2_ref = refs[:3]
        refs = refs[3:]
    xr_ref, xk_ref, xv_ref, ld_ref, a_ref, g_ref = refs[:6]

    t0 = pl.program_id(0) * tm
    gpre = gpre_ref[...]
    h = _rms(x_ref[...], gpre)
    keep_p = 1.0 - _in_list(t0, seq_starts).astype(F32)
    keep_n = 1.0 - _in_list(t0 + tm, seq_ends).astype(F32)
    hp = _rms(xp_ref[7:8, :], gpre) * keep_p
    hn = _rms(xn_ref[0:1, :], gpre) * keep_n
    row = lax.broadcasted_iota(jnp.int32, h.shape, 0)
    prev = jnp.where(row == 0, hp, pltpu.roll(h, 1, 0))
    nxt = jnp.where(row == tm - 1, hn, pltpu.roll(h, tm - 1, 0))
    xx = 0.5 * (prev + nxt) - h

    def mix(i):
        return (h + xx * mu_ref[i:i + 1, :]).astype(BF16)

    xr_ref[...] = mix(0)
    xw = mix(1)
    xk_ref[...] = mix(2)
    xv = mix(3)
    xv_ref[...] = xv
    xa = mix(4)
    xg = mix(5)
    for d in range(2):
        tw = jnp.tanh(jnp.dot(xw, w1_ref[d], preferred_element_type=F32))
        z = w0_ref[d:d + 1, :] + _bdot(tw, w2_ref[d])
        ld_ref[d] = (-math.exp(-0.5) * jax.nn.sigmoid(z)).astype(BF16)
        ta = jnp.dot(xa, a1_ref[d], preferred_element_type=F32)
        a_ref[d] = jax.nn.sigmoid(a0_ref[d:d + 1, :] + _bdot(ta, a2_ref[d])).astype(BF16)
    tg = jax.nn.sigmoid(jnp.dot(xg, g1_ref[...], preferred_element_type=F32))
    g_ref[...] = _bdot(tg, g2_ref[...]).astype(BF16)
    if has_vres:
        vg_ref = refs[6]
        tv = jnp.dot(xv, v1_ref[...], preferred_element_type=F32)
        vg_ref[...] = jax.nn.sigmoid(v0_ref[...] + _bdot(tv, v2_ref[...])).astype(BF16)


def _pad_lora(w_in, w_out):
    r = w_in.shape[-1]
    rp = -(-r // LANES) * LANES
    pin = [(0, 0)] * (w_in.ndim - 1) + [(0, rp - r)]
    pout = [(0, 0)] * (w_out.ndim - 2) + [(0, rp - r), (0, 0)]
    return jnp.pad(w_in, pin).astype(BF16), jnp.pad(w_out, pout).astype(BF16)


def _rwkv_pre(x, seq_starts, seq_ends, gpre, mu, w0, w1, w2, a0, a1, a2, g1, g2, vres):
    T, D = x.shape
    tm = _tile(T, 128)
    nb8 = T // 8
    has_vres = vres is not None
    w1p, w2p = _pad_lora(w1, w2)
    a1p, a2p = _pad_lora(a1, a2)
    g1p, g2p = _pad_lora(g1, g2)

    def full(a):
        nd = a.ndim
        return pl.BlockSpec(a.shape, lambda i, _nd=nd: (0,) * _nd)

    args = [x, x, x, gpre, mu, w0, w1p, w2p, a0, a1p, a2p, g1p, g2p]
    in_specs = [
        pl.BlockSpec((tm, D), lambda i: (i, 0)),
        pl.BlockSpec((8, D), lambda i: (jnp.maximum(i * (tm // 8) - 1, 0), 0)),
        pl.BlockSpec((8, D), lambda i: (jnp.minimum((i + 1) * (tm // 8), nb8 - 1), 0)),
    ] + [full(a) for a in args[3:]]
    if has_vres:
        v0, v1, v2 = vres
        v1p, v2p = _pad_lora(v1, v2)
        extra = [v0.reshape(1, D), v1p, v2p]
        args += extra
        in_specs += [full(a) for a in extra]
    tok = pl.BlockSpec((tm, D), lambda i: (i, 0))
    tok2 = pl.BlockSpec((2, tm, D), lambda i: (0, i, 0))
    out_specs = [tok, tok, tok, tok2, tok2, tok]
    out_shape = [jax.ShapeDtypeStruct((T, D), BF16)] * 3 + [
        jax.ShapeDtypeStruct((2, T, D), BF16), jax.ShapeDtypeStruct((2, T, D), BF16),
        jax.ShapeDtypeStruct((T, D), BF16)]
    if has_vres:
        out_specs.append(tok)
        out_shape.append(jax.ShapeDtypeStruct((T, D), BF16))
    return pl.pallas_call(
        functools.partial(_rwkv_pre_kernel, seq_starts, seq_ends, has_vres, tm),
        grid=(T // tm,),
        in_specs=in_specs,
        out_specs=out_specs,
        out_shape=out_shape,
        compiler_params=_params(("parallel",)),
    )(*args)


def _wkv_kernel(L, G, nc, resets_f, resets_r, has_vres, *refs):
    r_ref, k_ref, v_ref, ld_ref, a_ref, kk_ref, ka_ref = refs[:7]
    refs = refs[7:]
    if has_vres:
        vf_ref, vg_ref = refs[:2]
        refs = refs[2:]
    y_ref, s_ref = refs

    d = pl.program_id(0)
    c = pl.program_id(2)
    ci = c + d * (nc - 1 - 2 * c)
    reset = ((d == 0) & _in_list(ci, resets_f)) | ((d == 1) & _in_list(ci, resets_r))

    @pl.when(reset)
    def _():
        s_ref[...] = jnp.zeros_like(s_ref)

    sgn = 1 - 2 * d
    tt = lax.broadcasted_iota(jnp.int32, (L, LANES), 0)
    ss = lax.broadcasted_iota(jnp.int32, (L, LANES), 1) % L
    delta = sgn * (tt - ss)
    incl = delta >= 0
    strict = delta > 0
    eye = (tt == ss).astype(F32)
    tri = (sgn * (lax.broadcasted_iota(jnp.int32, (L, L), 0)
                  - lax.broadcasted_iota(jnp.int32, (L, L), 1)) >= 0).astype(BF16)
    first_head = lax.broadcasted_iota(jnp.int32, (L, LANES), 1) < RWKV_HEAD
    bi = lax.broadcasted_iota(jnp.int32, (LANES, LANES), 0) // RWKV_HEAD
    bj = lax.broadcasted_iota(jnp.int32, (LANES, LANES), 1) // RWKV_HEAD
    same_head = bi == bj
    head_ones = same_head.astype(BF16)
    n_double = int(math.log2(L)) - 1

    def bd(x):
        xb = x.astype(BF16)
        zero = jnp.zeros_like(xb)
        return jnp.concatenate([jnp.where(first_head, xb, zero), jnp.where(first_head, zero, xb)], axis=0)

    pairs = []
    for p in range(G):
        sl = slice(p * LANES, (p + 1) * LANES)
        r_ = r_ref[:, sl].astype(F32)
        k_ = k_ref[:, sl].astype(F32)
        v_ = v_ref[:, sl].astype(F32)
        ld_b = ld_ref[:, sl]
        ld_ = ld_b.astype(F32)
        a_ = a_ref[:, sl].astype(F32)
        if has_vres:
            v_ = v_ + (vf_ref[:, sl].astype(F32) - v_) * vg_ref[:, sl].astype(F32)
        kkc = k_ * kk_ref[:, sl]
        n2 = jnp.dot((kkc * kkc).astype(BF16), head_ones, preferred_element_type=F32)
        kkn = kkc / jnp.maximum(jnp.sqrt(n2), 1e-12)
        kd = k_ * (1.0 + (a_ - 1.0) * ka_ref[:, sl])
        cs = jnp.dot(tri, ld_b, preferred_element_type=F32)
        w_inv = jnp.exp(-cs)
        bt = kkn * a_ * w_inv
        kt = kd * w_inv
        pairs.append(dict(
            sl=sl, v=v_, bt=bt, kt=kt, st=s_ref[p],
            lhs=jnp.concatenate([-kkn * jnp.exp(cs - ld_), r_ * jnp.exp(cs)], axis=0).astype(BF16),
            wend=jnp.exp(jnp.sum(ld_, axis=0, keepdims=True))))

    for pr in pairs:
        rhs_t = jnp.concatenate([bd(pr["bt"]), bd(pr["kt"]), pr["st"].astype(BF16)], axis=0)
        prod = _ntdot(pr["lhs"], rhs_t)
        pr["a_ab"] = jnp.where(strict, prod[:L, :LANES], 0.0)
        pr["a_rb"] = jnp.where(incl, prod[L:, :LANES], 0.0)
        a_ak = jnp.where(strict, prod[:L, LANES:2 * LANES], 0.0)
        a_rk = jnp.where(incl, prod[L:, LANES:2 * LANES], 0.0)
        pr["sp"] = prod[:, 2 * LANES:]
        pr["a_k"] = jnp.concatenate([a_ak, a_rk], axis=0)
        pr["tinv"] = eye + pr["a_ab"]
        pr["ak"] = pr["a_ab"]
    for pr in pairs:
        pr["spv"] = pr["sp"] + _bdot(pr["a_k"], bd(pr["v"]))
    for _ in range(n_double):
        for pr in pairs:
            pr["ak"] = _bdot(pr["ak"], bd(pr["ak"]))
        for pr in pairs:
            pr["tinv"] = pr["tinv"] + _bdot(pr["ak"], bd(pr["tinv"]))
    for pr in pairs:
        pr["u"] = _bdot(pr["tinv"], bd(pr["spv"][:L]))
    for pr in pairs:
        pr["y"] = pr["spv"][L:] + _bdot(pr["a_rb"], bd(pr["u"]))
    for p, pr in enumerate(pairs):
        y_ref[:, pr["sl"]] = pr["y"].astype(BF16)
        uv = jnp.concatenate([pr["u"], pr["v"]], axis=0).astype(BF16)
        bk = jnp.concatenate([pr["bt"], pr["kt"]], axis=0).astype(BF16)
        upd = lax.dot_general(uv, bk, _TN, preferred_element_type=F32)
        s_ref[p] = (pr["st"] + jnp.where(same_head, upd, 0.0)) * pr["wend"]


def _wkv(r, k, v, ld, a, k_k, k_a, vres, seq_starts, seq_ends):
    T, D = r.shape
    L = WKV_CHUNK
    npairs = D // LANES
    assert 2 * L == LANES and RWKV_HEAD == L
    G = math.gcd(npairs, WKV_PAIRS_PER_STEP)
    nc = T // L
    W = G * LANES
    resets_f = tuple(s // L for s in seq_starts)
    resets_r = tuple(e // L - 1 for e in seq_ends)
    has_vres = vres is not None

    def cidx(d, c):
        return c + d * (nc - 1 - 2 * c)

    tok = pl.BlockSpec((L, W), lambda d, p, c: (cidx(d, c), p))
    tokd = pl.BlockSpec((None, L, W), lambda d, p, c: (d, cidx(d, c), p))
    par = pl.BlockSpec((1, W), lambda d, p, c: (0, p))
    args = [r, k, v, ld, a, k_k, k_a]
    in_specs = [tok, tok, tok, tokd, tokd, par, par]
    if has_vres:
        args += list(vres)
        in_specs += [tok, tok]
    return pl.pallas_call(
        functools.partial(_wkv_kernel, L, G, nc, resets_f, resets_r, has_vres),
        grid=(2, npairs // G, nc),
        in_specs=in_specs,
        out_specs=tokd,
        out_shape=jax.ShapeDtypeStruct((2, T, D), BF16),
        scratch_shapes=[pltpu.VMEM((G, LANES, LANES), F32)],
        compiler_params=_params(("arbitrary", "arbitrary", "arbitrary")),
    )(*args)


def _rwkv_out_kernel(has_vres, *refs):
    (x_ref, y_ref, r_ref, k_ref, v_ref, a_ref, g_ref, ka_ref, rk_ref, gnw_ref, gnb_ref,
     wo_ref, gpost_ref) = refs[:13]
    refs = refs[13:]
    if has_vres:
        vf_ref, vg_ref = refs[:2]
        refs = refs[2:]
    o_ref, act_ref = refs
    D = x_ref.shape[1]

    bi = lax.broadcasted_iota(jnp.int32, (LANES, LANES), 0) // RWKV_HEAD
    bj = lax.broadcasted_iota(jnp.int32, (LANES, LANES), 1) // RWKV_HEAD
    head_ones = (bi == bj).astype(BF16)
    inv_n = 1.0 / RWKV_HEAD

    slabs = [slice(s * LANES, (s + 1) * LANES) for s in range(D // LANES)]
    ys = [y_ref[0, :, sl].astype(F32) + y_ref[1, :, sl].astype(F32) for sl in slabs]
    means = [_bdot(y, head_ones) * inv_n for y in ys]
    dys = [y - mu for y, mu in zip(ys, means)]
    vars_ = [_bdot(dy * dy, head_ones) * inv_n for dy in dys]
    rks = []
    for sl in slabs:
        a_mean = 0.5 * (a_ref[0, :, sl].astype(F32) + a_ref[1, :, sl].astype(F32))
        k_mean = k_ref[:, sl].astype(F32) * (1.0 + (a_mean - 1.0) * ka_ref[:, sl])
        rks.append(_bdot(r_ref[:, sl].astype(F32) * k_mean * rk_ref[:, sl], head_ones))
    for sl, dy, var, rk in zip(slabs, dys, vars_, rks):
        yn = dy * lax.rsqrt(var + GN_EPS) * gnw_ref[:, sl] + gnb_ref[:, sl]
        v_ = v_ref[:, sl].astype(F32)
        if has_vres:
            v_ = v_ + (vf_ref[:, sl].astype(F32) - v_) * vg_ref[:, sl].astype(F32)
        act_ref[:, sl] = ((yn + rk * v_) * g_ref[:, sl].astype(F32)).astype(BF16)
    h = jnp.dot(act_ref[...], wo_ref[...], preferred_element_type=F32)
    o_ref[...] = x_ref[...] + _rms(h, gpost_ref[...])


def _rwkv_out(x, y, r, k, v, a, g, k_a, r_k, gn_w, gn_b, w_o, gpost, vres):
    T, D = x.shape
    tm = _tile(T, 256, 16)
    has_vres = vres is not None
    tok = pl.BlockSpec((tm, D), lambda i: (i, 0))
    tok2 = pl.BlockSpec((2, tm, D), lambda i: (0, i, 0))
    par = pl.BlockSpec((1, D), lambda i: (0, 0))
    args = [x, y, r, k, v, a, g, k_a, r_k, gn_w, gn_b, w_o, gpost]
    in_specs = [tok, tok2, tok, tok, tok, tok2, tok, par, par, par, par,
                pl.BlockSpec((D, D), lambda i: (0, 0)), par]
    if has_vres:
        args += list(vres)
        in_specs += [tok, tok]
    return pl.pallas_call(
        functools.partial(_rwkv_out_kernel, has_vres),
        grid=(T // tm,),
        in_specs=in_specs,
        out_specs=tok,
        out_shape=jax.ShapeDtypeStruct((T, D), F32),
        scratch_shapes=[pltpu.VMEM((tm, D), BF16)],
        compiler_params=_params(("parallel",)),
    )(*args)


def _qkv_kernel(tm, tn, dil, x_ref, gpre_ref, w_ref, cos_ref, sa_ref, sb_ref, o_ref, xn_ref, acc_ref):
    j = pl.program_id(1)

    @pl.when(j == 0)
    def _():
        xn_ref[...] = _rms(x_ref[...], gpre_ref[...]).astype(BF16)

    cos = cos_ref[...]
    sa = sa_ref[...]
    sb = sb_ref[...]
    half = ROPE_DIM // 2
    xn = xn_ref[...]
    per_dot = MXU_COLS // ATT_HEAD
    for s in range(tn // ATT_HEAD):
        sl = slice(s * ATT_HEAD, (s + 1) * ATT_HEAD)
        if s % per_dot == 0:
            acc = jnp.dot(xn, w_ref[:, s * ATT_HEAD:s * ATT_HEAD + MXU_COLS],
                          preferred_element_type=F32)
        t = acc[:, (s % per_dot) * ATT_HEAD:(s % per_dot + 1) * ATT_HEAD]
        rot = t * cos + pltpu.roll(t, half, 1) * sa + pltpu.roll(t, ATT_HEAD - half, 1) * sb
        if dil == 1:
            o_ref[0, :, sl] = rot.astype(BF16)
        else:
            acc_ref[s] = rot
            for c in range(dil):
                o_ref[c, :, sl] = acc_ref[s, pl.ds(c, tm // dil, stride=dil), :].astype(BF16)


def _rope_tables(pos):
    half = ROPE_DIM // 2
    inv = ROPE_THETA ** (-jnp.arange(half, dtype=F32) / half)
    ang = pos.astype(F32)[:, None] * inv[None, :]
    cos, sin = jnp.cos(ang), jnp.sin(ang)
    T = pos.shape[0]
    ones = jnp.ones((T, ATT_HEAD - ROPE_DIM), F32)
    zeros = jnp.zeros((T, ATT_HEAD - ROPE_DIM), F32)
    zh = jnp.zeros((T, half), F32)
    cos_t = jnp.concatenate([cos, cos, ones], axis=1)
    sa_t = jnp.concatenate([zh, sin, zeros], axis=1)
    sb_t = jnp.concatenate([-sin, zh, zeros], axis=1)
    return (jnp.stack([cos_t, jnp.ones_like(cos_t)]), jnp.stack([sa_t, jnp.zeros_like(sa_t)]),
            jnp.stack([sb_t, jnp.zeros_like(sb_t)]))


def _qkv(x, gpre, w, tables, gi, dil):
    T, D = x.shape
    tm = _tile(T, 1024, 16 * dil)
    tn = _tile(D, 1024, MXU_COLS)
    nj = 3 * D // tn
    tab = pl.BlockSpec((None, tm, ATT_HEAD), lambda i, j: ((j * tn) // (2 * D), i, 0))
    return pl.pallas_call(
        functools.partial(_qkv_kernel, tm, tn, dil),
        grid=(T // tm, nj),
        in_specs=[
            pl.BlockSpec((tm, D), lambda i, j: (i, 0)),
            pl.BlockSpec((1, D), lambda i, j: (0, 0)),
            pl.BlockSpec((D, tn), lambda i, j: (0, gi * nj + j)),
            tab, tab, tab,
        ],
        out_specs=pl.BlockSpec((dil, tm // dil, tn), lambda i, j: (0, i, j)),
        out_shape=jax.ShapeDtypeStruct((dil, T // dil, 3 * D), BF16),
        scratch_shapes=[pltpu.VMEM((tm, D), BF16), pltpu.VMEM((tn // ATT_HEAD, tm, ATT_HEAD), F32)],
        compiler_params=_params(("parallel", "arbitrary")),
    )(x, gpre, w, *tables)


def _attn_kernel(bq, n_heads, bounds, q_ref, kp_ref, kc_ref, kn_ref, vp_ref, vc_ref, vn_ref,
                 o_ref, lse_ref):
    row0 = pl.program_id(0) * bq
    lo = jnp.int32(bounds[0])
    hi = jnp.int32(bounds[-1])
    for b in bounds[1:-1]:
        lo = jnp.where(row0 >= b, b, lo)
    for b in reversed(bounds[1:-1]):
        hi = jnp.where(row0 < b, b, hi)
    nk = bq + 2 * ATT_HALF
    qrow = row0 + lax.broadcasted_iota(jnp.int32, (bq, nk), 0)
    krow = row0 - ATT_HALF + lax.broadcasted_iota(jnp.int32, (bq, nk), 1)
    valid = (jnp.abs(krow - qrow) <= ATT_HALF) & (krow >= lo) & (krow < hi)
    scale = ATT_HEAD ** -0.5
    lane = lax.broadcasted_iota(jnp.int32, (bq, LANES), 1)
    lse_all = jnp.zeros((bq, LANES), F32)
    for h in range(n_heads):
        sl = slice(h * ATT_HEAD, (h + 1) * ATT_HEAD)
        q = q_ref[:, sl]
        kcat = jnp.concatenate([kp_ref[:, sl], kc_ref[:, sl], kn_ref[:, sl]], axis=0)
        vcat = jnp.concatenate([vp_ref[:, sl], vc_ref[:, sl], vn_ref[:, sl]], axis=0)
        s = lax.dot_general(q, kcat, _NT, preferred_element_type=F32) * scale
        s = jnp.where(valid, s, -1e30)
        m = jnp.max(s, axis=-1, keepdims=True)
        p = jnp.exp(s - m)
        l = jnp.sum(p, axis=-1, keepdims=True)
        o = jnp.dot(p.astype(BF16), vcat, preferred_element_type=F32) / l
        o_ref[:, sl] = o.astype(BF16)
        lse_all = jnp.where(lane == h, m + jnp.log(l), lse_all)
    lse_ref[...] = lse_all


def _attn_group(qkv, D, seq_starts, seq_ends):
    dil, rows, _ = qkv.shape
    bounds = tuple(s // dil for s in seq_starts) + (seq_ends[-1] // dil,)
    seg = [b - a for a, b in zip(bounds[:-1], bounds[1:])]
    bq = 128
    for s in seg:
        bq = math.gcd(bq, s)
    assert bq >= ATT_HALF, "sequence too short for the dilated band blocks"
    nb = rows // bq

    per = bq // ATT_HALF
    nh = rows // ATT_HALF

    def cur(kind):
        return pl.BlockSpec((None, bq, D), lambda i, c: (c, i, kind))

    def halo(kind, after):
        def imap(i, c):
            blk = (i + 1) * per if after else i * per - 1
            return (c, jnp.clip(blk, 0, nh - 1), kind)
        return pl.BlockSpec((None, ATT_HALF, D), imap)

    return pl.pallas_call(
        functools.partial(_attn_kernel, bq, D // ATT_HEAD, bounds),
        grid=(nb, dil),
        in_specs=[cur(0), halo(1, False), cur(1), halo(1, True),
                  halo(2, False), cur(2), halo(2, True)],
        out_specs=[pl.BlockSpec((None, bq, D), lambda i, c: (c, i, 0)),
                   pl.BlockSpec((None, bq, LANES), lambda i, c: (c, i, 0))],
        out_shape=[jax.ShapeDtypeStruct((dil, rows, D), BF16),
                   jax.ShapeDtypeStruct((dil, rows, LANES), F32)],
        compiler_params=_params(("parallel", "arbitrary")),
    )(qkv, qkv, qkv, qkv, qkv, qkv, qkv)


def _attn_out_kernel(tm, x_ref, o0_ref, o1_ref, o2_ref, l0_ref, l1_ref, l2_ref, wo_ref, gpost_ref,
                     out_ref, act_ref, *scratch):
    D = x_ref.shape[1]

    def token_major(ref, sl, s):
        dil = ref.shape[0]
        if dil == 1:
            return ref[0, :, sl].astype(F32)
        for c in range(dil):
            s[pl.ds(c, tm // dil, stride=dil), :] = ref[c, :, sl].astype(F32)
        return s[...]

    free = list(scratch)
    l0, l1, l2 = [token_major(r, slice(0, LANES), free.pop() if r.shape[0] > 1 else None)
                  for r in (l0_ref, l1_ref, l2_ref)]
    m = jnp.maximum(jnp.maximum(l0, l1), l2)
    e0, e1, e2 = jnp.exp(l0 - m), jnp.exp(l1 - m), jnp.exp(l2 - m)
    inv = 1.0 / (e0 + e1 + e2)
    w0, w1, w2 = e0 * inv, e1 * inv, e2 * inv
    h = None
    heads_per_dot = MXU_COLS // ATT_HEAD
    for k in range(D // ATT_HEAD):
        sl = slice(k * ATT_HEAD, (k + 1) * ATT_HEAD)
        pool = list(free)
        o0, o1, o2 = [token_major(r, sl, pool.pop() if r.shape[0] > 1 else None)
                      for r in (o0_ref, o1_ref, o2_ref)]
        act_ref[:, sl] = (w0[:, k:k + 1] * o0 + w1[:, k:k + 1] * o1
                          + w2[:, k:k + 1] * o2).astype(BF16)
        if (k + 1) % heads_per_dot == 0:
            ks = slice((k + 1 - heads_per_dot) * ATT_HEAD, (k + 1) * ATT_HEAD)
            part = jnp.dot(act_ref[:, ks], wo_ref[ks, :], preferred_element_type=F32)
            h = part if h is None else h + part
    out_ref[...] = x_ref[...] + _rms(h, gpost_ref[...])


def _attn_out(x, os_, lses, w_o, gpost):
    T, D = x.shape
    tm = _tile(T, 256, 16 * max(o.shape[0] for o in os_))
    tok = pl.BlockSpec((tm, D), lambda i: (i, 0))

    def planes(a):
        dil = a.shape[0]
        return pl.BlockSpec((dil, tm // dil, a.shape[2]), lambda i: (0, i, 0))

    n_scratch = sum(2 for o in os_ if o.shape[0] > 1)
    return pl.pallas_call(
        functools.partial(_attn_out_kernel, tm),
        grid=(T // tm,),
        in_specs=[tok] + [planes(a) for a in os_] + [planes(a) for a in lses]
        + [pl.BlockSpec((D, D), lambda i: (0, 0)), pl.BlockSpec((1, D), lambda i: (0, 0))],
        out_specs=tok,
        out_shape=jax.ShapeDtypeStruct((T, D), F32),
        scratch_shapes=[pltpu.VMEM((tm, D), BF16)] + [pltpu.VMEM((tm, LANES), F32)] * n_scratch,
        compiler_params=_params(("parallel",)),
    )(x, *os_, *lses, w_o, gpost)


def kernel(x_prompt, x_sample, norm_pre, norm_post, ffn_w_gate, ffn_w_up, ffn_w_down, rwkv_mu, rwkv_w_rkv, rwkv_w0, rwkv_w1, rwkv_w2, rwkv_a0, rwkv_a1, rwkv_a2, rwkv_v0, rwkv_v1, rwkv_v2, rwkv_g1, rwkv_g2, rwkv_k_k, rwkv_k_a, rwkv_r_k, rwkv_gn_w, rwkv_gn_b, rwkv_w_o, attn_w_qkv, attn_w_o):
    D = x_prompt.shape[-1]
    depth = norm_pre.shape[0]
    seqs = [x_prompt.shape[1]] * x_prompt.shape[0] + [x_sample.shape[1]] * x_sample.shape[0]
    seq_starts, seq_ends, t = [], [], 0
    for s in seqs:
        seq_starts.append(t)
        t += s
        seq_ends.append(t)
    seq_starts, seq_ends = tuple(seq_starts), tuple(seq_ends)
    x = jnp.concatenate([x_prompt.reshape(-1, D), x_sample.reshape(-1, D)], axis=0)
    pos = jnp.concatenate([jnp.arange(s, dtype=jnp.int32) for s in seqs])
    tables = _rope_tables(pos)

    def vec(a):
        return a.reshape(1, D)

    v_first = None
    for layer in range(depth):
        x = _ffn(x, vec(norm_pre[layer, 0]), ffn_w_gate[layer, 0].astype(BF16),
                 ffn_w_up[layer, 0].astype(BF16), ffn_w_down[layer, 0].astype(BF16),
                 vec(norm_post[layer, 0]))
        j = layer // 2
        if layer % 2 == 0:
            vres_w = None if j == 0 else (rwkv_v0[j - 1], rwkv_v1[j - 1], rwkv_v2[j - 1])
            pre = _rwkv_pre(x, seq_starts, seq_ends, vec(norm_pre[layer, 1]), rwkv_mu[j],
                            rwkv_w0[j], rwkv_w1[j], rwkv_w2[j], rwkv_a0[j], rwkv_a1[j], rwkv_a2[j],
                            rwkv_g1[j], rwkv_g2[j], vres_w)
            xr, xk, xv, ld, a, g = pre[:6]
            r = _mm(xr, rwkv_w_rkv[j, 0].astype(BF16))
            k = _mm(xk, rwkv_w_rkv[j, 1].astype(BF16))
            v = _mm(xv, rwkv_w_rkv[j, 2].astype(BF16))
            if j == 0:
                v_first, vres = v, None
            else:
                vres = (v_first, pre[6])
            y = _wkv(r, k, v, ld, a, vec(rwkv_k_k[j]), vec(rwkv_k_a[j]), vres, seq_starts, seq_ends)
            x = _rwkv_out(x, y, r, k, v, a, g, vec(rwkv_k_a[j]), vec(rwkv_r_k[j]),
                          vec(rwkv_gn_w[j]), vec(rwkv_gn_b[j]), rwkv_w_o[j].astype(BF16),
                          vec(norm_post[layer, 1]), vres)
        else:
            w_qkv = attn_w_qkv[j].astype(BF16)
            os_, lses = [], []
            for gi, (_, dil) in enumerate(DILATED_GROUPS):
                qkv = _qkv(x, vec(norm_pre[layer, 1]), w_qkv, tables, gi, dil)
                o, lse = _attn_group(qkv, D, seq_starts, seq_ends)
                os_.append(o)
                lses.append(lse)
            x = _attn_out(x, os_, lses, attn_w_o[j].astype(BF16), vec(norm_post[layer, 1]))
        x = _ffn(x, vec(norm_pre[layer, 2]), ffn_w_gate[layer, 1].astype(BF16),
                 ffn_w_up[layer, 1].astype(BF16), ffn_w_down[layer, 1].astype(BF16),
                 vec(norm_post[layer, 2]))
    n_p = x_prompt.shape[0] * x_prompt.shape[1]
    return (x[:n_p].reshape(x_prompt.shape), x[n_p:].reshape(x_sample.shape))
```

```python
import functools
import math

import jax
import jax.numpy as jnp
from jax import lax
from jax.experimental import pallas as pl
from jax.experimental.pallas import tpu as pltpu

F32 = jnp.float32
BF16 = jnp.bfloat16

NORM_EPS = 1e-6
GN_EPS = 64e-5
RWKV_HEAD = 64
ATT_HEAD = 128
ROPE_DIM = ATT_HEAD // 4
ROPE_THETA = 500000.0
DILATED_GROUPS = ((128, 1), (512, 4), (2048, 16))
N_GROUPS = len(DILATED_GROUPS)
ATT_HALF = 64
LANES = 128
MXU_COLS = 256
WKV_CHUNK = 64
WKV_PAIRS_PER_STEP = 16
VMEM_LIMIT = 56 * 1024 * 1024

_NT = (((1,), (1,)), ((), ()))
_TN = (((0,), (0,)), ((), ()))


def _tile(n, pref, mult=8):
    if n <= pref:
        return n
    t = (pref // mult) * mult
    while t >= mult:
        if n % t == 0:
            return t
        t -= mult
    return n


def _in_list(v, lst):
    r = v == lst[0]
    for s in lst[1:]:
        r = r | (v == s)
    return r


def _rms(x, g):
    return x * lax.rsqrt(jnp.mean(x * x, axis=-1, keepdims=True) + NORM_EPS) * g


NORM_ROWS = 32


def _rms_rows(src_ref, g, emit):
    for r0 in range(0, src_ref.shape[0], NORM_ROWS):
        rows = slice(r0, min(r0 + NORM_ROWS, src_ref.shape[0]))
        emit(rows, _rms(src_ref[rows, :], g))


def _bdot(a, b):
    return jnp.dot(a.astype(BF16), b.astype(BF16), preferred_element_type=F32)


def _ntdot(a, b):
    return lax.dot_general(a, b, _NT, preferred_element_type=F32)


def _params(sem):
    return pltpu.CompilerParams(dimension_semantics=sem, vmem_limit_bytes=VMEM_LIMIT)


def _ffn_kernel(x_ref, gpre_ref, wg_ref, wu_ref, wd_ref, gpost_ref, o_ref, xn_ref, acc_ref):
    j = pl.program_id(1)

    @pl.when(j == 0)
    def _():
        def put(rows, y):
            xn_ref[rows, :] = y.astype(BF16)
        _rms_rows(x_ref, gpre_ref[...], put)
        acc_ref[...] = jnp.zeros_like(acc_ref)

    xn = xn_ref[...]
    g = jnp.dot(xn, wg_ref[...], preferred_element_type=F32)
    u = jnp.dot(xn, wu_ref[...], preferred_element_type=F32)
    h = (g * jax.nn.sigmoid(g)) * u
    acc_ref[...] += jnp.dot(h.astype(BF16), wd_ref[...], preferred_element_type=F32)

    @pl.when(j == pl.num_programs(1) - 1)
    def _():
        def put(rows, y):
            o_ref[rows, :] = x_ref[rows, :] + y
        _rms_rows(acc_ref, 0.5 * gpost_ref[...], put)


def _ffn(x, gpre, wg, wu, wd, gpost):
    T, D = x.shape
    F = wg.shape[1]
    tm = _tile(T, 768, MXU_COLS)
    tf = _tile(F, 512, LANES)
    return pl.pallas_call(
        _ffn_kernel,
        grid=(T // tm, F // tf),
        in_specs=[
            pl.BlockSpec((tm, D), lambda i, j: (i, 0)),
            pl.BlockSpec((1, D), lambda i, j: (0, 0)),
            pl.BlockSpec((D, tf), lambda i, j: (0, j)),
            pl.BlockSpec((D, tf), lambda i, j: (0, j)),
            pl.BlockSpec((tf, D), lambda i, j: (j, 0)),
            pl.BlockSpec((1, D), lambda i, j: (0, 0)),
        ],
        out_specs=pl.BlockSpec((tm, D), lambda i, j: (i, 0)),
        out_shape=jax.ShapeDtypeStruct((T, D), F32),
        scratch_shapes=[pltpu.VMEM((tm, D), BF16), pltpu.VMEM((tm, D), F32)],
        compiler_params=_params(("parallel", "arbitrary")),
    )(x, gpre, wg, wu, wd, gpost)


def _mm_kernel(a_ref, w_ref, o_ref):
    a = a_ref[...]
    for c in range(0, o_ref.shape[1], MXU_COLS):
        o_ref[:, c:c + MXU_COLS] = jnp.dot(
            a, w_ref[:, c:c + MXU_COLS], preferred_element_type=F32).astype(o_ref.dtype)


def _mm(a, w):
    M, K = a.shape
    N = w.shape[1]
    tm = _tile(M, 1024)
    tn = _tile(N, 1024, MXU_COLS)
    return pl.pallas_call(
        _mm_kernel,
        grid=(M // tm, N // tn),
        in_specs=[
            pl.BlockSpec((tm, K), lambda i, j: (i, 0)),
            pl.BlockSpec((K, tn), lambda i, j: (0, j)),
        ],
        out_specs=pl.BlockSpec((tm, tn), lambda i, j: (i, j)),
        out_shape=jax.ShapeDtypeStruct((M, N), BF16),
        compiler_params=_params(("parallel", "arbitrary")),
    )(a, w)


def _rwkv_pre_kernel(seq_starts, seq_ends, has_vres, tm, *refs):
    (x_ref, xp_ref, xn_ref, gpre_ref, mu_ref, w0_ref, w1_ref, w2_ref,
     a0_ref, a1_ref, a2_ref, g1_ref, g2_ref) = refs[:13]
    refs = refs[13:]
    if has_vres:
        v0_ref, v1_ref, v2_ref = refs[:3]
        refs = refs[3:]
    xr_ref, xk_ref, xv_ref, ld_ref, a_ref, g_ref = refs[:6]

    t0 = pl.program_id(0) * tm
    gpre = gpre_ref[...]
    h = _rms(x_ref[...], gpre)
    keep_p = 1.0 - _in_list(t0, seq_starts).astype(F32)
    keep_n = 1.0 - _in_list(t0 + tm, seq_ends).astype(F32)
    hp = _rms(xp_ref[7:8, :], gpre) * keep_p
    hn = _rms(xn_ref[0:1, :], gpre) * keep_n
    row = lax.broadcasted_iota(jnp.int32, h.shape, 0)
    prev = jnp.where(row == 0, hp, pltpu.roll(h, 1, 0))
    nxt = jnp.where(row == tm - 1, hn, pltpu.roll(h, tm - 1, 0))
    xx = 0.5 * (prev + nxt) - h

    def mix(i):
        return (h + xx * mu_ref[i:i + 1, :]).astype(BF16)

    xr_ref[...] = mix(0)
    xw = mix(1)
    xk_ref[...] = mix(2)
    xv = mix(3)
    xv_ref[...] = xv
    xa = mix(4)
    xg = mix(5)
    for d in range(2):
        tw = jnp.tanh(jnp.dot(xw, w1_ref[d], preferred_element_type=F32))
        z = w0_ref[d:d + 1, :] + _bdot(tw, w2_ref[d])
        ld_ref[d] = (-math.exp(-0.5) * jax.nn.sigmoid(z)).astype(BF16)
        ta = jnp.dot(xa, a1_ref[d], preferred_element_type=F32)
        a_ref[d] = jax.nn.sigmoid(a0_ref[d:d + 1, :] + _bdot(ta, a2_ref[d])).astype(BF16)
    tg = jax.nn.sigmoid(jnp.dot(xg, g1_ref[...], preferred_element_type=F32))
    g_ref[...] = _bdot(tg, g2_ref[...]).astype(BF16)
    if has_vres:
        vg_ref = refs[6]
        tv = jnp.dot(xv, v1_ref[...], preferred_element_type=F32)
        vg_ref[...] = jax.nn.sigmoid(v0_ref[...] + _bdot(tv, v2_ref[...])).astype(BF16)


def _pad_lora(w_in, w_out):
    r = w_in.shape[-1]
    rp = -(-r // LANES) * LANES
    pin = [(0, 0)] * (w_in.ndim - 1) + [(0, rp - r)]
    pout = [(0, 0)] * (w_out.ndim - 2) + [(0, rp - r), (0, 0)]
    return jnp.pad(w_in, pin).astype(BF16), jnp.pad(w_out, pout).astype(BF16)


def _rwkv_pre(x, seq_starts, seq_ends, gpre, mu, w0, w1, w2, a0, a1, a2, g1, g2, vres):
    T, D = x.shape
    tm = _tile(T, 128)
    nb8 = T // 8
    has_vres = vres is not None
    w1p, w2p = _pad_lora(w1, w2)
    a1p, a2p = _pad_lora(a1, a2)
    g1p, g2p = _pad_lora(g1, g2)

    def full(a):
        nd = a.ndim
        return pl.BlockSpec(a.shape, lambda i, _nd=nd: (0,) * _nd)

    args = [x, x, x, gpre, mu, w0, w1p, w2p, a0, a1p, a2p, g1p, g2p]
    in_specs = [
        pl.BlockSpec((tm, D), lambda i: (i, 0)),
        pl.BlockSpec((8, D), lambda i: (jnp.maximum(i * (tm // 8) - 1, 0), 0)),
        pl.BlockSpec((8, D), lambda i: (jnp.minimum((i + 1) * (tm // 8), nb8 - 1), 0)),
    ] + [full(a) for a in args[3:]]
    if has_vres:
        v0, v1, v2 = vres
        v1p, v2p = _pad_lora(v1, v2)
        extra = [v0.reshape(1, D), v1p, v2p]
        args += extra
        in_specs += [full(a) for a in extra]
    tok = pl.BlockSpec((tm, D), lambda i: (i, 0))
    tok2 = pl.BlockSpec((2, tm, D), lambda i: (0, i, 0))
    out_specs = [tok, tok, tok, tok2, tok2, tok]
    out_shape = [jax.ShapeDtypeStruct((T, D), BF16)] * 3 + [
        jax.ShapeDtypeStruct((2, T, D), BF16), jax.ShapeDtypeStruct((2, T, D), BF16),
        jax.ShapeDtypeStruct((T, D), BF16)]
    if has_vres:
        out_specs.append(tok)
        out_shape.append(jax.ShapeDtypeStruct((T, D), BF16))
    return pl.pallas_call(
        functools.partial(_rwkv_pre_kernel, seq_starts, seq_ends, has_vres, tm),
        grid=(T // tm,),
        in_specs=in_specs,
        out_specs=out_specs,
        out_shape=out_shape,
        compiler_params=_params(("parallel",)),
    )(*args)


def _wkv_kernel(L, G, nc, resets_f, resets_r, has_vres, *refs):
    r_ref, k_ref, v_ref, ld_ref, a_ref, kk_ref, ka_ref = refs[:7]
    refs = refs[7:]
    if has_vres:
        vf_ref, vg_ref = refs[:2]
        refs = refs[2:]
    y_ref, s_ref = refs

    d = pl.program_id(0)
    c = pl.program_id(2)
    ci = c + d * (nc - 1 - 2 * c)
    reset = ((d == 0) & _in_list(ci, resets_f)) | ((d == 1) & _in_list(ci, resets_r))

    @pl.when(reset)
    def _():
        s_ref[...] = jnp.zeros_like(s_ref)

    sgn = 1 - 2 * d
    tt = lax.broadcasted_iota(jnp.int32, (L, LANES), 0)
    ss = lax.broadcasted_iota(jnp.int32, (L, LANES), 1) % L
    delta = sgn * (tt - ss)
    incl = delta >= 0
    strict = delta > 0
    eye = (tt == ss).astype(F32)
    tri = (sgn * (lax.broadcasted_iota(jnp.int32, (L, L), 0)
                  - lax.broadcasted_iota(jnp.int32, (L, L), 1)) >= 0).astype(BF16)
    first_head = lax.broadcasted_iota(jnp.int32, (L, LANES), 1) < RWKV_HEAD
    bi = lax.broadcasted_iota(jnp.int32, (LANES, LANES), 0) // RWKV_HEAD
    bj = lax.broadcasted_iota(jnp.int32, (LANES, LANES), 1) // RWKV_HEAD
    same_head = bi == bj
    head_ones = same_head.astype(BF16)
    n_double = int(math.log2(L)) - 1

    def bd(x):
        xb = x.astype(BF16)
        zero = jnp.zeros_like(xb)
        return jnp.concatenate([jnp.where(first_head, xb, zero), jnp.where(first_head, zero, xb)], axis=0)

    pairs = []
    for p in range(G):
        sl = slice(p * LANES, (p + 1) * LANES)
        r_ = r_ref[:, sl].astype(F32)
        k_ = k_ref[:, sl].astype(F32)
        v_ = v_ref[:, sl].astype(F32)
        ld_b = ld_ref[:, sl]
        ld_ = ld_b.astype(F32)
        a_ = a_ref[:, sl].astype(F32)
        if has_vres:
            v_ = v_ + (vf_ref[:, sl].astype(F32) - v_) * vg_ref[:, sl].astype(F32)
        kkc = k_ * kk_ref[:, sl]
        n2 = jnp.dot((kkc * kkc).astype(BF16), head_ones, preferred_element_type=F32)
        kkn = kkc / jnp.maximum(jnp.sqrt(n2), 1e-12)
        kd = k_ * (1.0 + (a_ - 1.0) * ka_ref[:, sl])
        cs = jnp.dot(tri, ld_b, preferred_element_type=F32)
        w_inv = jnp.exp(-cs)
        bt = kkn * a_ * w_inv
        kt = kd * w_inv
        pairs.append(dict(
            sl=sl, v=v_, bt=bt, kt=kt, st=s_ref[p],
            lhs=jnp.concatenate([-kkn * jnp.exp(cs - ld_), r_ * jnp.exp(cs)], axis=0).astype(BF16),
            wend=jnp.exp(jnp.sum(ld_, axis=0, keepdims=True))))

    for pr in pairs:
        rhs_t = jnp.concatenate([bd(pr["bt"]), bd(pr["kt"]), pr["st"].astype(BF16)], axis=0)
        prod = _ntdot(pr["lhs"], rhs_t)
        pr["a_ab"] = jnp.where(strict, prod[:L, :LANES], 0.0)
        pr["a_rb"] = jnp.where(incl, prod[L:, :LANES], 0.0)
        a_ak = jnp.where(strict, prod[:L, LANES:2 * LANES], 0.0)
        a_rk = jnp.where(incl, prod[L:, LANES:2 * LANES], 0.0)
        pr["sp"] = prod[:, 2 * LANES:]
        pr["a_k"] = jnp.concatenate([a_ak, a_rk], axis=0)
        pr["tinv"] = eye + pr["a_ab"]
        pr["ak"] = pr["a_ab"]
    for pr in pairs:
        pr["spv"] = pr["sp"] + _bdot(pr["a_k"], bd(pr["v"]))
    for _ in range(n_double):
        for pr in pairs:
            pr["ak"] = _bdot(pr["ak"], bd(pr["ak"]))
        for pr in pairs:
            pr["tinv"] = pr["tinv"] + _bdot(pr["ak"], bd(pr["tinv"]))
    for pr in pairs:
        pr["u"] = _bdot(pr["tinv"], bd(pr["spv"][:L]))
    for pr in pairs:
        pr["y"] = pr["spv"][L:] + _bdot(pr["a_rb"], bd(pr["u"]))
    for p, pr in enumerate(pairs):
        y_ref[:, pr["sl"]] = pr["y"].astype(BF16)
        uv = jnp.concatenate([pr["u"], pr["v"]], axis=0).astype(BF16)
        bk = jnp.concatenate([pr["bt"], pr["kt"]], axis=0).astype(BF16)
        upd = lax.dot_general(uv, bk, _TN, preferred_element_type=F32)
        s_ref[p] = (pr["st"] + jnp.where(same_head, upd, 0.0)) * pr["wend"]


def _wkv(r, k, v, ld, a, k_k, k_a, vres, seq_starts, seq_ends):
    T, D = r.shape
    L = WKV_CHUNK
    npairs = D // LANES
    assert 2 * L == LANES and RWKV_HEAD == L
    G = math.gcd(npairs, WKV_PAIRS_PER_STEP)
    nc = T // L
    W = G * LANES
    resets_f = tuple(s // L for s in seq_starts)
    resets_r = tuple(e // L - 1 for e in seq_ends)
    has_vres = vres is not None

    def cidx(d, c):
        return c + d * (nc - 1 - 2 * c)

    tok = pl.BlockSpec((L, W), lambda d, p, c: (cidx(d, c), p))
    tokd = pl.BlockSpec((None, L, W), lambda d, p, c: (d, cidx(d, c), p))
    par = pl.BlockSpec((1, W), lambda d, p, c: (0, p))
    args = [r, k, v, ld, a, k_k, k_a]
    in_specs = [tok, tok, tok, tokd, tokd, par, par]
    if has_vres:
        args += list(vres)
        in_specs += [tok, tok]
    return pl.pallas_call(
        functools.partial(_wkv_kernel, L, G, nc, resets_f, resets_r, has_vres),
        grid=(2, npairs // G, nc),
        in_specs=in_specs,
        out_specs=tokd,
        out_shape=jax.ShapeDtypeStruct((2, T, D), BF16),
        scratch_shapes=[pltpu.VMEM((G, LANES, LANES), F32)],
        compiler_params=_params(("arbitrary", "arbitrary", "arbitrary")),
    )(*args)


def _rwkv_out_kernel(has_vres, *refs):
    (x_ref, y_ref, r_ref, k_ref, v_ref, a_ref, g_ref, ka_ref, rk_ref, gnw_ref, gnb_ref,
     wo_ref, gpost_ref) = refs[:13]
    refs = refs[13:]
    if has_vres:
        vf_ref, vg_ref = refs[:2]
        refs = refs[2:]
    o_ref, act_ref = refs
    D = x_ref.shape[1]

    bi = lax.broadcasted_iota(jnp.int32, (LANES, LANES), 0) // RWKV_HEAD
    bj = lax.broadcasted_iota(jnp.int32, (LANES, LANES), 1) // RWKV_HEAD
    head_ones = (bi == bj).astype(BF16)
    inv_n = 1.0 / RWKV_HEAD

    slabs = [slice(s * LANES, (s + 1) * LANES) for s in range(D // LANES)]
    ys = [y_ref[0, :, sl].astype(F32) + y_ref[1, :, sl].astype(F32) for sl in slabs]
    means = [_bdot(y, head_ones) * inv_n for y in ys]
    dys = [y - mu for y, mu in zip(ys, means)]
    vars_ = [_bdot(dy * dy, head_ones) * inv_n for dy in dys]
    rks = []
    for sl in slabs:
        a_mean = 0.5 * (a_ref[0, :, sl].astype(F32) + a_ref[1, :, sl].astype(F32))
        k_mean = k_ref[:, sl].astype(F32) * (1.0 + (a_mean - 1.0) * ka_ref[:, sl])
        rks.append(_bdot(r_ref[:, sl].astype(F32) * k_mean * rk_ref[:, sl], head_ones))
    for sl, dy, var, rk in zip(slabs, dys, vars_, rks):
        yn = dy * lax.rsqrt(var + GN_EPS) * gnw_ref[:, sl] + gnb_ref[:, sl]
        v_ = v_ref[:, sl].astype(F32)
        if has_vres:
            v_ = v_ + (vf_ref[:, sl].astype(F32) - v_) * vg_ref[:, sl].astype(F32)
        act_ref[:, sl] = ((yn + rk * v_) * g_ref[:, sl].astype(F32)).astype(BF16)
    h = jnp.dot(act_ref[...], wo_ref[...], preferred_element_type=F32)
    o_ref[...] = x_ref[...] + _rms(h, gpost_ref[...])


def _rwkv_out(x, y, r, k, v, a, g, k_a, r_k, gn_w, gn_b, w_o, gpost, vres):
    T, D = x.shape
    tm = _tile(T, 256, 16)
    has_vres = vres is not None
    tok = pl.BlockSpec((tm, D), lambda i: (i, 0))
    tok2 = pl.BlockSpec((2, tm, D), lambda i: (0, i, 0))
    par = pl.BlockSpec((1, D), lambda i: (0, 0))
    args = [x, y, r, k, v, a, g, k_a, r_k, gn_w, gn_b, w_o, gpost]
    in_specs = [tok, tok2, tok, tok, tok, tok2, tok, par, par, par, par,
                pl.BlockSpec((D, D), lambda i: (0, 0)), par]
    if has_vres:
        args += list(vres)
        in_specs += [tok, tok]
    return pl.pallas_call(
        functools.partial(_rwkv_out_kernel, has_vres),
        grid=(T // tm,),
        in_specs=in_specs,
        out_specs=tok,
        out_shape=jax.ShapeDtypeStruct((T, D), F32),
        scratch_shapes=[pltpu.VMEM((tm, D), BF16)],
        compiler_params=_params(("parallel",)),
    )(*args)


def _qkv_kernel(tm, tn, dil, x_ref, gpre_ref, w_ref, cos_ref, sin_ref, o_ref, xn_ref, acc_ref):
    j = pl.program_id(1)

    @pl.when(j == 0)
    def _():
        def put(rows, y):
            xn_ref[rows, :] = y.astype(BF16)
        _rms_rows(x_ref, gpre_ref[...], put)

    cos = cos_ref[...]
    sin = sin_ref[...]
    xn = xn_ref[...]
    per_dot = MXU_COLS // ATT_HEAD
    for s in range(tn // ATT_HEAD):
        sl = slice(s * ATT_HEAD, (s + 1) * ATT_HEAD)
        if s % per_dot == 0:
            acc = jnp.dot(xn, w_ref[:, s * ATT_HEAD:s * ATT_HEAD + MXU_COLS],
                          preferred_element_type=F32)
        t = acc[:, (s % per_dot) * ATT_HEAD:(s % per_dot + 1) * ATT_HEAD]
        rot = t * cos + pltpu.roll(t, ATT_HEAD // 2, 1) * sin
        if dil == 1:
            o_ref[0, :, sl] = rot.astype(BF16)
        else:
            acc_ref[s] = rot
            for c in range(dil):
                o_ref[c, :, sl] = acc_ref[s, pl.ds(c, tm // dil, stride=dil), :].astype(BF16)


def _rope_tables(pos):
    half = ROPE_DIM // 2
    inv = ROPE_THETA ** (-jnp.arange(half, dtype=F32) / half)
    ang = pos.astype(F32)[:, None] * inv[None, :]
    cos, sin = jnp.cos(ang), jnp.sin(ang)
    T = pos.shape[0]
    gap = ATT_HEAD // 2 - half
    ones = jnp.ones((T, gap), F32)
    zeros = jnp.zeros((T, gap), F32)
    cos_t = jnp.concatenate([cos, ones, cos, ones], axis=1)
    sin_t = jnp.concatenate([-sin, zeros, sin, zeros], axis=1)
    return (jnp.stack([cos_t, jnp.ones_like(cos_t)]), jnp.stack([sin_t, jnp.zeros_like(sin_t)]))


def _rope_perm(w_qkv, D):
    half = ROPE_DIM // 2
    mid = ATT_HEAD // 2
    idx = list(range(ATT_HEAD))
    idx[half:2 * half], idx[mid:mid + half] = idx[mid:mid + half], idx[half:2 * half]
    w = w_qkv.reshape(w_qkv.shape[0], N_GROUPS, 3, D // ATT_HEAD, ATT_HEAD)
    qk = jnp.take(w[:, :, :2], jnp.asarray(idx, jnp.int32), axis=-1)
    return jnp.concatenate([qk, w[:, :, 2:]], axis=2).reshape(w_qkv.shape)


def _qkv(x, gpre, w, tables, gi, dil):
    T, D = x.shape
    tm = _tile(T, 512, 16 * dil)
    tn = _tile(D, 2048, MXU_COLS)
    nj = 3 * D // tn
    tab = pl.BlockSpec((None, tm, ATT_HEAD), lambda i, j: ((j * tn) // (2 * D), i, 0))
    return pl.pallas_call(
        functools.partial(_qkv_kernel, tm, tn, dil),
        grid=(T // tm, nj),
        in_specs=[
            pl.BlockSpec((tm, D), lambda i, j: (i, 0)),
            pl.BlockSpec((1, D), lambda i, j: (0, 0)),
            pl.BlockSpec((D, tn), lambda i, j: (0, gi * nj + j)),
            tab, tab,
        ],
        out_specs=pl.BlockSpec((dil, tm // dil, tn), lambda i, j: (0, i, j)),
        out_shape=jax.ShapeDtypeStruct((dil, T // dil, 3 * D), BF16),
        scratch_shapes=[pltpu.VMEM((tm, D), BF16), pltpu.VMEM((tn // ATT_HEAD, tm, ATT_HEAD), F32)],
        compiler_params=_params(("parallel", "arbitrary")),
    )(x, gpre, w, *tables)


def _attn_kernel(bq, n_heads, bounds, q_ref, kp_ref, kc_ref, kn_ref, vp_ref, vc_ref, vn_ref,
                 o_ref, lse_ref):
    row0 = pl.program_id(0) * bq
    lo = jnp.int32(bounds[0])
    hi = jnp.int32(bounds[-1])
    for b in bounds[1:-1]:
        lo = jnp.where(row0 >= b, b, lo)
    for b in reversed(bounds[1:-1]):
        hi = jnp.where(row0 < b, b, hi)
    nk = bq + 2 * ATT_HALF
    qrow = row0 + lax.broadcasted_iota(jnp.int32, (bq, nk), 0)
    krow = row0 - ATT_HALF + lax.broadcasted_iota(jnp.int32, (bq, nk), 1)
    valid = (jnp.abs(krow - qrow) <= ATT_HALF) & (krow >= lo) & (krow < hi)
    scale = ATT_HEAD ** -0.5
    lane = lax.broadcasted_iota(jnp.int32, (bq, LANES), 1)
    lse_all = jnp.zeros((bq, LANES), F32)
    for h in range(n_heads):
        sl = slice(h * ATT_HEAD, (h + 1) * ATT_HEAD)
        q = q_ref[:, sl]
        kcat = jnp.concatenate([kp_ref[:, sl], kc_ref[:, sl], kn_ref[:, sl]], axis=0)
        vcat = jnp.concatenate([vp_ref[:, sl], vc_ref[:, sl], vn_ref[:, sl]], axis=0)
        s = lax.dot_general(q, kcat, _NT, preferred_element_type=F32) * scale
        s = jnp.where(valid, s, -1e30)
        m = jnp.max(s, axis=-1, keepdims=True)
        p = jnp.exp(s - m)
        l = jnp.sum(p, axis=-1, keepdims=True)
        o = jnp.dot(p.astype(BF16), vcat, preferred_element_type=F32) / l
        o_ref[:, sl] = o.astype(BF16)
        lse_all = jnp.where(lane == h, m + jnp.log(l), lse_all)
    lse_ref[...] = lse_all


def _attn_group(qkv, D, seq_starts, seq_ends):
    dil, rows, _ = qkv.shape
    bounds = tuple(s // dil for s in seq_starts) + (seq_ends[-1] // dil,)
    seg = [b - a for a, b in zip(bounds[:-1], bounds[1:])]
    bq = 128
    for s in seg:
        bq = math.gcd(bq, s)
    assert bq >= ATT_HALF, "sequence too short for the dilated band blocks"
    nb = rows // bq

    per = bq // ATT_HALF
    nh = rows // ATT_HALF

    def cur(kind):
        return pl.BlockSpec((None, bq, D), lambda i, c: (c, i, kind))

    def halo(kind, after):
        def imap(i, c):
            blk = (i + 1) * per if after else i * per - 1
            return (c, jnp.clip(blk, 0, nh - 1), kind)
        return pl.BlockSpec((None, ATT_HALF, D), imap)

    return pl.pallas_call(
        functools.partial(_attn_kernel, bq, D // ATT_HEAD, bounds),
        grid=(nb, dil),
        in_specs=[cur(0), halo(1, False), cur(1), halo(1, True),
                  halo(2, False), cur(2), halo(2, True)],
        out_specs=[pl.BlockSpec((None, bq, D), lambda i, c: (c, i, 0)),
                   pl.BlockSpec((None, bq, LANES), lambda i, c: (c, i, 0))],
        out_shape=[jax.ShapeDtypeStruct((dil, rows, D), BF16),
                   jax.ShapeDtypeStruct((dil, rows, LANES), F32)],
        compiler_params=_params(("parallel", "arbitrary")),
    )(qkv, qkv, qkv, qkv, qkv, qkv, qkv)


def _attn_out_kernel(tm, x_ref, o0_ref, o1_ref, o2_ref, l0_ref, l1_ref, l2_ref, wo_ref, gpost_ref,
                     out_ref, act_ref, *scratch):
    D = x_ref.shape[1]

    def token_major(ref, sl, s):
        dil = ref.shape[0]
        if dil == 1:
            return ref[0, :, sl].astype(F32)
        for c in range(dil):
            s[pl.ds(c, tm // dil, stride=dil), :] = ref[c, :, sl].astype(F32)
        return s[...]

    free = list(scratch)
    l0, l1, l2 = [token_major(r, slice(0, LANES), free.pop() if r.shape[0] > 1 else None)
                  for r in (l0_ref, l1_ref, l2_ref)]
    m = jnp.maximum(jnp.maximum(l0, l1), l2)
    e0, e1, e2 = jnp.exp(l0 - m), jnp.exp(l1 - m), jnp.exp(l2 - m)
    inv = 1.0 / (e0 + e1 + e2)
    w0, w1, w2 = e0 * inv, e1 * inv, e2 * inv
    h = None
    heads_per_dot = MXU_COLS // ATT_HEAD
    for k in range(D // ATT_HEAD):
        sl = slice(k * ATT_HEAD, (k + 1) * ATT_HEAD)
        pool = list(free)
        o0, o1, o2 = [token_major(r, sl, pool.pop() if r.shape[0] > 1 else None)
                      for r in (o0_ref, o1_ref, o2_ref)]
        act_ref[:, sl] = (w0[:, k:k + 1] * o0 + w1[:, k:k + 1] * o1
                          + w2[:, k:k + 1] * o2).astype(BF16)
        if (k + 1) % heads_per_dot == 0:
            ks = slice((k + 1 - heads_per_dot) * ATT_HEAD, (k + 1) * ATT_HEAD)
            part = jnp.dot(act_ref[:, ks], wo_ref[ks, :], preferred_element_type=F32)
            h = part if h is None else h + part
    out_ref[...] = x_ref[...] + _rms(h, gpost_ref[...])


def _attn_out(x, os_, lses, w_o, gpost):
    T, D = x.shape
    tm = _tile(T, 256, 16 * max(o.shape[0] for o in os_))
    tok = pl.BlockSpec((tm, D), lambda i: (i, 0))

    def planes(a):
        dil = a.shape[0]
        return pl.BlockSpec((dil, tm // dil, a.shape[2]), lambda i: (0, i, 0))

    n_scratch = sum(2 for o in os_ if o.shape[0] > 1)
    return pl.pallas_call(
        functools.partial(_attn_out_kernel, tm),
        grid=(T // tm,),
        in_specs=[tok] + [planes(a) for a in os_] + [planes(a) for a in lses]
        + [pl.BlockSpec((D, D), lambda i: (0, 0)), pl.BlockSpec((1, D), lambda i: (0, 0))],
        out_specs=tok,
        out_shape=jax.ShapeDtypeStruct((T, D), F32),
        scratch_shapes=[pltpu.VMEM((tm, D), BF16)] + [pltpu.VMEM((tm, LANES), F32)] * n_scratch,
        compiler_params=_params(("parallel",)),
    )(x, *os_, *lses, w_o, gpost)


def kernel(x_prompt, x_sample, norm_pre, norm_post, ffn_w_gate, ffn_w_up, ffn_w_down, rwkv_mu, rwkv_w_rkv, rwkv_w0, rwkv_w1, rwkv_w2, rwkv_a0, rwkv_a1, rwkv_a2, rwkv_v0, rwkv_v1, rwkv_v2, rwkv_g1, rwkv_g2, rwkv_k_k, rwkv_k_a, rwkv_r_k, rwkv_gn_w, rwkv_gn_b, rwkv_w_o, attn_w_qkv, attn_w_o):
    D = x_prompt.shape[-1]
    depth = norm_pre.shape[0]
    seqs = [x_prompt.shape[1]] * x_prompt.shape[0] + [x_sample.shape[1]] * x_sample.shape[0]
    seq_starts, seq_ends, t = [], [], 0
    for s in seqs:
        seq_starts.append(t)
        t += s
        seq_ends.append(t)
    seq_starts, seq_ends = tuple(seq_starts), tuple(seq_ends)
    x = jnp.concatenate([x_prompt.reshape(-1, D), x_sample.reshape(-1, D)], axis=0)
    pos = jnp.concatenate([jnp.arange(s, dtype=jnp.int32) for s in seqs])
    tables = _rope_tables(pos)

    def vec(a):
        return a.reshape(1, D)

    v_first = None
    for layer in range(depth):
        x = _ffn(x, vec(norm_pre[layer, 0]), ffn_w_gate[layer, 0].astype(BF16),
                 ffn_w_up[layer, 0].astype(BF16), ffn_w_down[layer, 0].astype(BF16),
                 vec(norm_post[layer, 0]))
        j = layer // 2
        if layer % 2 == 0:
            vres_w = None if j == 0 else (rwkv_v0[j - 1], rwkv_v1[j - 1], rwkv_v2[j - 1])
            pre = _rwkv_pre(x, seq_starts, seq_ends, vec(norm_pre[layer, 1]), rwkv_mu[j],
                            rwkv_w0[j], rwkv_w1[j], rwkv_w2[j], rwkv_a0[j], rwkv_a1[j], rwkv_a2[j],
                            rwkv_g1[j], rwkv_g2[j], vres_w)
            xr, xk, xv, ld, a, g = pre[:6]
            r = _mm(xr, rwkv_w_rkv[j, 0].astype(BF16))
            k = _mm(xk, rwkv_w_rkv[j, 1].astype(BF16))
            v = _mm(xv, rwkv_w_rkv[j, 2].astype(BF16))
            if j == 0:
                v_first, vres = v, None
            else:
                vres = (v_first, pre[6])
            y = _wkv(r, k, v, ld, a, vec(rwkv_k_k[j]), vec(rwkv_k_a[j]), vres, seq_starts, seq_ends)
            x = _rwkv_out(x, y, r, k, v, a, g, vec(rwkv_k_a[j]), vec(rwkv_r_k[j]),
                          vec(rwkv_gn_w[j]), vec(rwkv_gn_b[j]), rwkv_w_o[j].astype(BF16),
                          vec(norm_post[layer, 1]), vres)
        else:
            w_qkv = _rope_perm(attn_w_qkv[j].astype(BF16), D)
            os_, lses = [], []
            for gi, (_, dil) in enumerate(DILATED_GROUPS):
                qkv = _qkv(x, vec(norm_pre[layer, 1]), w_qkv, tables, gi, dil)
                o, lse = _attn_group(qkv, D, seq_starts, seq_ends)
                os_.append(o)
                lses.append(lse)
            x = _attn_out(x, os_, lses, attn_w_o[j].astype(BF16), vec(norm_post[layer, 1]))
        x = _ffn(x, vec(norm_pre[layer, 2]), ffn_w_gate[layer, 1].astype(BF16),
                 ffn_w_up[layer, 1].astype(BF16), ffn_w_down[layer, 1].astype(BF16),
                 vec(norm_post[layer, 2]))
    n_p = x_prompt.shape[0] * x_prompt.shape[1]
    return (x[:n_p].reshape(x_prompt.shape), x[n_p:].reshape(x_sample.shape))
```

```python
import functools
import math

import jax
import jax.numpy as jnp
from jax import lax
from jax.experimental import pallas as pl
from jax.experimental.pallas import tpu as pltpu

F32 = jnp.float32
BF16 = jnp.bfloat16

NORM_EPS = 1e-6
GN_EPS = 64e-5
RWKV_HEAD = 64
ATT_HEAD = 128
ROPE_DIM = ATT_HEAD // 4
ROPE_THETA = 500000.0
DILATED_GROUPS = ((128, 1), (512, 4), (2048, 16))
N_GROUPS = len(DILATED_GROUPS)
ATT_HALF = 64
LANES = 128
MXU_COLS = 256
WKV_CHUNK = 64
WKV_PAIRS_PER_STEP = 16
VMEM_LIMIT = 56 * 1024 * 1024

_NT = (((1,), (1,)), ((), ()))
_TN = (((0,), (0,)), ((), ()))


def _tile(n, pref, mult=8):
    if n <= pref:
        return n
    t = (pref // mult) * mult
    while t >= mult:
        if n % t == 0:
            return t
        t -= mult
    return n


def _in_list(v, lst):
    r = v == lst[0]
    for s in lst[1:]:
        r = r | (v == s)
    return r


def _rms(x, g):
    return x * lax.rsqrt(jnp.mean(x * x, axis=-1, keepdims=True) + NORM_EPS) * g


NORM_ROWS = 32


def _rms_rows(src_ref, g, emit):
    for r0 in range(0, src_ref.shape[0], NORM_ROWS):
        rows = slice(r0, min(r0 + NORM_ROWS, src_ref.shape[0]))
        emit(rows, _rms(src_ref[rows, :], g))


def _bdot(a, b):
    return jnp.dot(a.astype(BF16), b.astype(BF16), preferred_element_type=F32)


def _ntdot(a, b):
    return lax.dot_general(a, b, _NT, preferred_element_type=F32)


def _params(sem):
    return pltpu.CompilerParams(dimension_semantics=sem, vmem_limit_bytes=VMEM_LIMIT)


def _ffn_kernel(x_ref, gpre_ref, wg_ref, wu_ref, wd_ref, gpost_ref, o_ref, xn_ref, acc_ref):
    j = pl.program_id(1)

    @pl.when(j == 0)
    def _():
        def put(rows, y):
            xn_ref[rows, :] = y.astype(BF16)
        _rms_rows(x_ref, gpre_ref[...], put)
        acc_ref[...] = jnp.zeros_like(acc_ref)

    xn = xn_ref[...]
    g = jnp.dot(xn, wg_ref[...], preferred_element_type=F32)
    u = jnp.dot(xn, wu_ref[...], preferred_element_type=F32)
    h = (g * jax.nn.sigmoid(g)) * u
    acc_ref[...] += jnp.dot(h.astype(BF16), wd_ref[...], preferred_element_type=F32)

    @pl.when(j == pl.num_programs(1) - 1)
    def _():
        def put(rows, y):
            o_ref[rows, :] = x_ref[rows, :] + y
        _rms_rows(acc_ref, 0.5 * gpost_ref[...], put)


def _ffn(x, gpre, wg, wu, wd, gpost):
    T, D = x.shape
    F = wg.shape[1]
    tm = _tile(T, 768, MXU_COLS)
    tf = _tile(F, 512, LANES)
    return pl.pallas_call(
        _ffn_kernel,
        grid=(T // tm, F // tf),
        in_specs=[
            pl.BlockSpec((tm, D), lambda i, j: (i, 0)),
            pl.BlockSpec((1, D), lambda i, j: (0, 0)),
            pl.BlockSpec((D, tf), lambda i, j: (0, j)),
            pl.BlockSpec((D, tf), lambda i, j: (0, j)),
            pl.BlockSpec((tf, D), lambda i, j: (j, 0)),
            pl.BlockSpec((1, D), lambda i, j: (0, 0)),
        ],
        out_specs=pl.BlockSpec((tm, D), lambda i, j: (i, 0)),
        out_shape=jax.ShapeDtypeStruct((T, D), F32),
        scratch_shapes=[pltpu.VMEM((tm, D), BF16), pltpu.VMEM((tm, D), F32)],
        compiler_params=_params(("parallel", "arbitrary")),
    )(x, gpre, wg, wu, wd, gpost)


def _mm_kernel(a_ref, w_ref, o_ref):
    a = a_ref[...]
    for c in range(0, o_ref.shape[1], MXU_COLS):
        o_ref[:, c:c + MXU_COLS] = jnp.dot(
            a, w_ref[:, c:c + MXU_COLS], preferred_element_type=F32).astype(o_ref.dtype)


def _mm(a, w):
    M, K = a.shape
    N = w.shape[1]
    tm = _tile(M, 1024)
    tn = _tile(N, 1024, MXU_COLS)
    return pl.pallas_call(
        _mm_kernel,
        grid=(M // tm, N // tn),
        in_specs=[
            pl.BlockSpec((tm, K), lambda i, j: (i, 0)),
            pl.BlockSpec((K, tn), lambda i, j: (0, j)),
        ],
        out_specs=pl.BlockSpec((tm, tn), lambda i, j: (i, j)),
        out_shape=jax.ShapeDtypeStruct((M, N), BF16),
        compiler_params=_params(("parallel", "arbitrary")),
    )(a, w)


def _rwkv_pre_kernel(seq_starts, seq_ends, has_vres, tm, *refs):
    (x_ref, xp_ref, xn_ref, gpre_ref, mu_ref, w0_ref, w1_ref, w2_ref,
     a0_ref, a1_ref, a2_ref, g1_ref, g2_ref) = refs[:13]
    refs = refs[13:]
    if has_vres:
        v0_ref, v1_ref, v2_ref = refs[:3]
        refs = refs[3:]
    xr_ref, xk_ref, xv_ref, ld_ref, a_ref, g_ref = refs[:6]

    t0 = pl.program_id(0) * tm
    gpre = gpre_ref[...]
    h = _rms(x_ref[...], gpre)
    keep_p = 1.0 - _in_list(t0, seq_starts).astype(F32)
    keep_n = 1.0 - _in_list(t0 + tm, seq_ends).astype(F32)
    hp = _rms(xp_ref[7:8, :], gpre) * keep_p
    hn = _rms(xn_ref[0:1, :], gpre) * keep_n
    row = lax.broadcasted_iota(jnp.int32, h.shape, 0)
    prev = jnp.where(row == 0, hp, pltpu.roll(h, 1, 0))
    nxt = jnp.where(row == tm - 1, hn, pltpu.roll(h, tm - 1, 0))
    xx = 0.5 * (prev + nxt) - h

    def mix(i):
        return (h + xx * mu_ref[i:i + 1, :]).astype(BF16)

    xr_ref[...] = mix(0)
    xw = mix(1)
    xk_ref[...] = mix(2)
    xv = mix(3)
    xv_ref[...] = xv
    xa = mix(4)
    xg = mix(5)
    for d in range(2):
        tw = jnp.tanh(jnp.dot(xw, w1_ref[d], preferred_element_type=F32))
        z = w0_ref[d:d + 1, :] + _bdot(tw, w2_ref[d])
        ld_ref[d] = (-math.exp(-0.5) * jax.nn.sigmoid(z)).astype(BF16)
        ta = jnp.dot(xa, a1_ref[d], preferred_element_type=F32)
        a_ref[d] = jax.nn.sigmoid(a0_ref[d:d + 1, :] + _bdot(ta, a2_ref[d])).astype(BF16)
    tg = jax.nn.sigmoid(jnp.dot(xg, g1_ref[...], preferred_element_type=F32))
    g_ref[...] = _bdot(tg, g2_ref[...]).astype(BF16)
    if has_vres:
        vg_ref = refs[6]
        tv = jnp.dot(xv, v1_ref[...], preferred_element_type=F32)
        vg_ref[...] = jax.nn.sigmoid(v0_ref[...] + _bdot(tv, v2_ref[...])).astype(BF16)


def _pad_lora(w_in, w_out):
    r = w_in.shape[-1]
    rp = -(-r // LANES) * LANES
    pin = [(0, 0)] * (w_in.ndim - 1) + [(0, rp - r)]
    pout = [(0, 0)] * (w_out.ndim - 2) + [(0, rp - r), (0, 0)]
    return jnp.pad(w_in, pin).astype(BF16), jnp.pad(w_out, pout).astype(BF16)


def _rwkv_pre(x, seq_starts, seq_ends, gpre, mu, w0, w1, w2, a0, a1, a2, g1, g2, vres):
    T, D = x.shape
    tm = _tile(T, 128)
    nb8 = T // 8
    has_vres = vres is not None
    w1p, w2p = _pad_lora(w1, w2)
    a1p, a2p = _pad_lora(a1, a2)
    g1p, g2p = _pad_lora(g1, g2)

    def full(a):
        nd = a.ndim
        return pl.BlockSpec(a.shape, lambda i, _nd=nd: (0,) * _nd)

    args = [x, x, x, gpre, mu, w0, w1p, w2p, a0, a1p, a2p, g1p, g2p]
    in_specs = [
        pl.BlockSpec((tm, D), lambda i: (i, 0)),
        pl.BlockSpec((8, D), lambda i: (jnp.maximum(i * (tm // 8) - 1, 0), 0)),
        pl.BlockSpec((8, D), lambda i: (jnp.minimum((i + 1) * (tm // 8), nb8 - 1), 0)),
    ] + [full(a) for a in args[3:]]
    if has_vres:
        v0, v1, v2 = vres
        v1p, v2p = _pad_lora(v1, v2)
        extra = [v0.reshape(1, D), v1p, v2p]
        args += extra
        in_specs += [full(a) for a in extra]
    tok = pl.BlockSpec((tm, D), lambda i: (i, 0))
    tok2 = pl.BlockSpec((2, tm, D), lambda i: (0, i, 0))
    out_specs = [tok, tok, tok, tok2, tok2, tok]
    out_shape = [jax.ShapeDtypeStruct((T, D), BF16)] * 3 + [
        jax.ShapeDtypeStruct((2, T, D), BF16), jax.ShapeDtypeStruct((2, T, D), BF16),
        jax.ShapeDtypeStruct((T, D), BF16)]
    if has_vres:
        out_specs.append(tok)
        out_shape.append(jax.ShapeDtypeStruct((T, D), BF16))
    return pl.pallas_call(
        functools.partial(_rwkv_pre_kernel, seq_starts, seq_ends, has_vres, tm),
        grid=(T // tm,),
        in_specs=in_specs,
        out_specs=out_specs,
        out_shape=out_shape,
        compiler_params=_params(("parallel",)),
    )(*args)


def _wkv_kernel(L, G, nc, resets_f, resets_r, has_vres, *refs):
    r_ref, k_ref, v_ref, ld_ref, a_ref, kk_ref, ka_ref = refs[:7]
    refs = refs[7:]
    if has_vres:
        vf_ref, vg_ref = refs[:2]
        refs = refs[2:]
    y_ref, s_ref = refs

    d = pl.program_id(0)
    c = pl.program_id(2)
    ci = c + d * (nc - 1 - 2 * c)
    reset = ((d == 0) & _in_list(ci, resets_f)) | ((d == 1) & _in_list(ci, resets_r))

    @pl.when(reset)
    def _():
        s_ref[...] = jnp.zeros_like(s_ref)

    sgn = 1 - 2 * d
    tt = lax.broadcasted_iota(jnp.int32, (L, LANES), 0)
    ss = lax.broadcasted_iota(jnp.int32, (L, LANES), 1) % L
    delta = sgn * (tt - ss)
    incl = delta >= 0
    strict = delta > 0
    eye = (tt == ss).astype(F32)
    tri = (sgn * (lax.broadcasted_iota(jnp.int32, (L, L), 0)
                  - lax.broadcasted_iota(jnp.int32, (L, L), 1)) >= 0).astype(BF16)
    first_head = lax.broadcasted_iota(jnp.int32, (L, LANES), 1) < RWKV_HEAD
    bi = lax.broadcasted_iota(jnp.int32, (LANES, LANES), 0) // RWKV_HEAD
    bj = lax.broadcasted_iota(jnp.int32, (LANES, LANES), 1) // RWKV_HEAD
    same_head = bi == bj
    head_ones = same_head.astype(BF16)
    n_double = int(math.log2(L)) - 1

    def bd(x):
        xb = x.astype(BF16)
        zero = jnp.zeros_like(xb)
        return jnp.concatenate([jnp.where(first_head, xb, zero), jnp.where(first_head, zero, xb)], axis=0)

    pairs = []
    for p in range(G):
        sl = slice(p * LANES, (p + 1) * LANES)
        r_ = r_ref[:, sl].astype(F32)
        k_ = k_ref[:, sl].astype(F32)
        v_ = v_ref[:, sl].astype(F32)
        ld_b = ld_ref[:, sl]
        ld_ = ld_b.astype(F32)
        a_ = a_ref[:, sl].astype(F32)
        if has_vres:
            v_ = v_ + (vf_ref[:, sl].astype(F32) - v_) * vg_ref[:, sl].astype(F32)
        kkc = k_ * kk_ref[:, sl]
        n2 = jnp.dot((kkc * kkc).astype(BF16), head_ones, preferred_element_type=F32)
        kkn = kkc / jnp.maximum(jnp.sqrt(n2), 1e-12)
        kd = k_ * (1.0 + (a_ - 1.0) * ka_ref[:, sl])
        cs = jnp.dot(tri, ld_b, preferred_element_type=F32)
        w_inv = jnp.exp(-cs)
        bt = kkn * a_ * w_inv
        kt = kd * w_inv
        pairs.append(dict(
            sl=sl, v=v_, bt=bt, kt=kt, st=s_ref[p],
            lhs=jnp.concatenate([-kkn * jnp.exp(cs - ld_), r_ * jnp.exp(cs)], axis=0).astype(BF16),
            wend=jnp.exp(jnp.sum(ld_, axis=0, keepdims=True))))

    for pr in pairs:
        rhs_t = jnp.concatenate([bd(pr["bt"]), bd(pr["kt"]), pr["st"].astype(BF16)], axis=0)
        prod = _ntdot(pr["lhs"], rhs_t)
        pr["a_ab"] = jnp.where(strict, prod[:L, :LANES], 0.0)
        pr["a_rb"] = jnp.where(incl, prod[L:, :LANES], 0.0)
        a_ak = jnp.where(strict, prod[:L, LANES:2 * LANES], 0.0)
        a_rk = jnp.where(incl, prod[L:, LANES:2 * LANES], 0.0)
        pr["sp"] = prod[:, 2 * LANES:]
        pr["a_k"] = jnp.concatenate([a_ak, a_rk], axis=0)
    for pr in pairs:
        pr["spv"] = pr["sp"] + _bdot(pr["a_k"], bd(pr["v"]))
    for pr in pairs:
        x = _bdot(jnp.concatenate([pr["a_ab"], pr["a_rb"]], axis=0), bd(pr["a_ab"]))
        pr["tinv"] = eye + pr["a_ab"]
        pr["pw"] = x[:L]
        pr["rt"] = pr["a_rb"] + x[L:]
    for lvl in range(1, n_double + 1):
        last = lvl == n_double
        for pr in pairs:
            rows = [pr["tinv"], pr["rt"]] if last else [pr["tinv"], pr["rt"], pr["pw"]]
            x = _bdot(jnp.concatenate(rows, axis=0), bd(pr["pw"]))
            pr["tinv"] = pr["tinv"] + x[:L]
            pr["rt"] = pr["rt"] + x[L:2 * L]
            if not last:
                pr["pw"] = x[2 * L:]
    for pr in pairs:
        z = _bdot(jnp.concatenate([pr["tinv"], pr["rt"]], axis=0), bd(pr["spv"][:L]))
        pr["u"] = z[:L]
        pr["y"] = pr["spv"][L:] + z[L:]
    for p, pr in enumerate(pairs):
        y_ref[:, pr["sl"]] = pr["y"].astype(BF16)
        uv = jnp.concatenate([pr["u"], pr["v"]], axis=0).astype(BF16)
        bk = jnp.concatenate([pr["bt"], pr["kt"]], axis=0).astype(BF16)
        upd = lax.dot_general(uv, bk, _TN, preferred_element_type=F32)
        s_ref[p] = (pr["st"] + jnp.where(same_head, upd, 0.0)) * pr["wend"]


def _wkv(r, k, v, ld, a, k_k, k_a, vres, seq_starts, seq_ends):
    T, D = r.shape
    L = WKV_CHUNK
    npairs = D // LANES
    assert 2 * L == LANES and RWKV_HEAD == L
    G = math.gcd(npairs, WKV_PAIRS_PER_STEP)
    nc = T // L
    W = G * LANES
    resets_f = tuple(s // L for s in seq_starts)
    resets_r = tuple(e // L - 1 for e in seq_ends)
    has_vres = vres is not None

    def cidx(d, c):
        return c + d * (nc - 1 - 2 * c)

    tok = pl.BlockSpec((L, W), lambda d, p, c: (cidx(d, c), p))
    tokd = pl.BlockSpec((None, L, W), lambda d, p, c: (d, cidx(d, c), p))
    par = pl.BlockSpec((1, W), lambda d, p, c: (0, p))
    args = [r, k, v, ld, a, k_k, k_a]
    in_specs = [tok, tok, tok, tokd, tokd, par, par]
    if has_vres:
        args += list(vres)
        in_specs += [tok, tok]
    return pl.pallas_call(
        functools.partial(_wkv_kernel, L, G, nc, resets_f, resets_r, has_vres),
        grid=(2, npairs // G, nc),
        in_specs=in_specs,
        out_specs=tokd,
        out_shape=jax.ShapeDtypeStruct((2, T, D), BF16),
        scratch_shapes=[pltpu.VMEM((G, LANES, LANES), F32)],
        compiler_params=_params(("arbitrary", "arbitrary", "arbitrary")),
    )(*args)


def _rwkv_out_kernel(has_vres, *refs):
    (x_ref, y_ref, r_ref, k_ref, v_ref, a_ref, g_ref, ka_ref, rk_ref, gnw_ref, gnb_ref,
     wo_ref, gpost_ref) = refs[:13]
    refs = refs[13:]
    if has_vres:
        vf_ref, vg_ref = refs[:2]
        refs = refs[2:]
    o_ref, act_ref = refs
    D = x_ref.shape[1]

    bi = lax.broadcasted_iota(jnp.int32, (LANES, LANES), 0) // RWKV_HEAD
    bj = lax.broadcasted_iota(jnp.int32, (LANES, LANES), 1) // RWKV_HEAD
    head_ones = (bi == bj).astype(BF16)
    inv_n = 1.0 / RWKV_HEAD

    slabs = [slice(s * LANES, (s + 1) * LANES) for s in range(D // LANES)]
    ys = [y_ref[0, :, sl].astype(F32) + y_ref[1, :, sl].astype(F32) for sl in slabs]
    means = [_bdot(y, head_ones) * inv_n for y in ys]
    dys = [y - mu for y, mu in zip(ys, means)]
    vars_ = [_bdot(dy * dy, head_ones) * inv_n for dy in dys]
    rks = []
    for sl in slabs:
        a_mean = 0.5 * (a_ref[0, :, sl].astype(F32) + a_ref[1, :, sl].astype(F32))
        k_mean = k_ref[:, sl].astype(F32) * (1.0 + (a_mean - 1.0) * ka_ref[:, sl])
        rks.append(_bdot(r_ref[:, sl].astype(F32) * k_mean * rk_ref[:, sl], head_ones))
    for sl, dy, var, rk in zip(slabs, dys, vars_, rks):
        yn = dy * lax.rsqrt(var + GN_EPS) * gnw_ref[:, sl] + gnb_ref[:, sl]
        v_ = v_ref[:, sl].astype(F32)
        if has_vres:
            v_ = v_ + (vf_ref[:, sl].astype(F32) - v_) * vg_ref[:, sl].astype(F32)
        act_ref[:, sl] = ((yn + rk * v_) * g_ref[:, sl].astype(F32)).astype(BF16)
    h = jnp.dot(act_ref[...], wo_ref[...], preferred_element_type=F32)
    o_ref[...] = x_ref[...] + _rms(h, gpost_ref[...])


def _rwkv_out(x, y, r, k, v, a, g, k_a, r_k, gn_w, gn_b, w_o, gpost, vres):
    T, D = x.shape
    tm = _tile(T, 256, 16)
    has_vres = vres is not None
    tok = pl.BlockSpec((tm, D), lambda i: (i, 0))
    tok2 = pl.BlockSpec((2, tm, D), lambda i: (0, i, 0))
    par = pl.BlockSpec((1, D), lambda i: (0, 0))
    args = [x, y, r, k, v, a, g, k_a, r_k, gn_w, gn_b, w_o, gpost]
    in_specs = [tok, tok2, tok, tok, tok, tok2, tok, par, par, par, par,
                pl.BlockSpec((D, D), lambda i: (0, 0)), par]
    if has_vres:
        args += list(vres)
        in_specs += [tok, tok]
    return pl.pallas_call(
        functools.partial(_rwkv_out_kernel, has_vres),
        grid=(T // tm,),
        in_specs=in_specs,
        out_specs=tok,
        out_shape=jax.ShapeDtypeStruct((T, D), F32),
        scratch_shapes=[pltpu.VMEM((tm, D), BF16)],
        compiler_params=_params(("parallel",)),
    )(*args)


def _qkv_kernel(tm, tn, deint, x_ref, gpre_ref, w_ref, cos_ref, sin_ref, o_ref, xn_ref, *acc):
    j = pl.program_id(1)

    @pl.when(j == 0)
    def _():
        def put(rows, y):
            xn_ref[rows, :] = y.astype(BF16)
        _rms_rows(x_ref, gpre_ref[...], put)

    cos = cos_ref[...]
    sin = sin_ref[...]
    xn = xn_ref[...]
    per_dot = MXU_COLS // ATT_HEAD
    for s in range(tn // ATT_HEAD):
        sl = slice(s * ATT_HEAD, (s + 1) * ATT_HEAD)
        if s % per_dot == 0:
            prod = jnp.dot(xn, w_ref[:, s * ATT_HEAD:s * ATT_HEAD + MXU_COLS],
                           preferred_element_type=F32)
        t = prod[:, (s % per_dot) * ATT_HEAD:(s % per_dot + 1) * ATT_HEAD]
        rot = t * cos + pltpu.roll(t, ATT_HEAD // 2, 1) * sin
        if deint == 1:
            o_ref[:, sl] = rot.astype(BF16)
        else:
            acc_ref, = acc
            acc_ref[s] = rot
            for c in range(deint):
                o_ref[c, :, sl] = acc_ref[s, pl.ds(c, tm // deint, stride=deint), :].astype(BF16)


def _rope_tables(pos):
    half = ROPE_DIM // 2
    inv = ROPE_THETA ** (-jnp.arange(half, dtype=F32) / half)
    ang = pos.astype(F32)[:, None] * inv[None, :]
    cos, sin = jnp.cos(ang), jnp.sin(ang)
    T = pos.shape[0]
    gap = ATT_HEAD // 2 - half
    ones = jnp.ones((T, gap), F32)
    zeros = jnp.zeros((T, gap), F32)
    cos_t = jnp.concatenate([cos, ones, cos, ones], axis=1)
    sin_t = jnp.concatenate([-sin, zeros, sin, zeros], axis=1)
    return (jnp.stack([cos_t, jnp.ones_like(cos_t)]), jnp.stack([sin_t, jnp.zeros_like(sin_t)]))


def _rope_perm(w_qkv, D):
    half = ROPE_DIM // 2
    mid = ATT_HEAD // 2
    idx = list(range(ATT_HEAD))
    idx[half:2 * half], idx[mid:mid + half] = idx[mid:mid + half], idx[half:2 * half]
    w = w_qkv.reshape(w_qkv.shape[0], N_GROUPS, 3, D // ATT_HEAD, ATT_HEAD)
    qk = jnp.take(w[:, :, :2], jnp.asarray(idx, jnp.int32), axis=-1)
    return jnp.concatenate([qk, w[:, :, 2:]], axis=2).reshape(w_qkv.shape)


def _qkv(x, gpre, w, tables, gi, dil):
    T, D = x.shape
    tn = _tile(D, 2048, MXU_COLS)
    nj = 3 * D // tn

    def plane(j):
        return (j * tn) // (2 * D)

    tm = _tile(T, 512, 16 * dil)
    tab = pl.BlockSpec((None, tm, ATT_HEAD), lambda i, j: (plane(j), i, 0))
    if dil == 1:
        out = pl.BlockSpec((None, tm, tn), lambda i, j: (0, i, j))
        deint_scratch = []
    else:
        out = pl.BlockSpec((dil, tm // dil, tn), lambda i, j: (0, i, j))
        deint_scratch = [pltpu.VMEM((tn // ATT_HEAD, tm, ATT_HEAD), F32)]
    return pl.pallas_call(
        functools.partial(_qkv_kernel, tm, tn, dil),
        grid=(T // tm, nj),
        in_specs=[
            pl.BlockSpec((tm, D), lambda i, j: (i, 0)),
            pl.BlockSpec((1, D), lambda i, j: (0, 0)),
            pl.BlockSpec((D, tn), lambda i, j: (0, gi * nj + j)),
            tab, tab,
        ],
        out_specs=out,
        out_shape=jax.ShapeDtypeStruct((dil, T // dil, 3 * D), BF16),
        scratch_shapes=[pltpu.VMEM((tm, D), BF16)] + deint_scratch,
        compiler_params=_params(("parallel", "arbitrary")),
    )(x, gpre, w, *tables)


def _attn_kernel(bq, n_heads, bounds, q_ref, kp_ref, kc_ref, kn_ref, vp_ref, vc_ref, vn_ref,
                 o_ref, lse_ref):
    row0 = pl.program_id(0) * bq
    lo = jnp.int32(bounds[0])
    hi = jnp.int32(bounds[-1])
    for b in bounds[1:-1]:
        lo = jnp.where(row0 >= b, b, lo)
    for b in reversed(bounds[1:-1]):
        hi = jnp.where(row0 < b, b, hi)
    nk = bq + 2 * ATT_HALF
    qrow = row0 + lax.broadcasted_iota(jnp.int32, (bq, nk), 0)
    krow = row0 - ATT_HALF + lax.broadcasted_iota(jnp.int32, (bq, nk), 1)
    valid = (jnp.abs(krow - qrow) <= ATT_HALF) & (krow >= lo) & (krow < hi)
    scale = ATT_HEAD ** -0.5
    lane = lax.broadcasted_iota(jnp.int32, (bq, LANES), 1)
    lse_all = jnp.zeros((bq, LANES), F32)
    for h in range(n_heads):
        sl = slice(h * ATT_HEAD, (h + 1) * ATT_HEAD)
        q = q_ref[:, sl]
        kcat = jnp.concatenate([kp_ref[:, sl], kc_ref[:, sl], kn_ref[:, sl]], axis=0)
        vcat = jnp.concatenate([vp_ref[:, sl], vc_ref[:, sl], vn_ref[:, sl]], axis=0)
        s = lax.dot_general(q, kcat, _NT, preferred_element_type=F32) * scale
        s = jnp.where(valid, s, -1e30)
        m = jnp.max(s, axis=-1, keepdims=True)
        p = jnp.exp(s - m)
        l = jnp.sum(p, axis=-1, keepdims=True)
        o = jnp.dot(p.astype(BF16), vcat, preferred_element_type=F32) / l
        o_ref[:, sl] = o.astype(BF16)
        lse_all = jnp.where(lane == h, m + jnp.log(l), lse_all)
    lse_ref[...] = lse_all


def _attn_group(qkv, D, seq_starts, seq_ends):
    dil, rows, _ = qkv.shape
    bounds = tuple(s // dil for s in seq_starts) + (seq_ends[-1] // dil,)
    seg = [b - a for a, b in zip(bounds[:-1], bounds[1:])]
    bq = 128
    for s in seg:
        bq = math.gcd(bq, s)
    assert bq >= ATT_HALF, "sequence too short for the dilated band blocks"
    nb = rows // bq

    per = bq // ATT_HALF
    nh = rows // ATT_HALF

    def cur(kind):
        return pl.BlockSpec((None, bq, D), lambda i, c: (c, i, kind))

    def halo(kind, after):
        def imap(i, c):
            blk = (i + 1) * per if after else i * per - 1
            return (c, jnp.clip(blk, 0, nh - 1), kind)
        return pl.BlockSpec((None, ATT_HALF, D), imap)

    return pl.pallas_call(
        functools.partial(_attn_kernel, bq, D // ATT_HEAD, bounds),
        grid=(nb, dil),
        in_specs=[cur(0), halo(1, False), cur(1), halo(1, True),
                  halo(2, False), cur(2), halo(2, True)],
        out_specs=[pl.BlockSpec((None, bq, D), lambda i, c: (c, i, 0)),
                   pl.BlockSpec((None, bq, LANES), lambda i, c: (c, i, 0))],
        out_shape=[jax.ShapeDtypeStruct((dil, rows, D), BF16),
                   jax.ShapeDtypeStruct((dil, rows, LANES), F32)],
        compiler_params=_params(("parallel", "arbitrary")),
    )(qkv, qkv, qkv, qkv, qkv, qkv, qkv)


def _attn_out_kernel(tm, x_ref, o0_ref, o1_ref, o2_ref, l0_ref, l1_ref, l2_ref, wo_ref, gpost_ref,
                     out_ref, act_ref, *scratch):
    D = x_ref.shape[1]

    def token_major(ref, sl, s):
        dil = ref.shape[0]
        if dil == 1:
            return ref[0, :, sl].astype(F32)
        for c in range(dil):
            s[pl.ds(c, tm // dil, stride=dil), :] = ref[c, :, sl].astype(F32)
        return s[...]

    free = list(scratch)
    l0, l1, l2 = [token_major(r, slice(0, LANES), free.pop() if r.shape[0] > 1 else None)
                  for r in (l0_ref, l1_ref, l2_ref)]
    m = jnp.maximum(jnp.maximum(l0, l1), l2)
    e0, e1, e2 = jnp.exp(l0 - m), jnp.exp(l1 - m), jnp.exp(l2 - m)
    inv = 1.0 / (e0 + e1 + e2)
    w0, w1, w2 = e0 * inv, e1 * inv, e2 * inv
    h = None
    heads_per_dot = MXU_COLS // ATT_HEAD
    for k in range(D // ATT_HEAD):
        sl = slice(k * ATT_HEAD, (k + 1) * ATT_HEAD)
        pool = list(free)
        o0, o1, o2 = [token_major(r, sl, pool.pop() if r.shape[0] > 1 else None)
                      for r in (o0_ref, o1_ref, o2_ref)]
        act_ref[:, sl] = (w0[:, k:k + 1] * o0 + w1[:, k:k + 1] * o1
                          + w2[:, k:k + 1] * o2).astype(BF16)
        if (k + 1) % heads_per_dot == 0:
            ks = slice((k + 1 - heads_per_dot) * ATT_HEAD, (k + 1) * ATT_HEAD)
            part = jnp.dot(act_ref[:, ks], wo_ref[ks, :], preferred_element_type=F32)
            h = part if h is None else h + part
    out_ref[...] = x_ref[...] + _rms(h, gpost_ref[...])


def _attn_out(x, os_, lses, w_o, gpost):
    T, D = x.shape
    tm = _tile(T, 256, 16 * max(o.shape[0] for o in os_))
    tok = pl.BlockSpec((tm, D), lambda i: (i, 0))

    def planes(a):
        dil = a.shape[0]
        return pl.BlockSpec((dil, tm // dil, a.shape[2]), lambda i: (0, i, 0))

    n_scratch = sum(2 for o in os_ if o.shape[0] > 1)
    return pl.pallas_call(
        functools.partial(_attn_out_kernel, tm),
        grid=(T // tm,),
        in_specs=[tok] + [planes(a) for a in os_] + [planes(a) for a in lses]
        + [pl.BlockSpec((D, D), lambda i: (0, 0)), pl.BlockSpec((1, D), lambda i: (0, 0))],
        out_specs=tok,
        out_shape=jax.ShapeDtypeStruct((T, D), F32),
        scratch_shapes=[pltpu.VMEM((tm, D), BF16)] + [pltpu.VMEM((tm, LANES), F32)] * n_scratch,
        compiler_params=_params(("parallel",)),
    )(x, *os_, *lses, w_o, gpost)


def kernel(x_prompt, x_sample, norm_pre, norm_post, ffn_w_gate, ffn_w_up, ffn_w_down, rwkv_mu, rwkv_w_rkv, rwkv_w0, rwkv_w1, rwkv_w2, rwkv_a0, rwkv_a1, rwkv_a2, rwkv_v0, rwkv_v1, rwkv_v2, rwkv_g1, rwkv_g2, rwkv_k_k, rwkv_k_a, rwkv_r_k, rwkv_gn_w, rwkv_gn_b, rwkv_w_o, attn_w_qkv, attn_w_o):
    D = x_prompt.shape[-1]
    depth = norm_pre.shape[0]
    seqs = [x_prompt.shape[1]] * x_prompt.shape[0] + [x_sample.shape[1]] * x_sample.shape[0]
    seq_starts, seq_ends, t = [], [], 0
    for s in seqs:
        seq_starts.append(t)
        t += s
        seq_ends.append(t)
    seq_starts, seq_ends = tuple(seq_starts), tuple(seq_ends)
    x = jnp.concatenate([x_prompt.reshape(-1, D), x_sample.reshape(-1, D)], axis=0)
    pos = jnp.concatenate([jnp.arange(s, dtype=jnp.int32) for s in seqs])
    tables = _rope_tables(pos)

    def vec(a):
        return a.reshape(1, D)

    v_first = None
    for layer in range(depth):
        x = _ffn(x, vec(norm_pre[layer, 0]), ffn_w_gate[layer, 0].astype(BF16),
                 ffn_w_up[layer, 0].astype(BF16), ffn_w_down[layer, 0].astype(BF16),
                 vec(norm_post[layer, 0]))
        j = layer // 2
        if layer % 2 == 0:
            vres_w = None if j == 0 else (rwkv_v0[j - 1], rwkv_v1[j - 1], rwkv_v2[j - 1])
            pre = _rwkv_pre(x, seq_starts, seq_ends, vec(norm_pre[layer, 1]), rwkv_mu[j],
                            rwkv_w0[j], rwkv_w1[j], rwkv_w2[j], rwkv_a0[j], rwkv_a1[j], rwkv_a2[j],
                            rwkv_g1[j], rwkv_g2[j], vres_w)
            xr, xk, xv, ld, a, g = pre[:6]
            r = _mm(xr, rwkv_w_rkv[j, 0].astype(BF16))
            k = _mm(xk, rwkv_w_rkv[j, 1].astype(BF16))
            v = _mm(xv, rwkv_w_rkv[j, 2].astype(BF16))
            if j == 0:
                v_first, vres = v, None
            else:
                vres = (v_first, pre[6])
            y = _wkv(r, k, v, ld, a, vec(rwkv_k_k[j]), vec(rwkv_k_a[j]), vres, seq_starts, seq_ends)
            x = _rwkv_out(x, y, r, k, v, a, g, vec(rwkv_k_a[j]), vec(rwkv_r_k[j]),
                          vec(rwkv_gn_w[j]), vec(rwkv_gn_b[j]), rwkv_w_o[j].astype(BF16),
                          vec(norm_post[layer, 1]), vres)
        else:
            w_qkv = _rope_perm(attn_w_qkv[j].astype(BF16), D)
            os_, lses = [], []
            for gi, (_, dil) in enumerate(DILATED_GROUPS):
                qkv = _qkv(x, vec(norm_pre[layer, 1]), w_qkv, tables, gi, dil)
                o, lse = _attn_group(qkv, D, seq_starts, seq_ends)
                os_.append(o)
                lses.append(lse)
            x = _attn_out(x, os_, lses, attn_w_o[j].astype(BF16), vec(norm_post[layer, 1]))
        x = _ffn(x, vec(norm_pre[layer, 2]), ffn_w_gate[layer, 1].astype(BF16),
                 ffn_w_up[layer, 1].astype(BF16), ffn_w_down[layer, 1].astype(BF16),
                 vec(norm_post[layer, 2]))
    n_p = x_prompt.shape[0] * x_prompt.shape[1]
    return (x[:n_p].reshape(x_prompt.shape), x[n_p:].reshape(x_sample.shape))
```

```python
import functools
import math

import jax
import jax.numpy as jnp
from jax import lax
from jax.experimental import pallas as pl
from jax.experimental.pallas import tpu as pltpu

F32 = jnp.float32
BF16 = jnp.bfloat16

NORM_EPS = 1e-6
GN_EPS = 64e-5
RWKV_HEAD = 64
ATT_HEAD = 128
ROPE_DIM = ATT_HEAD // 4
ROPE_THETA = 500000.0
DILATED_GROUPS = ((128, 1), (512, 4), (2048, 16))
N_GROUPS = len(DILATED_GROUPS)
ATT_HALF = 64
LANES = 128
MXU_COLS = 256
LOG2E = math.log2(math.e)
WKV_CHUNK = 64
WKV_PAIRS_PER_STEP = 16
VMEM_LIMIT = 56 * 1024 * 1024

_NT = (((1,), (1,)), ((), ()))
_TN = (((0,), (0,)), ((), ()))


def _tile(n, pref, mult=8):
    if n <= pref:
        return n
    t = (pref // mult) * mult
    while t >= mult:
        if n % t == 0:
            return t
        t -= mult
    return n


def _in_list(v, lst):
    r = v == lst[0]
    for s in lst[1:]:
        r = r | (v == s)
    return r


def _rms(x, g):
    return x * lax.rsqrt(jnp.mean(x * x, axis=-1, keepdims=True) + NORM_EPS) * g


NORM_ROWS = 32


def _rms_rows(src_ref, g, emit):
    for r0 in range(0, src_ref.shape[0], NORM_ROWS):
        rows = slice(r0, min(r0 + NORM_ROWS, src_ref.shape[0]))
        emit(rows, _rms(src_ref[rows, :], g))


def _bdot(a, b):
    return jnp.dot(a.astype(BF16), b.astype(BF16), preferred_element_type=F32)


def _ntdot(a, b):
    return lax.dot_general(a, b, _NT, preferred_element_type=F32)


def _params(sem):
    return pltpu.CompilerParams(dimension_semantics=sem, vmem_limit_bytes=VMEM_LIMIT)


def _ffn_kernel(x_ref, gpre_ref, wg_ref, wu_ref, wd_ref, gpost_ref, o_ref, xn_ref, acc_ref):
    j = pl.program_id(1)

    @pl.when(j == 0)
    def _():
        def put(rows, y):
            xn_ref[rows, :] = y.astype(BF16)
        _rms_rows(x_ref, gpre_ref[...], put)
        acc_ref[...] = jnp.zeros_like(acc_ref)

    xn = xn_ref[...]
    g = jnp.dot(xn, wg_ref[...], preferred_element_type=F32)
    u = jnp.dot(xn, wu_ref[...], preferred_element_type=F32)
    h = (g * jax.nn.sigmoid(g)) * u
    acc_ref[...] += jnp.dot(h.astype(BF16), wd_ref[...], preferred_element_type=F32)

    @pl.when(j == pl.num_programs(1) - 1)
    def _():
        def put(rows, y):
            o_ref[rows, :] = x_ref[rows, :] + y
        _rms_rows(acc_ref, 0.5 * gpost_ref[...], put)


def _ffn(x, gpre, wg, wu, wd, gpost):
    T, D = x.shape
    F = wg.shape[1]
    tm = _tile(T, 768, MXU_COLS)
    tf = _tile(F, 512, LANES)
    return pl.pallas_call(
        _ffn_kernel,
        grid=(T // tm, F // tf),
        in_specs=[
            pl.BlockSpec((tm, D), lambda i, j: (i, 0)),
            pl.BlockSpec((1, D), lambda i, j: (0, 0)),
            pl.BlockSpec((D, tf), lambda i, j: (0, j)),
            pl.BlockSpec((D, tf), lambda i, j: (0, j)),
            pl.BlockSpec((tf, D), lambda i, j: (j, 0)),
            pl.BlockSpec((1, D), lambda i, j: (0, 0)),
        ],
        out_specs=pl.BlockSpec((tm, D), lambda i, j: (i, 0)),
        out_shape=jax.ShapeDtypeStruct((T, D), F32),
        scratch_shapes=[pltpu.VMEM((tm, D), BF16), pltpu.VMEM((tm, D), F32)],
        compiler_params=_params(("parallel", "arbitrary")),
    )(x, gpre, wg, wu, wd, gpost)


def _mm_kernel(a_ref, w_ref, o_ref):
    a = a_ref[...]
    for c in range(0, o_ref.shape[1], MXU_COLS):
        o_ref[:, c:c + MXU_COLS] = jnp.dot(
            a, w_ref[:, c:c + MXU_COLS], preferred_element_type=F32).astype(o_ref.dtype)


def _mm(a, w):
    M, K = a.shape
    N = w.shape[1]
    tm = _tile(M, 1024)
    tn = _tile(N, 1024, MXU_COLS)
    return pl.pallas_call(
        _mm_kernel,
        grid=(M // tm, N // tn),
        in_specs=[
            pl.BlockSpec((tm, K), lambda i, j: (i, 0)),
            pl.BlockSpec((K, tn), lambda i, j: (0, j)),
        ],
        out_specs=pl.BlockSpec((tm, tn), lambda i, j: (i, j)),
        out_shape=jax.ShapeDtypeStruct((M, N), BF16),
        compiler_params=_params(("parallel", "arbitrary")),
    )(a, w)


def _rwkv_pre_kernel(seq_starts, seq_ends, has_vres, tm, *refs):
    (x_ref, xp_ref, xn_ref, gpre_ref, mu_ref, w0_ref, w1_ref, w2_ref,
     a0_ref, a1_ref, a2_ref, g1_ref, g2_ref) = refs[:13]
    refs = refs[13:]
    if has_vres:
        v0_ref, v1_ref, v2_ref = refs[:3]
        refs = refs[3:]
    xr_ref, xk_ref, xv_ref, ld_ref, a_ref, g_ref = refs[:6]

    t0 = pl.program_id(0) * tm
    gpre = gpre_ref[...]
    h = _rms(x_ref[...], gpre)
    keep_p = 1.0 - _in_list(t0, seq_starts).astype(F32)
    keep_n = 1.0 - _in_list(t0 + tm, seq_ends).astype(F32)
    hp = _rms(xp_ref[7:8, :], gpre) * keep_p
    hn = _rms(xn_ref[0:1, :], gpre) * keep_n
    row = lax.broadcasted_iota(jnp.int32, h.shape, 0)
    prev = jnp.where(row == 0, hp, pltpu.roll(h, 1, 0))
    nxt = jnp.where(row == tm - 1, hn, pltpu.roll(h, tm - 1, 0))
    xx = 0.5 * (prev + nxt) - h

    def mix(i):
        return (h + xx * mu_ref[i:i + 1, :]).astype(BF16)

    xr_ref[...] = mix(0)
    xw = mix(1)
    xk_ref[...] = mix(2)
    xv = mix(3)
    xv_ref[...] = xv
    xa = mix(4)
    xg = mix(5)
    for d in range(2):
        tw = jnp.tanh(jnp.dot(xw, w1_ref[d], preferred_element_type=F32))
        z = w0_ref[d:d + 1, :] + _bdot(tw, w2_ref[d])
        ld_ref[d] = (-math.exp(-0.5) * jax.nn.sigmoid(z)).astype(BF16)
        ta = jnp.dot(xa, a1_ref[d], preferred_element_type=F32)
        a_ref[d] = jax.nn.sigmoid(a0_ref[d:d + 1, :] + _bdot(ta, a2_ref[d])).astype(BF16)
    tg = jax.nn.sigmoid(jnp.dot(xg, g1_ref[...], preferred_element_type=F32))
    g_ref[...] = _bdot(tg, g2_ref[...]).astype(BF16)
    if has_vres:
        vg_ref = refs[6]
        tv = jnp.dot(xv, v1_ref[...], preferred_element_type=F32)
        vg_ref[...] = jax.nn.sigmoid(v0_ref[...] + _bdot(tv, v2_ref[...])).astype(BF16)


def _pad_lora(w_in, w_out):
    r = w_in.shape[-1]
    rp = -(-r // LANES) * LANES
    pin = [(0, 0)] * (w_in.ndim - 1) + [(0, rp - r)]
    pout = [(0, 0)] * (w_out.ndim - 2) + [(0, rp - r), (0, 0)]
    return jnp.pad(w_in, pin).astype(BF16), jnp.pad(w_out, pout).astype(BF16)


def _rwkv_pre(x, seq_starts, seq_ends, gpre, mu, w0, w1, w2, a0, a1, a2, g1, g2, vres):
    T, D = x.shape
    tm = _tile(T, 128)
    nb8 = T // 8
    has_vres = vres is not None
    w1p, w2p = _pad_lora(w1, w2)
    a1p, a2p = _pad_lora(a1, a2)
    g1p, g2p = _pad_lora(g1, g2)

    def full(a):
        nd = a.ndim
        return pl.BlockSpec(a.shape, lambda i, _nd=nd: (0,) * _nd)

    args = [x, x, x, gpre, mu, w0, w1p, w2p, a0, a1p, a2p, g1p, g2p]
    in_specs = [
        pl.BlockSpec((tm, D), lambda i: (i, 0)),
        pl.BlockSpec((8, D), lambda i: (jnp.maximum(i * (tm // 8) - 1, 0), 0)),
        pl.BlockSpec((8, D), lambda i: (jnp.minimum((i + 1) * (tm // 8), nb8 - 1), 0)),
    ] + [full(a) for a in args[3:]]
    if has_vres:
        v0, v1, v2 = vres
        v1p, v2p = _pad_lora(v1, v2)
        extra = [v0.reshape(1, D), v1p, v2p]
        args += extra
        in_specs += [full(a) for a in extra]
    tok = pl.BlockSpec((tm, D), lambda i: (i, 0))
    tok2 = pl.BlockSpec((2, tm, D), lambda i: (0, i, 0))
    out_specs = [tok, tok, tok, tok2, tok2, tok]
    out_shape = [jax.ShapeDtypeStruct((T, D), BF16)] * 3 + [
        jax.ShapeDtypeStruct((2, T, D), BF16), jax.ShapeDtypeStruct((2, T, D), BF16),
        jax.ShapeDtypeStruct((T, D), BF16)]
    if has_vres:
        out_specs.append(tok)
        out_shape.append(jax.ShapeDtypeStruct((T, D), BF16))
    return pl.pallas_call(
        functools.partial(_rwkv_pre_kernel, seq_starts, seq_ends, has_vres, tm),
        grid=(T // tm,),
        in_specs=in_specs,
        out_specs=out_specs,
        out_shape=out_shape,
        compiler_params=_params(("parallel",)),
    )(*args)


def _wkv_kernel(L, G, nc, resets_f, resets_r, has_vres, *refs):
    r_ref, k_ref, v_ref, ld_ref, a_ref, kk_ref, ka_ref = refs[:7]
    refs = refs[7:]
    if has_vres:
        vf_ref, vg_ref = refs[:2]
        refs = refs[2:]
    y_ref, s_ref = refs

    d = pl.program_id(0)
    c = pl.program_id(2)
    ci = c + d * (nc - 1 - 2 * c)
    reset = ((d == 0) & _in_list(ci, resets_f)) | ((d == 1) & _in_list(ci, resets_r))

    @pl.when(reset)
    def _():
        s_ref[...] = jnp.zeros_like(s_ref)

    sgn = 1 - 2 * d
    tt = lax.broadcasted_iota(jnp.int32, (L, LANES), 0)
    ss = lax.broadcasted_iota(jnp.int32, (L, LANES), 1) % L
    delta = sgn * (tt - ss)
    incl = delta >= 0
    strict = delta > 0
    eye = (tt == ss).astype(F32)
    tri = (sgn * (lax.broadcasted_iota(jnp.int32, (L, L), 0)
                  - lax.broadcasted_iota(jnp.int32, (L, L), 1)) >= 0).astype(BF16)
    first_head = lax.broadcasted_iota(jnp.int32, (L, LANES), 1) < RWKV_HEAD
    bi = lax.broadcasted_iota(jnp.int32, (LANES, LANES), 0) // RWKV_HEAD
    bj = lax.broadcasted_iota(jnp.int32, (LANES, LANES), 1) // RWKV_HEAD
    same_head = bi == bj
    head_ones = same_head.astype(BF16)
    n_double = int(math.log2(L)) - 1

    def bd(x):
        xb = x.astype(BF16)
        zero = jnp.zeros_like(xb)
        return jnp.concatenate([jnp.where(first_head, xb, zero), jnp.where(first_head, zero, xb)], axis=0)

    kkcs = [k_ref[:, p * LANES:(p + 1) * LANES].astype(F32) * kk_ref[:, p * LANES:(p + 1) * LANES]
            for p in range(G)]
    n2_all = jnp.dot(jnp.concatenate([(kkc * kkc).astype(BF16) for kkc in kkcs], axis=0), head_ones,
                     preferred_element_type=F32)
    pairs = []
    for p in range(G):
        sl = slice(p * LANES, (p + 1) * LANES)
        r_ = r_ref[:, sl].astype(F32)
        k_ = k_ref[:, sl].astype(F32)
        v_ = v_ref[:, sl].astype(F32)
        ld_b = ld_ref[:, sl]
        ld_ = ld_b.astype(F32)
        a_ = a_ref[:, sl].astype(F32)
        if has_vres:
            v_ = v_ + (vf_ref[:, sl].astype(F32) - v_) * vg_ref[:, sl].astype(F32)
        kkn = kkcs[p] / jnp.maximum(jnp.sqrt(n2_all[p * L:(p + 1) * L]), 1e-12)
        kd = k_ * (1.0 + (a_ - 1.0) * ka_ref[:, sl])
        cs = jnp.dot(tri, ld_b, preferred_element_type=F32)
        w_inv = jnp.exp(-cs)
        bt = kkn * a_ * w_inv
        kt = kd * w_inv
        pairs.append(dict(
            sl=sl, v=v_, bt=bt, kt=kt, st=s_ref[p],
            lhs=jnp.concatenate([-kkn * jnp.exp(cs - ld_), r_ * jnp.exp(cs)], axis=0).astype(BF16),
            wend=jnp.exp(jnp.sum(ld_, axis=0, keepdims=True))))

    for pr in pairs:
        rhs_t = jnp.concatenate([bd(pr["bt"]), bd(pr["kt"]), pr["st"].astype(BF16)], axis=0)
        prod = _ntdot(pr["lhs"], rhs_t)
        pr["a_ab"] = jnp.where(strict, prod[:L, :LANES], 0.0)
        pr["a_rb"] = jnp.where(incl, prod[L:, :LANES], 0.0)
        a_ak = jnp.where(strict, prod[:L, LANES:2 * LANES], 0.0)
        a_rk = jnp.where(incl, prod[L:, LANES:2 * LANES], 0.0)
        pr["sp"] = prod[:, 2 * LANES:]
        pr["a_k"] = jnp.concatenate([a_ak, a_rk], axis=0)
    for pr in pairs:
        pr["spv"] = pr["sp"] + _bdot(pr["a_k"], bd(pr["v"]))
    for pr in pairs:
        x = _bdot(jnp.concatenate([pr["a_ab"], pr["a_rb"]], axis=0), bd(pr["a_ab"]))
        pr["tinv"] = eye + pr["a_ab"]
        pr["pw"] = x[:L]
        pr["rt"] = pr["a_rb"] + x[L:]
    for lvl in range(1, n_double + 1):
        last = lvl == n_double
        for pr in pairs:
            rows = [pr["tinv"], pr["rt"]] if last else [pr["tinv"], pr["rt"], pr["pw"]]
            x = _bdot(jnp.concatenate(rows, axis=0), bd(pr["pw"]))
            pr["tinv"] = pr["tinv"] + x[:L]
            pr["rt"] = pr["rt"] + x[L:2 * L]
            if not last:
                pr["pw"] = x[2 * L:]
    for pr in pairs:
        z = _bdot(jnp.concatenate([pr["tinv"], pr["rt"]], axis=0), bd(pr["spv"][:L]))
        pr["u"] = z[:L]
        pr["y"] = pr["spv"][L:] + z[L:]
    for p, pr in enumerate(pairs):
        y_ref[:, pr["sl"]] = pr["y"].astype(BF16)
        uv = jnp.concatenate([pr["u"], pr["v"]], axis=0).astype(BF16)
        bk = jnp.concatenate([pr["bt"], pr["kt"]], axis=0).astype(BF16)
        upd = lax.dot_general(uv, bk, _TN, preferred_element_type=F32)
        s_ref[p] = (pr["st"] + jnp.where(same_head, upd, 0.0)) * pr["wend"]


def _wkv(r, k, v, ld, a, k_k, k_a, vres, seq_starts, seq_ends):
    T, D = r.shape
    L = WKV_CHUNK
    npairs = D // LANES
    assert 2 * L == LANES and RWKV_HEAD == L
    G = math.gcd(npairs, WKV_PAIRS_PER_STEP)
    nc = T // L
    W = G * LANES
    resets_f = tuple(s // L for s in seq_starts)
    resets_r = tuple(e // L - 1 for e in seq_ends)
    has_vres = vres is not None

    def cidx(d, c):
        return c + d * (nc - 1 - 2 * c)

    tok = pl.BlockSpec((L, W), lambda d, p, c: (cidx(d, c), p))
    tokd = pl.BlockSpec((None, L, W), lambda d, p, c: (d, cidx(d, c), p))
    par = pl.BlockSpec((1, W), lambda d, p, c: (0, p))
    args = [r, k, v, ld, a, k_k, k_a]
    in_specs = [tok, tok, tok, tokd, tokd, par, par]
    if has_vres:
        args += list(vres)
        in_specs += [tok, tok]
    return pl.pallas_call(
        functools.partial(_wkv_kernel, L, G, nc, resets_f, resets_r, has_vres),
        grid=(2, npairs // G, nc),
        in_specs=in_specs,
        out_specs=tokd,
        out_shape=jax.ShapeDtypeStruct((2, T, D), BF16),
        scratch_shapes=[pltpu.VMEM((G, LANES, LANES), F32)],
        compiler_params=_params(("arbitrary", "arbitrary", "arbitrary")),
    )(*args)


def _rwkv_out_kernel(has_vres, *refs):
    (x_ref, y_ref, r_ref, k_ref, v_ref, a_ref, g_ref, ka_ref, rk_ref, gnw_ref, gnb_ref,
     wo_ref, gpost_ref) = refs[:13]
    refs = refs[13:]
    if has_vres:
        vf_ref, vg_ref = refs[:2]
        refs = refs[2:]
    o_ref, act_ref = refs
    D = x_ref.shape[1]

    bi = lax.broadcasted_iota(jnp.int32, (LANES, LANES), 0) // RWKV_HEAD
    bj = lax.broadcasted_iota(jnp.int32, (LANES, LANES), 1) // RWKV_HEAD
    head_ones = (bi == bj).astype(BF16)
    inv_n = 1.0 / RWKV_HEAD

    slabs = [slice(s * LANES, (s + 1) * LANES) for s in range(D // LANES)]
    ys = [y_ref[0, :, sl].astype(F32) + y_ref[1, :, sl].astype(F32) for sl in slabs]
    means = [_bdot(y, head_ones) * inv_n for y in ys]
    dys = [y - mu for y, mu in zip(ys, means)]
    vars_ = [_bdot(dy * dy, head_ones) * inv_n for dy in dys]
    rks = []
    for sl in slabs:
        a_mean = 0.5 * (a_ref[0, :, sl].astype(F32) + a_ref[1, :, sl].astype(F32))
        k_mean = k_ref[:, sl].astype(F32) * (1.0 + (a_mean - 1.0) * ka_ref[:, sl])
        rks.append(_bdot(r_ref[:, sl].astype(F32) * k_mean * rk_ref[:, sl], head_ones))
    for sl, dy, var, rk in zip(slabs, dys, vars_, rks):
        yn = dy * lax.rsqrt(var + GN_EPS) * gnw_ref[:, sl] + gnb_ref[:, sl]
        v_ = v_ref[:, sl].astype(F32)
        if has_vres:
            v_ = v_ + (vf_ref[:, sl].astype(F32) - v_) * vg_ref[:, sl].astype(F32)
        act_ref[:, sl] = ((yn + rk * v_) * g_ref[:, sl].astype(F32)).astype(BF16)
    h = jnp.dot(act_ref[...], wo_ref[...], preferred_element_type=F32)
    o_ref[...] = x_ref[...] + _rms(h, gpost_ref[...])


def _rwkv_out(x, y, r, k, v, a, g, k_a, r_k, gn_w, gn_b, w_o, gpost, vres):
    T, D = x.shape
    tm = _tile(T, 256, 16)
    has_vres = vres is not None
    tok = pl.BlockSpec((tm, D), lambda i: (i, 0))
    tok2 = pl.BlockSpec((2, tm, D), lambda i: (0, i, 0))
    par = pl.BlockSpec((1, D), lambda i: (0, 0))
    args = [x, y, r, k, v, a, g, k_a, r_k, gn_w, gn_b, w_o, gpost]
    in_specs = [tok, tok2, tok, tok, tok, tok2, tok, par, par, par, par,
                pl.BlockSpec((D, D), lambda i: (0, 0)), par]
    if has_vres:
        args += list(vres)
        in_specs += [tok, tok]
    return pl.pallas_call(
        functools.partial(_rwkv_out_kernel, has_vres),
        grid=(T // tm,),
        in_specs=in_specs,
        out_specs=tok,
        out_shape=jax.ShapeDtypeStruct((T, D), F32),
        scratch_shapes=[pltpu.VMEM((tm, D), BF16)],
        compiler_params=_params(("parallel",)),
    )(*args)


def _qkv_kernel(tm, tn, deint, x_ref, gpre_ref, w_ref, cos_ref, sin_ref, o_ref, xn_ref, *acc):
    j = pl.program_id(1)

    @pl.when(j == 0)
    def _():
        def put(rows, y):
            xn_ref[rows, :] = y.astype(BF16)
        _rms_rows(x_ref, gpre_ref[...], put)

    cos = cos_ref[...]
    sin = sin_ref[...]
    xn = xn_ref[...]
    per_dot = MXU_COLS // ATT_HEAD
    for s in range(tn // ATT_HEAD):
        sl = slice(s * ATT_HEAD, (s + 1) * ATT_HEAD)
        if s % per_dot == 0:
            prod = jnp.dot(xn, w_ref[:, s * ATT_HEAD:s * ATT_HEAD + MXU_COLS],
                           preferred_element_type=F32)
        t = prod[:, (s % per_dot) * ATT_HEAD:(s % per_dot + 1) * ATT_HEAD]
        rot = t * cos + pltpu.roll(t, ATT_HEAD // 2, 1) * sin
        if deint == 1:
            o_ref[:, sl] = rot.astype(BF16)
        else:
            acc_ref, = acc
            acc_ref[s] = rot
            for c in range(deint):
                o_ref[c, :, sl] = acc_ref[s, pl.ds(c, tm // deint, stride=deint), :].astype(BF16)


def _rope_tables(pos):
    half = ROPE_DIM // 2
    inv = ROPE_THETA ** (-jnp.arange(half, dtype=F32) / half)
    ang = pos.astype(F32)[:, None] * inv[None, :]
    cos, sin = jnp.cos(ang), jnp.sin(ang)
    T = pos.shape[0]
    gap = ATT_HEAD // 2 - half
    ones = jnp.ones((T, gap), F32)
    zeros = jnp.zeros((T, gap), F32)
    cos_t = jnp.concatenate([cos, ones, cos, ones], axis=1)
    sin_t = jnp.concatenate([-sin, zeros, sin, zeros], axis=1)
    return (jnp.stack([cos_t, jnp.ones_like(cos_t)]), jnp.stack([sin_t, jnp.zeros_like(sin_t)]))


def _rope_perm(w_qkv, D):
    half = ROPE_DIM // 2
    mid = ATT_HEAD // 2
    idx = list(range(ATT_HEAD))
    idx[half:2 * half], idx[mid:mid + half] = idx[mid:mid + half], idx[half:2 * half]
    w = w_qkv.reshape(w_qkv.shape[0], N_GROUPS, 3, D // ATT_HEAD, ATT_HEAD)
    qk = jnp.take(w[:, :, :2], jnp.asarray(idx, jnp.int32), axis=-1)
    return jnp.concatenate([qk, w[:, :, 2:]], axis=2).reshape(w_qkv.shape)


def _qkv(x, gpre, w, tables, gi, dil):
    T, D = x.shape
    tn = _tile(D, 2048, MXU_COLS)
    nj = 3 * D // tn

    def plane(j):
        return (j * tn) // (2 * D)

    tm = _tile(T, 1024 if dil == 1 else 512, 16 * dil)
    tab = pl.BlockSpec((None, tm, ATT_HEAD), lambda i, j: (plane(j), i, 0))
    if dil == 1:
        out = pl.BlockSpec((None, tm, tn), lambda i, j: (0, i, j))
        deint_scratch = []
    else:
        out = pl.BlockSpec((dil, tm // dil, tn), lambda i, j: (0, i, j))
        deint_scratch = [pltpu.VMEM((tn // ATT_HEAD, tm, ATT_HEAD), F32)]
    return pl.pallas_call(
        functools.partial(_qkv_kernel, tm, tn, dil),
        grid=(T // tm, nj),
        in_specs=[
            pl.BlockSpec((tm, D), lambda i, j: (i, 0)),
            pl.BlockSpec((1, D), lambda i, j: (0, 0)),
            pl.BlockSpec((D, tn), lambda i, j: (0, gi * nj + j)),
            tab, tab,
        ],
        out_specs=out,
        out_shape=jax.ShapeDtypeStruct((dil, T // dil, 3 * D), BF16),
        scratch_shapes=[pltpu.VMEM((tm, D), BF16)] + deint_scratch,
        compiler_params=_params(("parallel", "arbitrary")),
    )(x, gpre, w, *tables)


def _attn_kernel(bq, n_heads, bounds, q_ref, kp_ref, kc_ref, kn_ref, vp_ref, vc_ref, vn_ref,
                 o_ref, lse_ref):
    row0 = pl.program_id(0) * bq
    lo = jnp.int32(bounds[0])
    hi = jnp.int32(bounds[-1])
    for b in bounds[1:-1]:
        lo = jnp.where(row0 >= b, b, lo)
    for b in reversed(bounds[1:-1]):
        hi = jnp.where(row0 < b, b, hi)
    nk = bq + 2 * ATT_HALF
    qrow = row0 + lax.broadcasted_iota(jnp.int32, (bq, nk), 0)
    krow = row0 - ATT_HALF + lax.broadcasted_iota(jnp.int32, (bq, nk), 1)
    valid = (jnp.abs(krow - qrow) <= ATT_HALF) & (krow >= lo) & (krow < hi)
    scale = ATT_HEAD ** -0.5
    lane = lax.broadcasted_iota(jnp.int32, (bq, LANES), 1)
    lse_all = jnp.zeros((bq, LANES), F32)
    for h in range(n_heads):
        sl = slice(h * ATT_HEAD, (h + 1) * ATT_HEAD)
        q = q_ref[:, sl]
        kcat = jnp.concatenate([kp_ref[:, sl], kc_ref[:, sl], kn_ref[:, sl]], axis=0)
        vcat = jnp.concatenate([vp_ref[:, sl], vc_ref[:, sl], vn_ref[:, sl]], axis=0)
        s = lax.dot_general(q, kcat, _NT, preferred_element_type=F32) * (scale * LOG2E)
        s = jnp.where(valid, s, -1e30)
        m = jnp.max(s, axis=-1, keepdims=True)
        p = jnp.exp2(s - m)
        l = jnp.sum(p, axis=-1, keepdims=True)
        o = jnp.dot(p.astype(BF16), vcat, preferred_element_type=F32) / l
        o_ref[:, sl] = o.astype(BF16)
        lse_all = jnp.where(lane == h, (m + jnp.log2(l)) * (1.0 / LOG2E), lse_all)
    lse_ref[...] = lse_all


def _attn_group(qkv, D, seq_starts, seq_ends):
    dil, rows, _ = qkv.shape
    bounds = tuple(s // dil for s in seq_starts) + (seq_ends[-1] // dil,)
    seg = [b - a for a, b in zip(bounds[:-1], bounds[1:])]
    bq = 128
    for s in seg:
        bq = math.gcd(bq, s)
    assert bq >= ATT_HALF, "sequence too short for the dilated band blocks"
    nb = rows // bq

    per = bq // ATT_HALF
    nh = rows // ATT_HALF

    def cur(kind):
        return pl.BlockSpec((None, bq, D), lambda i, c: (c, i, kind))

    def halo(kind, after):
        def imap(i, c):
            blk = (i + 1) * per if after else i * per - 1
            return (c, jnp.clip(blk, 0, nh - 1), kind)
        return pl.BlockSpec((None, ATT_HALF, D), imap)

    return pl.pallas_call(
        functools.partial(_attn_kernel, bq, D // ATT_HEAD, bounds),
        grid=(nb, dil),
        in_specs=[cur(0), halo(1, False), cur(1), halo(1, True),
                  halo(2, False), cur(2), halo(2, True)],
        out_specs=[pl.BlockSpec((None, bq, D), lambda i, c: (c, i, 0)),
                   pl.BlockSpec((None, bq, LANES), lambda i, c: (c, i, 0))],
        out_shape=[jax.ShapeDtypeStruct((dil, rows, D), BF16),
                   jax.ShapeDtypeStruct((dil, rows, LANES), F32)],
        compiler_params=_params(("parallel", "arbitrary")),
    )(qkv, qkv, qkv, qkv, qkv, qkv, qkv)


def _attn_out_kernel(tm, x_ref, o0_ref, o1_ref, o2_ref, l0_ref, l1_ref, l2_ref, wo_ref, gpost_ref,
                     out_ref, act_ref, *scratch):
    D = x_ref.shape[1]

    def token_major(ref, sl, s):
        dil = ref.shape[0]
        if dil == 1:
            return ref[0, :, sl].astype(F32)
        for c in range(dil):
            s[pl.ds(c, tm // dil, stride=dil), :] = ref[c, :, sl].astype(F32)
        return s[...]

    free = list(scratch)
    l0, l1, l2 = [token_major(r, slice(0, LANES), free.pop() if r.shape[0] > 1 else None)
                  for r in (l0_ref, l1_ref, l2_ref)]
    m = jnp.maximum(jnp.maximum(l0, l1), l2)
    e0, e1, e2 = jnp.exp(l0 - m), jnp.exp(l1 - m), jnp.exp(l2 - m)
    inv = 1.0 / (e0 + e1 + e2)
    w0, w1, w2 = e0 * inv, e1 * inv, e2 * inv
    h = None
    heads_per_dot = MXU_COLS // ATT_HEAD
    for k in range(D // ATT_HEAD):
        sl = slice(k * ATT_HEAD, (k + 1) * ATT_HEAD)
        pool = list(free)
        o0, o1, o2 = [token_major(r, sl, pool.pop() if r.shape[0] > 1 else None)
                      for r in (o0_ref, o1_ref, o2_ref)]
        act_ref[:, sl] = (w0[:, k:k + 1] * o0 + w1[:, k:k + 1] * o1
                          + w2[:, k:k + 1] * o2).astype(BF16)
        if (k + 1) % heads_per_dot == 0:
            ks = slice((k + 1 - heads_per_dot) * ATT_HEAD, (k + 1) * ATT_HEAD)
            part = jnp.dot(act_ref[:, ks], wo_ref[ks, :], preferred_element_type=F32)
            h = part if h is None else h + part
    out_ref[...] = x_ref[...] + _rms(h, gpost_ref[...])


def _attn_out(x, os_, lses, w_o, gpost):
    T, D = x.shape
    tm = _tile(T, 256, 16 * max(o.shape[0] for o in os_))
    tok = pl.BlockSpec((tm, D), lambda i: (i, 0))

    def planes(a):
        dil = a.shape[0]
        return pl.BlockSpec((dil, tm // dil, a.shape[2]), lambda i: (0, i, 0))

    n_scratch = sum(2 for o in os_ if o.shape[0] > 1)
    return pl.pallas_call(
        functools.partial(_attn_out_kernel, tm),
        grid=(T // tm,),
        in_specs=[tok] + [planes(a) for a in os_] + [planes(a) for a in lses]
        + [pl.BlockSpec((D, D), lambda i: (0, 0)), pl.BlockSpec((1, D), lambda i: (0, 0))],
        out_specs=tok,
        out_shape=jax.ShapeDtypeStruct((T, D), F32),
        scratch_shapes=[pltpu.VMEM((tm, D), BF16)] + [pltpu.VMEM((tm, LANES), F32)] * n_scratch,
        compiler_params=_params(("parallel",)),
    )(x, *os_, *lses, w_o, gpost)


def kernel(x_prompt, x_sample, norm_pre, norm_post, ffn_w_gate, ffn_w_up, ffn_w_down, rwkv_mu, rwkv_w_rkv, rwkv_w0, rwkv_w1, rwkv_w2, rwkv_a0, rwkv_a1, rwkv_a2, rwkv_v0, rwkv_v1, rwkv_v2, rwkv_g1, rwkv_g2, rwkv_k_k, rwkv_k_a, rwkv_r_k, rwkv_gn_w, rwkv_gn_b, rwkv_w_o, attn_w_qkv, attn_w_o):
    D = x_prompt.shape[-1]
    depth = norm_pre.shape[0]
    seqs = [x_prompt.shape[1]] * x_prompt.shape[0] + [x_sample.shape[1]] * x_sample.shape[0]
    seq_starts, seq_ends, t = [], [], 0
    for s in seqs:
        seq_starts.append(t)
        t += s
        seq_ends.append(t)
    seq_starts, seq_ends = tuple(seq_starts), tuple(seq_ends)
    x = jnp.concatenate([x_prompt.reshape(-1, D), x_sample.reshape(-1, D)], axis=0)
    pos = jnp.concatenate([jnp.arange(s, dtype=jnp.int32) for s in seqs])
    tables = _rope_tables(pos)

    def vec(a):
        return a.reshape(1, D)

    v_first = None
    for layer in range(depth):
        x = _ffn(x, vec(norm_pre[layer, 0]), ffn_w_gate[layer, 0].astype(BF16),
                 ffn_w_up[layer, 0].astype(BF16), ffn_w_down[layer, 0].astype(BF16),
                 vec(norm_post[layer, 0]))
        j = layer // 2
        if layer % 2 == 0:
            vres_w = None if j == 0 else (rwkv_v0[j - 1], rwkv_v1[j - 1], rwkv_v2[j - 1])
            pre = _rwkv_pre(x, seq_starts, seq_ends, vec(norm_pre[layer, 1]), rwkv_mu[j],
                            rwkv_w0[j], rwkv_w1[j], rwkv_w2[j], rwkv_a0[j], rwkv_a1[j], rwkv_a2[j],
                            rwkv_g1[j], rwkv_g2[j], vres_w)
            xr, xk, xv, ld, a, g = pre[:6]
            r = _mm(xr, rwkv_w_rkv[j, 0].astype(BF16))
            k = _mm(xk, rwkv_w_rkv[j, 1].astype(BF16))
            v = _mm(xv, rwkv_w_rkv[j, 2].astype(BF16))
            if j == 0:
                v_first, vres = v, None
            else:
                vres = (v_first, pre[6])
            y = _wkv(r, k, v, ld, a, vec(rwkv_k_k[j]), vec(rwkv_k_a[j]), vres, seq_starts, seq_ends)
            x = _rwkv_out(x, y, r, k, v, a, g, vec(rwkv_k_a[j]), vec(rwkv_r_k[j]),
                          vec(rwkv_gn_w[j]), vec(rwkv_gn_b[j]), rwkv_w_o[j].astype(BF16),
                          vec(norm_post[layer, 1]), vres)
        else:
            w_qkv = _rope_perm(attn_w_qkv[j].astype(BF16), D)
            os_, lses = [], []
            for gi, (_, dil) in enumerate(DILATED_GROUPS):
                qkv = _qkv(x, vec(norm_pre[layer, 1]), w_qkv, tables, gi, dil)
                o, lse = _attn_group(qkv, D, seq_starts, seq_ends)
                os_.append(o)
                lses.append(lse)
            x = _attn_out(x, os_, lses, attn_w_o[j].astype(BF16), vec(norm_post[layer, 1]))
        x = _ffn(x, vec(norm_pre[layer, 2]), ffn_w_gate[layer, 1].astype(BF16),
                 ffn_w_up[layer, 1].astype(BF16), ffn_w_down[layer, 1].astype(BF16),
                 vec(norm_post[layer, 2]))
    n_p = x_prompt.shape[0] * x_prompt.shape[1]
    return (x[:n_p].reshape(x_prompt.shape), x[n_p:].reshape(x_sample.shape))
```

```python
import functools
import math

import jax
import jax.numpy as jnp
from jax import lax
from jax.experimental import pallas as pl
from jax.experimental.pallas import tpu as pltpu

F32 = jnp.float32
BF16 = jnp.bfloat16

NORM_EPS = 1e-6
GN_EPS = 64e-5
RWKV_HEAD = 64
ATT_HEAD = 128
ROPE_DIM = ATT_HEAD // 4
ROPE_THETA = 500000.0
DILATED_GROUPS = ((128, 1), (512, 4), (2048, 16))
N_GROUPS = len(DILATED_GROUPS)
ATT_HALF = 64
LANES = 128
MXU_COLS = 256
LOG2E = math.log2(math.e)
WKV_CHUNK = 64
WKV_PAIRS_PER_STEP = 16
VMEM_LIMIT = 56 * 1024 * 1024

_NT = (((1,), (1,)), ((), ()))
_TN = (((0,), (0,)), ((), ()))


def _tile(n, pref, mult=8):
    if n <= pref:
        return n
    t = (pref // mult) * mult
    while t >= mult:
        if n % t == 0:
            return t
        t -= mult
    return n


def _in_list(v, lst):
    r = v == lst[0]
    for s in lst[1:]:
        r = r | (v == s)
    return r


def _rms(x, g):
    return x * lax.rsqrt(jnp.mean(x * x, axis=-1, keepdims=True) + NORM_EPS) * g


NORM_ROWS = 32


def _rms_rows(src_ref, g, emit):
    for r0 in range(0, src_ref.shape[0], NORM_ROWS):
        rows = slice(r0, min(r0 + NORM_ROWS, src_ref.shape[0]))
        emit(rows, _rms(src_ref[rows, :], g))


def _bdot(a, b):
    return jnp.dot(a.astype(BF16), b.astype(BF16), preferred_element_type=F32)


def _ntdot(a, b):
    return lax.dot_general(a, b, _NT, preferred_element_type=F32)


def _params(sem):
    return pltpu.CompilerParams(dimension_semantics=sem, vmem_limit_bytes=VMEM_LIMIT)


def _ffn_kernel(x_ref, gpre_ref, wg_ref, wu_ref, wd_ref, gpost_ref, o_ref, xn_ref, acc_ref):
    j = pl.program_id(1)

    @pl.when(j == 0)
    def _():
        def put(rows, y):
            xn_ref[rows, :] = y.astype(BF16)
        _rms_rows(x_ref, gpre_ref[...], put)
        acc_ref[...] = jnp.zeros_like(acc_ref)

    xn = xn_ref[...]
    g = jnp.dot(xn, wg_ref[...], preferred_element_type=F32)
    u = jnp.dot(xn, wu_ref[...], preferred_element_type=F32)
    h = (g * jax.nn.sigmoid(g)) * u
    acc_ref[...] += jnp.dot(h.astype(BF16), wd_ref[...], preferred_element_type=F32)

    @pl.when(j == pl.num_programs(1) - 1)
    def _():
        def put(rows, y):
            o_ref[rows, :] = x_ref[rows, :] + y
        _rms_rows(acc_ref, 0.5 * gpost_ref[...], put)


def _ffn(x, gpre, wg, wu, wd, gpost, layer, k):
    T, D = x.shape
    F = wg.shape[-1]
    tm = _tile(T, 768, MXU_COLS)
    tf = _tile(F, 512, LANES)
    return pl.pallas_call(
        _ffn_kernel,
        grid=(T // tm, F // tf),
        in_specs=[
            pl.BlockSpec((tm, D), lambda i, j: (i, 0)),
            pl.BlockSpec((1, D), lambda i, j: (0, 0)),
            pl.BlockSpec((None, None, D, tf), lambda i, j: (layer, k, 0, j)),
            pl.BlockSpec((None, None, D, tf), lambda i, j: (layer, k, 0, j)),
            pl.BlockSpec((None, None, tf, D), lambda i, j: (layer, k, j, 0)),
            pl.BlockSpec((1, D), lambda i, j: (0, 0)),
        ],
        out_specs=pl.BlockSpec((tm, D), lambda i, j: (i, 0)),
        out_shape=jax.ShapeDtypeStruct((T, D), F32),
        scratch_shapes=[pltpu.VMEM((tm, D), BF16), pltpu.VMEM((tm, D), F32)],
        compiler_params=_params(("parallel", "arbitrary")),
    )(x, gpre, wg, wu, wd, gpost)


def _mm_kernel(a_ref, w_ref, o_ref):
    a = a_ref[...]
    for c in range(0, o_ref.shape[1], MXU_COLS):
        o_ref[:, c:c + MXU_COLS] = jnp.dot(
            a, w_ref[:, c:c + MXU_COLS], preferred_element_type=F32).astype(o_ref.dtype)


def _mm(a, w, lead):
    M, K = a.shape
    N = w.shape[-1]
    tm = _tile(M, 1024)
    tn = _tile(N, 1024, MXU_COLS)
    return pl.pallas_call(
        _mm_kernel,
        grid=(M // tm, N // tn),
        in_specs=[
            pl.BlockSpec((tm, K), lambda i, j: (i, 0)),
            pl.BlockSpec((None,) * len(lead) + (K, tn), lambda i, j: tuple(lead) + (0, j)),
        ],
        out_specs=pl.BlockSpec((tm, tn), lambda i, j: (i, j)),
        out_shape=jax.ShapeDtypeStruct((M, N), BF16),
        compiler_params=_params(("parallel", "arbitrary")),
    )(a, w)


def _rwkv_pre_kernel(seq_starts, seq_ends, has_vres, tm, *refs):
    (x_ref, xp_ref, xn_ref, gpre_ref, mu_ref, w0_ref, w1_ref, w2_ref,
     a0_ref, a1_ref, a2_ref, g1_ref, g2_ref) = refs[:13]
    refs = refs[13:]
    if has_vres:
        v0_ref, v1_ref, v2_ref = refs[:3]
        refs = refs[3:]
    xr_ref, xk_ref, xv_ref, ld_ref, a_ref, g_ref = refs[:6]

    t0 = pl.program_id(0) * tm
    gpre = gpre_ref[...]
    h = _rms(x_ref[...], gpre)
    keep_p = 1.0 - _in_list(t0, seq_starts).astype(F32)
    keep_n = 1.0 - _in_list(t0 + tm, seq_ends).astype(F32)
    hp = _rms(xp_ref[7:8, :], gpre) * keep_p
    hn = _rms(xn_ref[0:1, :], gpre) * keep_n
    row = lax.broadcasted_iota(jnp.int32, h.shape, 0)
    prev = jnp.where(row == 0, hp, pltpu.roll(h, 1, 0))
    nxt = jnp.where(row == tm - 1, hn, pltpu.roll(h, tm - 1, 0))
    xx = 0.5 * (prev + nxt) - h

    def mix(i):
        return (h + xx * mu_ref[i:i + 1, :]).astype(BF16)

    xr_ref[...] = mix(0)
    xw = mix(1)
    xk_ref[...] = mix(2)
    xv = mix(3)
    xv_ref[...] = xv
    xa = mix(4)
    xg = mix(5)
    for d in range(2):
        tw = jnp.tanh(jnp.dot(xw, w1_ref[d], preferred_element_type=F32))
        z = w0_ref[d:d + 1, :] + _bdot(tw, w2_ref[d])
        ld_ref[d] = (-math.exp(-0.5) * jax.nn.sigmoid(z)).astype(BF16)
        ta = jnp.dot(xa, a1_ref[d], preferred_element_type=F32)
        a_ref[d] = jax.nn.sigmoid(a0_ref[d:d + 1, :] + _bdot(ta, a2_ref[d])).astype(BF16)
    tg = jax.nn.sigmoid(jnp.dot(xg, g1_ref[...], preferred_element_type=F32))
    g_ref[...] = _bdot(tg, g2_ref[...]).astype(BF16)
    if has_vres:
        vg_ref = refs[6]
        tv = jnp.dot(xv, v1_ref[...], preferred_element_type=F32)
        vg_ref[...] = jax.nn.sigmoid(v0_ref[...] + _bdot(tv, v2_ref[...])).astype(BF16)


def _pad_lora(w_in, w_out):
    r = w_in.shape[-1]
    rp = -(-r // LANES) * LANES
    pin = [(0, 0)] * (w_in.ndim - 1) + [(0, rp - r)]
    pout = [(0, 0)] * (w_out.ndim - 2) + [(0, rp - r), (0, 0)]
    return jnp.pad(w_in, pin).astype(BF16), jnp.pad(w_out, pout).astype(BF16)


def _rwkv_pre(x, seq_starts, seq_ends, gpre, mu, w0, w1, w2, a0, a1, a2, g1, g2, vres):
    T, D = x.shape
    tm = _tile(T, 256, 16)
    nb8 = T // 8
    has_vres = vres is not None
    w1p, w2p = _pad_lora(w1, w2)
    a1p, a2p = _pad_lora(a1, a2)
    g1p, g2p = _pad_lora(g1, g2)

    def full(a):
        nd = a.ndim
        return pl.BlockSpec(a.shape, lambda i, _nd=nd: (0,) * _nd)

    args = [x, x, x, gpre, mu, w0, w1p, w2p, a0, a1p, a2p, g1p, g2p]
    in_specs = [
        pl.BlockSpec((tm, D), lambda i: (i, 0)),
        pl.BlockSpec((8, D), lambda i: (jnp.maximum(i * (tm // 8) - 1, 0), 0)),
        pl.BlockSpec((8, D), lambda i: (jnp.minimum((i + 1) * (tm // 8), nb8 - 1), 0)),
    ] + [full(a) for a in args[3:]]
    if has_vres:
        v0, v1, v2 = vres
        v1p, v2p = _pad_lora(v1, v2)
        extra = [v0.reshape(1, D), v1p, v2p]
        args += extra
        in_specs += [full(a) for a in extra]
    tok = pl.BlockSpec((tm, D), lambda i: (i, 0))
    tok2 = pl.BlockSpec((2, tm, D), lambda i: (0, i, 0))
    out_specs = [tok, tok, tok, tok2, tok2, tok]
    out_shape = [jax.ShapeDtypeStruct((T, D), BF16)] * 3 + [
        jax.ShapeDtypeStruct((2, T, D), BF16), jax.ShapeDtypeStruct((2, T, D), BF16),
        jax.ShapeDtypeStruct((T, D), BF16)]
    if has_vres:
        out_specs.append(tok)
        out_shape.append(jax.ShapeDtypeStruct((T, D), BF16))
    return pl.pallas_call(
        functools.partial(_rwkv_pre_kernel, seq_starts, seq_ends, has_vres, tm),
        grid=(T // tm,),
        in_specs=in_specs,
        out_specs=out_specs,
        out_shape=out_shape,
        compiler_params=_params(("parallel",)),
    )(*args)


def _wkv_kernel(L, G, nc, resets_f, resets_r, has_vres, *refs):
    r_ref, k_ref, v_ref, ld_ref, a_ref, kk_ref, ka_ref = refs[:7]
    refs = refs[7:]
    if has_vres:
        vf_ref, vg_ref = refs[:2]
        refs = refs[2:]
    y_ref, s_ref = refs

    d = pl.program_id(0)
    c = pl.program_id(2)
    ci = c + d * (nc - 1 - 2 * c)
    reset = ((d == 0) & _in_list(ci, resets_f)) | ((d == 1) & _in_list(ci, resets_r))

    @pl.when(reset)
    def _():
        s_ref[...] = jnp.zeros_like(s_ref)

    sgn = 1 - 2 * d
    tt = lax.broadcasted_iota(jnp.int32, (L, LANES), 0)
    ss = lax.broadcasted_iota(jnp.int32, (L, LANES), 1) % L
    delta = sgn * (tt - ss)
    incl = delta >= 0
    strict = delta > 0
    eye = (tt == ss).astype(F32)
    tri = (sgn * (lax.broadcasted_iota(jnp.int32, (L, L), 0)
                  - lax.broadcasted_iota(jnp.int32, (L, L), 1)) >= 0).astype(BF16)
    first_head = lax.broadcasted_iota(jnp.int32, (L, LANES), 1) < RWKV_HEAD
    bi = lax.broadcasted_iota(jnp.int32, (LANES, LANES), 0) // RWKV_HEAD
    bj = lax.broadcasted_iota(jnp.int32, (LANES, LANES), 1) // RWKV_HEAD
    same_head = bi == bj
    head_ones = same_head.astype(BF16)
    n_double = int(math.log2(L)) - 1

    def bd(x):
        xb = x.astype(BF16)
        zero = jnp.zeros_like(xb)
        return jnp.concatenate([jnp.where(first_head, xb, zero), jnp.where(first_head, zero, xb)], axis=0)

    kkcs = [k_ref[:, p * LANES:(p + 1) * LANES].astype(F32) * kk_ref[:, p * LANES:(p + 1) * LANES]
            for p in range(G)]
    n2_all = jnp.dot(jnp.concatenate([(kkc * kkc).astype(BF16) for kkc in kkcs], axis=0), head_ones,
                     preferred_element_type=F32)
    pairs = []
    for p in range(G):
        sl = slice(p * LANES, (p + 1) * LANES)
        r_ = r_ref[:, sl].astype(F32)
        k_ = k_ref[:, sl].astype(F32)
        v_ = v_ref[:, sl].astype(F32)
        ld_b = ld_ref[:, sl]
        ld_ = ld_b.astype(F32)
        a_ = a_ref[:, sl].astype(F32)
        if has_vres:
            v_ = v_ + (vf_ref[:, sl].astype(F32) - v_) * vg_ref[:, sl].astype(F32)
        kkn = kkcs[p] / jnp.maximum(jnp.sqrt(n2_all[p * L:(p + 1) * L]), 1e-12)
        kd = k_ * (1.0 + (a_ - 1.0) * ka_ref[:, sl])
        cs = jnp.dot(tri, ld_b, preferred_element_type=F32)
        w_inv = jnp.exp(-cs)
        bt = kkn * a_ * w_inv
        kt = kd * w_inv
        pairs.append(dict(
            sl=sl, v=v_, bt=bt, kt=kt, st=s_ref[p],
            lhs=jnp.concatenate([-kkn * jnp.exp(cs - ld_), r_ * jnp.exp(cs)], axis=0).astype(BF16),
            wend=jnp.exp(jnp.sum(ld_, axis=0, keepdims=True))))

    for pr in pairs:
        rhs_t = jnp.concatenate([bd(pr["bt"]), bd(pr["kt"]), pr["st"].astype(BF16)], axis=0)
        prod = _ntdot(pr["lhs"], rhs_t)
        pr["a_ab"] = jnp.where(strict, prod[:L, :LANES], 0.0)
        pr["a_rb"] = jnp.where(incl, prod[L:, :LANES], 0.0)
        a_ak = jnp.where(strict, prod[:L, LANES:2 * LANES], 0.0)
        a_rk = jnp.where(incl, prod[L:, LANES:2 * LANES], 0.0)
        pr["sp"] = prod[:, 2 * LANES:]
        pr["a_k"] = jnp.concatenate([a_ak, a_rk], axis=0)
    for pr in pairs:
        pr["spv"] = pr["sp"] + _bdot(pr["a_k"], bd(pr["v"]))
    for pr in pairs:
        x = _bdot(jnp.concatenate([pr["a_ab"], pr["a_rb"]], axis=0), bd(pr["a_ab"]))
        pr["tinv"] = eye + pr["a_ab"]
        pr["pw"] = x[:L]
        pr["rt"] = pr["a_rb"] + x[L:]
    for lvl in range(1, n_double + 1):
        last = lvl == n_double
        for pr in pairs:
            rows = [pr["tinv"], pr["rt"]] if last else [pr["tinv"], pr["rt"], pr["pw"]]
            x = _bdot(jnp.concatenate(rows, axis=0), bd(pr["pw"]))
            pr["tinv"] = pr["tinv"] + x[:L]
            pr["rt"] = pr["rt"] + x[L:2 * L]
            if not last:
                pr["pw"] = x[2 * L:]
    for pr in pairs:
        z = _bdot(jnp.concatenate([pr["tinv"], pr["rt"]], axis=0), bd(pr["spv"][:L]))
        pr["u"] = z[:L]
        pr["y"] = pr["spv"][L:] + z[L:]
    for p, pr in enumerate(pairs):
        y_ref[:, pr["sl"]] = pr["y"].astype(BF16)
        uv = jnp.concatenate([pr["u"], pr["v"]], axis=0).astype(BF16)
        bk = jnp.concatenate([pr["bt"], pr["kt"]], axis=0).astype(BF16)
        upd = lax.dot_general(uv, bk, _TN, preferred_element_type=F32)
        s_ref[p] = (pr["st"] + jnp.where(same_head, upd, 0.0)) * pr["wend"]


def _wkv(r, k, v, ld, a, k_k, k_a, vres, seq_starts, seq_ends):
    T, D = r.shape
    L = WKV_CHUNK
    npairs = D // LANES
    assert 2 * L == LANES and RWKV_HEAD == L
    G = math.gcd(npairs, WKV_PAIRS_PER_STEP)
    nc = T // L
    W = G * LANES
    resets_f = tuple(s // L for s in seq_starts)
    resets_r = tuple(e // L - 1 for e in seq_ends)
    has_vres = vres is not None

    def cidx(d, c):
        return c + d * (nc - 1 - 2 * c)

    tok = pl.BlockSpec((L, W), lambda d, p, c: (cidx(d, c), p))
    tokd = pl.BlockSpec((None, L, W), lambda d, p, c: (d, cidx(d, c), p))
    par = pl.BlockSpec((1, W), lambda d, p, c: (0, p))
    args = [r, k, v, ld, a, k_k, k_a]
    in_specs = [tok, tok, tok, tokd, tokd, par, par]
    if has_vres:
        args += list(vres)
        in_specs += [tok, tok]
    return pl.pallas_call(
        functools.partial(_wkv_kernel, L, G, nc, resets_f, resets_r, has_vres),
        grid=(2, npairs // G, nc),
        in_specs=in_specs,
        out_specs=tokd,
        out_shape=jax.ShapeDtypeStruct((2, T, D), BF16),
        scratch_shapes=[pltpu.VMEM((G, LANES, LANES), F32)],
        compiler_params=_params(("arbitrary", "arbitrary", "arbitrary")),
    )(*args)


def _rwkv_out_kernel(has_vres, *refs):
    (x_ref, y_ref, r_ref, k_ref, v_ref, a_ref, g_ref, ka_ref, rk_ref, gnw_ref, gnb_ref,
     wo_ref, gpost_ref) = refs[:13]
    refs = refs[13:]
    if has_vres:
        vf_ref, vg_ref = refs[:2]
        refs = refs[2:]
    o_ref, act_ref = refs
    D = x_ref.shape[1]

    bi = lax.broadcasted_iota(jnp.int32, (LANES, LANES), 0) // RWKV_HEAD
    bj = lax.broadcasted_iota(jnp.int32, (LANES, LANES), 1) // RWKV_HEAD
    head_ones = (bi == bj).astype(BF16)
    inv_n = 1.0 / RWKV_HEAD

    slabs = [slice(s * LANES, (s + 1) * LANES) for s in range(D // LANES)]
    ys = [y_ref[0, :, sl].astype(F32) + y_ref[1, :, sl].astype(F32) for sl in slabs]
    means = [_bdot(y, head_ones) * inv_n for y in ys]
    dys = [y - mu for y, mu in zip(ys, means)]
    vars_ = [_bdot(dy * dy, head_ones) * inv_n for dy in dys]
    rks = []
    for sl in slabs:
        a_mean = 0.5 * (a_ref[0, :, sl].astype(F32) + a_ref[1, :, sl].astype(F32))
        k_mean = k_ref[:, sl].astype(F32) * (1.0 + (a_mean - 1.0) * ka_ref[:, sl])
        rks.append(_bdot(r_ref[:, sl].astype(F32) * k_mean * rk_ref[:, sl], head_ones))
    h = None
    slabs_per_dot = MXU_COLS // LANES
    for s, (sl, dy, var, rk) in enumerate(zip(slabs, dys, vars_, rks)):
        yn = dy * lax.rsqrt(var + GN_EPS) * gnw_ref[:, sl] + gnb_ref[:, sl]
        v_ = v_ref[:, sl].astype(F32)
        if has_vres:
            v_ = v_ + (vf_ref[:, sl].astype(F32) - v_) * vg_ref[:, sl].astype(F32)
        act_ref[:, sl] = ((yn + rk * v_) * g_ref[:, sl].astype(F32)).astype(BF16)
        if (s + 1) % slabs_per_dot == 0:
            ks = slice((s + 1 - slabs_per_dot) * LANES, (s + 1) * LANES)
            part = jnp.dot(act_ref[:, ks], wo_ref[ks, :], preferred_element_type=F32)
            h = part if h is None else h + part
    o_ref[...] = x_ref[...] + _rms(h, gpost_ref[...])


def _rwkv_out(x, y, r, k, v, a, g, k_a, r_k, gn_w, gn_b, w_o, gpost, vres):
    T, D = x.shape
    tm = _tile(T, 256, 16)
    has_vres = vres is not None
    tok = pl.BlockSpec((tm, D), lambda i: (i, 0))
    tok2 = pl.BlockSpec((2, tm, D), lambda i: (0, i, 0))
    par = pl.BlockSpec((1, D), lambda i: (0, 0))
    args = [x, y, r, k, v, a, g, k_a, r_k, gn_w, gn_b, w_o, gpost]
    in_specs = [tok, tok2, tok, tok, tok, tok2, tok, par, par, par, par,
                pl.BlockSpec((D, D), lambda i: (0, 0)), par]
    if has_vres:
        args += list(vres)
        in_specs += [tok, tok]
    return pl.pallas_call(
        functools.partial(_rwkv_out_kernel, has_vres),
        grid=(T // tm,),
        in_specs=in_specs,
        out_specs=tok,
        out_shape=jax.ShapeDtypeStruct((T, D), F32),
        scratch_shapes=[pltpu.VMEM((tm, D), BF16)],
        compiler_params=_params(("parallel",)),
    )(*args)


def _qkv_kernel(tm, tn, deint, x_ref, gpre_ref, w_ref, cos_ref, sin_ref, o_ref, xn_ref, *acc):
    j = pl.program_id(1)

    @pl.when(j == 0)
    def _():
        def put(rows, y):
            xn_ref[rows, :] = y.astype(BF16)
        _rms_rows(x_ref, gpre_ref[...], put)

    cos = cos_ref[...]
    sin = sin_ref[...]
    xn = xn_ref[...]
    per_dot = MXU_COLS // ATT_HEAD
    for s in range(tn // ATT_HEAD):
        sl = slice(s * ATT_HEAD, (s + 1) * ATT_HEAD)
        if s % per_dot == 0:
            prod = jnp.dot(xn, w_ref[:, s * ATT_HEAD:s * ATT_HEAD + MXU_COLS],
                           preferred_element_type=F32)
        t = prod[:, (s % per_dot) * ATT_HEAD:(s % per_dot + 1) * ATT_HEAD]
        rot = t * cos + pltpu.roll(t, ATT_HEAD // 2, 1) * sin
        if deint == 1:
            o_ref[:, sl] = rot.astype(BF16)
        else:
            acc_ref, = acc
            slot = s % acc_ref.shape[0]
            acc_ref[slot] = rot
            for c in range(deint):
                o_ref[c, :, sl] = acc_ref[slot, pl.ds(c, tm // deint, stride=deint), :].astype(BF16)


def _rope_tables(pos):
    half = ROPE_DIM // 2
    inv = ROPE_THETA ** (-jnp.arange(half, dtype=F32) / half)
    ang = pos.astype(F32)[:, None] * inv[None, :]
    cos, sin = jnp.cos(ang), jnp.sin(ang)
    T = pos.shape[0]
    gap = ATT_HEAD // 2 - half
    ones = jnp.ones((T, gap), F32)
    zeros = jnp.zeros((T, gap), F32)
    cos_t = jnp.concatenate([cos, ones, cos, ones], axis=1)
    sin_t = jnp.concatenate([-sin, zeros, sin, zeros], axis=1)
    return (jnp.stack([cos_t, jnp.ones_like(cos_t)]), jnp.stack([sin_t, jnp.zeros_like(sin_t)]))


def _rope_perm(w_qkv, D):
    half = ROPE_DIM // 2
    mid = ATT_HEAD // 2
    w = w_qkv.reshape(w_qkv.shape[0], N_GROUPS, 3, D // ATT_HEAD, ATT_HEAD)
    qk = w[:, :, :2]
    qk = jnp.concatenate([qk[..., :half], qk[..., mid:mid + half], qk[..., 2 * half:mid],
                          qk[..., half:2 * half], qk[..., mid + half:]], axis=-1)
    return jnp.concatenate([qk, w[:, :, 2:]], axis=2).reshape(w_qkv.shape)


def _qkv(x, gpre, w, tables, gi, dil):
    T, D = x.shape
    tn = _tile(D, 2048, MXU_COLS)
    nj = 3 * D // tn

    def plane(j):
        return (j * tn) // (2 * D)

    tm = _tile(T, 1024, 16 * dil)
    tab = pl.BlockSpec((None, tm, ATT_HEAD), lambda i, j: (plane(j), i, 0))
    if dil == 1:
        out = pl.BlockSpec((None, tm, tn), lambda i, j: (0, i, j))
        deint_scratch = []
    else:
        out = pl.BlockSpec((dil, tm // dil, tn), lambda i, j: (0, i, j))
        deint_scratch = [pltpu.VMEM((2, tm, ATT_HEAD), F32)]
    return pl.pallas_call(
        functools.partial(_qkv_kernel, tm, tn, dil),
        grid=(T // tm, nj),
        in_specs=[
            pl.BlockSpec((tm, D), lambda i, j: (i, 0)),
            pl.BlockSpec((1, D), lambda i, j: (0, 0)),
            pl.BlockSpec((D, tn), lambda i, j: (0, gi * nj + j)),
            tab, tab,
        ],
        out_specs=out,
        out_shape=jax.ShapeDtypeStruct((dil, T // dil, 3 * D), BF16),
        scratch_shapes=[pltpu.VMEM((tm, D), BF16)] + deint_scratch,
        compiler_params=_params(("parallel", "arbitrary")),
    )(x, gpre, w, *tables)


def _attn_kernel(bq, n_heads, bounds, q_ref, kp_ref, kc_ref, kn_ref, vp_ref, vc_ref, vn_ref,
                 o_ref, lse_ref):
    row0 = pl.program_id(0) * bq
    lo = jnp.int32(bounds[0])
    hi = jnp.int32(bounds[-1])
    for b in bounds[1:-1]:
        lo = jnp.where(row0 >= b, b, lo)
    for b in reversed(bounds[1:-1]):
        hi = jnp.where(row0 < b, b, hi)
    nk = bq + 2 * ATT_HALF
    qrow = row0 + lax.broadcasted_iota(jnp.int32, (bq, nk), 0)
    krow = row0 - ATT_HALF + lax.broadcasted_iota(jnp.int32, (bq, nk), 1)
    valid = (jnp.abs(krow - qrow) <= ATT_HALF) & (krow >= lo) & (krow < hi)
    scale = ATT_HEAD ** -0.5
    lane = lax.broadcasted_iota(jnp.int32, (bq, LANES), 1)
    lse_all = jnp.zeros((bq, LANES), F32)
    for h in range(n_heads):
        sl = slice(h * ATT_HEAD, (h + 1) * ATT_HEAD)
        q = q_ref[:, sl]
        kcat = jnp.concatenate([kp_ref[:, sl], kc_ref[:, sl], kn_ref[:, sl]], axis=0)
        vcat = jnp.concatenate([vp_ref[:, sl], vc_ref[:, sl], vn_ref[:, sl]], axis=0)
        s = lax.dot_general(q, kcat, _NT, preferred_element_type=F32) * (scale * LOG2E)
        s = jnp.where(valid, s, -1e30)
        m = jnp.max(s, axis=-1, keepdims=True)
        p = jnp.exp2(s - m)
        l = jnp.sum(p, axis=-1, keepdims=True)
        o = jnp.dot(p.astype(BF16), vcat, preferred_element_type=F32) / l
        o_ref[:, sl] = o.astype(BF16)
        lse_all = jnp.where(lane == h, (m + jnp.log2(l)) * (1.0 / LOG2E), lse_all)
    lse_ref[...] = lse_all


def _attn_group(qkv, D, seq_starts, seq_ends):
    dil, rows, _ = qkv.shape
    bounds = tuple(s // dil for s in seq_starts) + (seq_ends[-1] // dil,)
    seg = [b - a for a, b in zip(bounds[:-1], bounds[1:])]
    bq = 128
    for s in seg:
        bq = math.gcd(bq, s)
    assert bq >= ATT_HALF, "sequence too short for the dilated band blocks"
    nb = rows // bq

    per = bq // ATT_HALF
    nh = rows // ATT_HALF

    def cur(kind):
        return pl.BlockSpec((None, bq, D), lambda i, c: (c, i, kind))

    def halo(kind, after):
        def imap(i, c):
            blk = (i + 1) * per if after else i * per - 1
            return (c, jnp.clip(blk, 0, nh - 1), kind)
        return pl.BlockSpec((None, ATT_HALF, D), imap)

    return pl.pallas_call(
        functools.partial(_attn_kernel, bq, D // ATT_HEAD, bounds),
        grid=(nb, dil),
        in_specs=[cur(0), halo(1, False), cur(1), halo(1, True),
                  halo(2, False), cur(2), halo(2, True)],
        out_specs=[pl.BlockSpec((None, bq, D), lambda i, c: (c, i, 0)),
                   pl.BlockSpec((None, bq, LANES), lambda i, c: (c, i, 0))],
        out_shape=[jax.ShapeDtypeStruct((dil, rows, D), BF16),
                   jax.ShapeDtypeStruct((dil, rows, LANES), F32)],
        compiler_params=_params(("parallel", "arbitrary")),
    )(qkv, qkv, qkv, qkv, qkv, qkv, qkv)


def _attn_out_kernel(tm, x_ref, o0_ref, o1_ref, o2_ref, l0_ref, l1_ref, l2_ref, wo_ref, gpost_ref,
                     out_ref, act_ref, *scratch):
    D = x_ref.shape[1]

    def token_major(ref, sl, s):
        dil = ref.shape[0]
        if dil == 1:
            return ref[0, :, sl].astype(F32)
        for c in range(dil):
            s[pl.ds(c, tm // dil, stride=dil), :] = ref[c, :, sl].astype(F32)
        return s[...]

    free = list(scratch)
    l0, l1, l2 = [token_major(r, slice(0, LANES), free.pop() if r.shape[0] > 1 else None)
                  for r in (l0_ref, l1_ref, l2_ref)]
    m = jnp.maximum(jnp.maximum(l0, l1), l2)
    e0, e1, e2 = jnp.exp(l0 - m), jnp.exp(l1 - m), jnp.exp(l2 - m)
    inv = 1.0 / (e0 + e1 + e2)
    w0, w1, w2 = e0 * inv, e1 * inv, e2 * inv
    h = None
    heads_per_dot = MXU_COLS // ATT_HEAD
    for k in range(D // ATT_HEAD):
        sl = slice(k * ATT_HEAD, (k + 1) * ATT_HEAD)
        pool = list(free)
        o0, o1, o2 = [token_major(r, sl, pool.pop() if r.shape[0] > 1 else None)
                      for r in (o0_ref, o1_ref, o2_ref)]
        act_ref[:, sl] = (w0[:, k:k + 1] * o0 + w1[:, k:k + 1] * o1
                          + w2[:, k:k + 1] * o2).astype(BF16)
        if (k + 1) % heads_per_dot == 0:
            ks = slice((k + 1 - heads_per_dot) * ATT_HEAD, (k + 1) * ATT_HEAD)
            part = jnp.dot(act_ref[:, ks], wo_ref[ks, :], preferred_element_type=F32)
            h = part if h is None else h + part
    out_ref[...] = x_ref[...] + _rms(h, gpost_ref[...])


def _attn_out(x, os_, lses, w_o, gpost):
    T, D = x.shape
    tm = _tile(T, 256, 16 * max(o.shape[0] for o in os_))
    tok = pl.BlockSpec((tm, D), lambda i: (i, 0))

    def planes(a):
        dil = a.shape[0]
        return pl.BlockSpec((dil, tm // dil, a.shape[2]), lambda i: (0, i, 0))

    n_scratch = sum(2 for o in os_ if o.shape[0] > 1)
    return pl.pallas_call(
        functools.partial(_attn_out_kernel, tm),
        grid=(T // tm,),
        in_specs=[tok] + [planes(a) for a in os_] + [planes(a) for a in lses]
        + [pl.BlockSpec((D, D), lambda i: (0, 0)), pl.BlockSpec((1, D), lambda i: (0, 0))],
        out_specs=tok,
        out_shape=jax.ShapeDtypeStruct((T, D), F32),
        scratch_shapes=[pltpu.VMEM((tm, D), BF16)] + [pltpu.VMEM((tm, LANES), F32)] * n_scratch,
        compiler_params=_params(("parallel",)),
    )(x, *os_, *lses, w_o, gpost)


def kernel(x_prompt, x_sample, norm_pre, norm_post, ffn_w_gate, ffn_w_up, ffn_w_down, rwkv_mu, rwkv_w_rkv, rwkv_w0, rwkv_w1, rwkv_w2, rwkv_a0, rwkv_a1, rwkv_a2, rwkv_v0, rwkv_v1, rwkv_v2, rwkv_g1, rwkv_g2, rwkv_k_k, rwkv_k_a, rwkv_r_k, rwkv_gn_w, rwkv_gn_b, rwkv_w_o, attn_w_qkv, attn_w_o):
    D = x_prompt.shape[-1]
    depth = norm_pre.shape[0]
    seqs = [x_prompt.shape[1]] * x_prompt.shape[0] + [x_sample.shape[1]] * x_sample.shape[0]
    seq_starts, seq_ends, t = [], [], 0
    for s in seqs:
        seq_starts.append(t)
        t += s
        seq_ends.append(t)
    seq_starts, seq_ends = tuple(seq_starts), tuple(seq_ends)
    x = jnp.concatenate([x_prompt.reshape(-1, D), x_sample.reshape(-1, D)], axis=0)
    pos = jnp.concatenate([jnp.arange(s, dtype=jnp.int32) for s in seqs])
    tables = _rope_tables(pos)

    def vec(a):
        return a.reshape(1, D)

    wg, wu, wd = ffn_w_gate.astype(BF16), ffn_w_up.astype(BF16), ffn_w_down.astype(BF16)
    w_rkv = rwkv_w_rkv.astype(BF16)
    v_first = None
    for layer in range(depth):
        x = _ffn(x, vec(norm_pre[layer, 0]), wg, wu, wd, vec(norm_post[layer, 0]), layer, 0)
        j = layer // 2
        if layer % 2 == 0:
            vres_w = None if j == 0 else (rwkv_v0[j - 1], rwkv_v1[j - 1], rwkv_v2[j - 1])
            pre = _rwkv_pre(x, seq_starts, seq_ends, vec(norm_pre[layer, 1]), rwkv_mu[j],
                            rwkv_w0[j], rwkv_w1[j], rwkv_w2[j], rwkv_a0[j], rwkv_a1[j], rwkv_a2[j],
                            rwkv_g1[j], rwkv_g2[j], vres_w)
            xr, xk, xv, ld, a, g = pre[:6]
            r = _mm(xr, w_rkv, (j, 0))
            k = _mm(xk, w_rkv, (j, 1))
            v = _mm(xv, w_rkv, (j, 2))
            if j == 0:
                v_first, vres = v, None
            else:
                vres = (v_first, pre[6])
            y = _wkv(r, k, v, ld, a, vec(rwkv_k_k[j]), vec(rwkv_k_a[j]), vres, seq_starts, seq_ends)
            x = _rwkv_out(x, y, r, k, v, a, g, vec(rwkv_k_a[j]), vec(rwkv_r_k[j]),
                          vec(rwkv_gn_w[j]), vec(rwkv_gn_b[j]), rwkv_w_o[j].astype(BF16),
                          vec(norm_post[layer, 1]), vres)
        else:
            w_qkv = _rope_perm(attn_w_qkv[j].astype(BF16), D)
            os_, lses = [], []
            for gi, (_, dil) in enumerate(DILATED_GROUPS):
                qkv = _qkv(x, vec(norm_pre[layer, 1]), w_qkv, tables, gi, dil)
                o, lse = _attn_group(qkv, D, seq_starts, seq_ends)
                os_.append(o)
                lses.append(lse)
            x = _attn_out(x, os_, lses, attn_w_o[j].astype(BF16), vec(norm_post[layer, 1]))
        x = _ffn(x, vec(norm_pre[layer, 2]), wg, wu, wd, vec(norm_post[layer, 2]), layer, 1)
    n_p = x_prompt.shape[0] * x_prompt.shape[1]
    return (x[:n_p].reshape(x_prompt.shape), x[n_p:].reshape(x_sample.shape))
```

```python
import functools
import math

import jax
import jax.numpy as jnp
from jax import lax
from jax.experimental import pallas as pl
from jax.experimental.pallas import tpu as pltpu

F32 = jnp.float32
BF16 = jnp.bfloat16

NORM_EPS = 1e-6
GN_EPS = 64e-5
RWKV_HEAD = 64
ATT_HEAD = 128
ROPE_DIM = ATT_HEAD // 4
ROPE_THETA = 500000.0
DILATED_GROUPS = ((128, 1), (512, 4), (2048, 16))
N_GROUPS = len(DILATED_GROUPS)
ATT_HALF = 64
LANES = 128
MXU_COLS = 256
LOG2E = math.log2(math.e)
WKV_CHUNK = 64
WKV_PAIRS_PER_STEP = 16
VMEM_LIMIT = 56 * 1024 * 1024
BF16_ROWS = 16

FFN_ROWS, FFN_COLS = 768, 512
MM_ROWS, MM_COLS = 1024, 1024
QKV_ROWS, QKV_COLS = 1024, 2048
RWKV_PRE_ROWS = 256
RWKV_OUT_ROWS = 256
ATTN_OUT_ROWS = 256
ATTN_QUERY_ROWS = 128

_NT = (((1,), (1,)), ((), ()))
_TN = (((0,), (0,)), ((), ()))


def _tile(n, pref, mult=8):
    if n <= pref:
        return n
    t = (pref // mult) * mult
    while t >= mult:
        if n % t == 0:
            return t
        t -= mult
    return n


def _in_list(v, lst):
    r = v == lst[0]
    for s in lst[1:]:
        r = r | (v == s)
    return r


def _rms(x, g):
    return x * lax.rsqrt(jnp.mean(x * x, axis=-1, keepdims=True) + NORM_EPS) * g


NORM_ROWS = 32


def _rms_rows(src_ref, g, emit):
    for r0 in range(0, src_ref.shape[0], NORM_ROWS):
        rows = slice(r0, min(r0 + NORM_ROWS, src_ref.shape[0]))
        emit(rows, _rms(src_ref[rows, :], g))


def _bdot(a, b):
    return jnp.dot(a.astype(BF16), b.astype(BF16), preferred_element_type=F32)


def _ntdot(a, b):
    return lax.dot_general(a, b, _NT, preferred_element_type=F32)


def _params(sem):
    return pltpu.CompilerParams(dimension_semantics=sem, vmem_limit_bytes=VMEM_LIMIT)


def _ffn_kernel(x_ref, gpre_ref, wg_ref, wu_ref, wd_ref, gpost_ref, o_ref, xn_ref, acc_ref):
    j = pl.program_id(1)

    @pl.when(j == 0)
    def _():
        def put(rows, y):
            xn_ref[rows, :] = y.astype(BF16)
        _rms_rows(x_ref, gpre_ref[...], put)
        acc_ref[...] = jnp.zeros_like(acc_ref)

    xn = xn_ref[...]
    g = jnp.dot(xn, wg_ref[...], preferred_element_type=F32)
    u = jnp.dot(xn, wu_ref[...], preferred_element_type=F32)
    h = (g * jax.nn.sigmoid(g)) * u
    acc_ref[...] += jnp.dot(h.astype(BF16), wd_ref[...], preferred_element_type=F32)

    @pl.when(j == pl.num_programs(1) - 1)
    def _():
        def put(rows, y):
            o_ref[rows, :] = x_ref[rows, :] + y
        _rms_rows(acc_ref, 0.5 * gpost_ref[...], put)


def _ffn(x, gpre, wg, wu, wd, gpost, layer, k):
    T, D = x.shape
    F = wg.shape[-1]
    tm = _tile(T, FFN_ROWS, MXU_COLS)
    tf = _tile(F, FFN_COLS, LANES)
    return pl.pallas_call(
        _ffn_kernel,
        grid=(T // tm, F // tf),
        in_specs=[
            pl.BlockSpec((tm, D), lambda i, j: (i, 0)),
            pl.BlockSpec((1, D), lambda i, j: (0, 0)),
            pl.BlockSpec((None, None, D, tf), lambda i, j: (layer, k, 0, j)),
            pl.BlockSpec((None, None, D, tf), lambda i, j: (layer, k, 0, j)),
            pl.BlockSpec((None, None, tf, D), lambda i, j: (layer, k, j, 0)),
            pl.BlockSpec((1, D), lambda i, j: (0, 0)),
        ],
        out_specs=pl.BlockSpec((tm, D), lambda i, j: (i, 0)),
        out_shape=jax.ShapeDtypeStruct((T, D), F32),
        scratch_shapes=[pltpu.VMEM((tm, D), BF16), pltpu.VMEM((tm, D), F32)],
        compiler_params=_params(("parallel", "arbitrary")),
    )(x, gpre, wg, wu, wd, gpost)


def _mm_kernel(a_ref, w_ref, o_ref):
    a = a_ref[...]
    for c in range(0, o_ref.shape[1], MXU_COLS):
        o_ref[:, c:c + MXU_COLS] = jnp.dot(
            a, w_ref[:, c:c + MXU_COLS], preferred_element_type=F32).astype(o_ref.dtype)


def _mm(a, w, lead):
    M, K = a.shape
    N = w.shape[-1]
    tm = _tile(M, MM_ROWS, BF16_ROWS)
    tn = _tile(N, MM_COLS, MXU_COLS)
    return pl.pallas_call(
        _mm_kernel,
        grid=(M // tm, N // tn),
        in_specs=[
            pl.BlockSpec((tm, K), lambda i, j: (i, 0)),
            pl.BlockSpec((None,) * len(lead) + (K, tn), lambda i, j: tuple(lead) + (0, j)),
        ],
        out_specs=pl.BlockSpec((tm, tn), lambda i, j: (i, j)),
        out_shape=jax.ShapeDtypeStruct((M, N), BF16),
        compiler_params=_params(("parallel", "arbitrary")),
    )(a, w)


def _rwkv_pre_kernel(seq_starts, seq_ends, has_vres, tm, *refs):
    (x_ref, xp_ref, xn_ref, gpre_ref, mu_ref, w0_ref, w1_ref, w2_ref,
     a0_ref, a1_ref, a2_ref, g1_ref, g2_ref) = refs[:13]
    refs = refs[13:]
    if has_vres:
        v0_ref, v1_ref, v2_ref = refs[:3]
        refs = refs[3:]
    xr_ref, xk_ref, xv_ref, ld_ref, a_ref, g_ref = refs[:6]

    t0 = pl.program_id(0) * tm
    gpre = gpre_ref[...]
    h = _rms(x_ref[...], gpre)
    keep_p = 1.0 - _in_list(t0, seq_starts).astype(F32)
    keep_n = 1.0 - _in_list(t0 + tm, seq_ends).astype(F32)
    hp = _rms(xp_ref[7:8, :], gpre) * keep_p
    hn = _rms(xn_ref[0:1, :], gpre) * keep_n
    row = lax.broadcasted_iota(jnp.int32, h.shape, 0)
    prev = jnp.where(row == 0, hp, pltpu.roll(h, 1, 0))
    nxt = jnp.where(row == tm - 1, hn, pltpu.roll(h, tm - 1, 0))
    xx = 0.5 * (prev + nxt) - h

    def mix(i):
        return (h + xx * mu_ref[i:i + 1, :]).astype(BF16)

    xr_ref[...] = mix(0)
    xw = mix(1)
    xk_ref[...] = mix(2)
    xv = mix(3)
    xv_ref[...] = xv
    xa = mix(4)
    xg = mix(5)
    for d in range(2):
        tw = jnp.tanh(jnp.dot(xw, w1_ref[d], preferred_element_type=F32))
        z = w0_ref[d:d + 1, :] + _bdot(tw, w2_ref[d])
        ld_ref[d] = (-math.exp(-0.5) * jax.nn.sigmoid(z)).astype(BF16)
        ta = jnp.dot(xa, a1_ref[d], preferred_element_type=F32)
        a_ref[d] = jax.nn.sigmoid(a0_ref[d:d + 1, :] + _bdot(ta, a2_ref[d])).astype(BF16)
    tg = jax.nn.sigmoid(jnp.dot(xg, g1_ref[...], preferred_element_type=F32))
    g_ref[...] = _bdot(tg, g2_ref[...]).astype(BF16)
    if has_vres:
        vg_ref = refs[6]
        tv = jnp.dot(xv, v1_ref[...], preferred_element_type=F32)
        vg_ref[...] = jax.nn.sigmoid(v0_ref[...] + _bdot(tv, v2_ref[...])).astype(BF16)


def _pad_lora(w_in, w_out):
    r = w_in.shape[-1]
    rp = -(-r // LANES) * LANES
    pin = [(0, 0)] * (w_in.ndim - 1) + [(0, rp - r)]
    pout = [(0, 0)] * (w_out.ndim - 2) + [(0, rp - r), (0, 0)]
    return jnp.pad(w_in, pin).astype(BF16), jnp.pad(w_out, pout).astype(BF16)


def _rwkv_pre(x, seq_starts, seq_ends, gpre, mu, w0, w1, w2, a0, a1, a2, g1, g2, vres):
    T, D = x.shape
    tm = _tile(T, RWKV_PRE_ROWS, BF16_ROWS)
    nb8 = T // 8
    has_vres = vres is not None
    w1p, w2p = _pad_lora(w1, w2)
    a1p, a2p = _pad_lora(a1, a2)
    g1p, g2p = _pad_lora(g1, g2)

    def full(a):
        nd = a.ndim
        return pl.BlockSpec(a.shape, lambda i, _nd=nd: (0,) * _nd)

    args = [x, x, x, gpre, mu, w0, w1p, w2p, a0, a1p, a2p, g1p, g2p]
    in_specs = [
        pl.BlockSpec((tm, D), lambda i: (i, 0)),
        pl.BlockSpec((8, D), lambda i: (jnp.maximum(i * (tm // 8) - 1, 0), 0)),
        pl.BlockSpec((8, D), lambda i: (jnp.minimum((i + 1) * (tm // 8), nb8 - 1), 0)),
    ] + [full(a) for a in args[3:]]
    if has_vres:
        v0, v1, v2 = vres
        v1p, v2p = _pad_lora(v1, v2)
        extra = [v0.reshape(1, D), v1p, v2p]
        args += extra
        in_specs += [full(a) for a in extra]
    tok = pl.BlockSpec((tm, D), lambda i: (i, 0))
    tok2 = pl.BlockSpec((2, tm, D), lambda i: (0, i, 0))
    out_specs = [tok, tok, tok, tok2, tok2, tok]
    out_shape = [jax.ShapeDtypeStruct((T, D), BF16)] * 3 + [
        jax.ShapeDtypeStruct((2, T, D), BF16), jax.ShapeDtypeStruct((2, T, D), BF16),
        jax.ShapeDtypeStruct((T, D), BF16)]
    if has_vres:
        out_specs.append(tok)
        out_shape.append(jax.ShapeDtypeStruct((T, D), BF16))
    return pl.pallas_call(
        functools.partial(_rwkv_pre_kernel, seq_starts, seq_ends, has_vres, tm),
        grid=(T // tm,),
        in_specs=in_specs,
        out_specs=out_specs,
        out_shape=out_shape,
        compiler_params=_params(("parallel",)),
    )(*args)


def _wkv_kernel(L, G, nc, resets_f, resets_r, has_vres, *refs):
    r_ref, k_ref, v_ref, ld_ref, a_ref, kk_ref, ka_ref = refs[:7]
    refs = refs[7:]
    if has_vres:
        vf_ref, vg_ref = refs[:2]
        refs = refs[2:]
    y_ref, s_ref = refs

    d = pl.program_id(0)
    c = pl.program_id(2)
    ci = c + d * (nc - 1 - 2 * c)
    reset = ((d == 0) & _in_list(ci, resets_f)) | ((d == 1) & _in_list(ci, resets_r))

    @pl.when(reset)
    def _():
        s_ref[...] = jnp.zeros_like(s_ref)

    sgn = 1 - 2 * d
    tt = lax.broadcasted_iota(jnp.int32, (L, LANES), 0)
    ss = lax.broadcasted_iota(jnp.int32, (L, LANES), 1) % L
    delta = sgn * (tt - ss)
    incl = delta >= 0
    strict = delta > 0
    eye = (tt == ss).astype(F32)
    tri = (sgn * (lax.broadcasted_iota(jnp.int32, (L, L), 0)
                  - lax.broadcasted_iota(jnp.int32, (L, L), 1)) >= 0).astype(BF16)
    first_head = lax.broadcasted_iota(jnp.int32, (L, LANES), 1) < RWKV_HEAD
    bi = lax.broadcasted_iota(jnp.int32, (LANES, LANES), 0) // RWKV_HEAD
    bj = lax.broadcasted_iota(jnp.int32, (LANES, LANES), 1) // RWKV_HEAD
    same_head = bi == bj
    head_ones = same_head.astype(BF16)
    n_double = int(math.log2(L)) - 1

    def bd(x):
        xb = x.astype(BF16)
        zero = jnp.zeros_like(xb)
        return jnp.concatenate([jnp.where(first_head, xb, zero), jnp.where(first_head, zero, xb)], axis=0)

    kkcs = [k_ref[:, p * LANES:(p + 1) * LANES].astype(F32) * kk_ref[:, p * LANES:(p + 1) * LANES]
            for p in range(G)]
    n2_all = jnp.dot(jnp.concatenate([(kkc * kkc).astype(BF16) for kkc in kkcs], axis=0), head_ones,
                     preferred_element_type=F32)
    pairs = []
    for p in range(G):
        sl = slice(p * LANES, (p + 1) * LANES)
        r_ = r_ref[:, sl].astype(F32)
        k_ = k_ref[:, sl].astype(F32)
        v_ = v_ref[:, sl].astype(F32)
        ld_b = ld_ref[:, sl]
        ld_ = ld_b.astype(F32)
        a_ = a_ref[:, sl].astype(F32)
        if has_vres:
            v_ = v_ + (vf_ref[:, sl].astype(F32) - v_) * vg_ref[:, sl].astype(F32)
        kkn = kkcs[p] / jnp.maximum(jnp.sqrt(n2_all[p * L:(p + 1) * L]), 1e-12)
        kd = k_ * (1.0 + (a_ - 1.0) * ka_ref[:, sl])
        cs = jnp.dot(tri, ld_b, preferred_element_type=F32)
        w_inv = jnp.exp(-cs)
        bt = kkn * a_ * w_inv
        kt = kd * w_inv
        pairs.append(dict(
            sl=sl, v=v_, bt=bt, kt=kt, st=s_ref[p],
            lhs=jnp.concatenate([-kkn * jnp.exp(cs - ld_), r_ * jnp.exp(cs)], axis=0).astype(BF16),
            wend=jnp.exp(jnp.sum(ld_, axis=0, keepdims=True))))

    for pr in pairs:
        rhs_t = jnp.concatenate([bd(pr["bt"]), bd(pr["kt"]), pr["st"].astype(BF16)], axis=0)
        prod = _ntdot(pr["lhs"], rhs_t)
        pr["a_ab"] = jnp.where(strict, prod[:L, :LANES], 0.0)
        pr["a_rb"] = jnp.where(incl, prod[L:, :LANES], 0.0)
        a_ak = jnp.where(strict, prod[:L, LANES:2 * LANES], 0.0)
        a_rk = jnp.where(incl, prod[L:, LANES:2 * LANES], 0.0)
        pr["sp"] = prod[:, 2 * LANES:]
        pr["a_k"] = jnp.concatenate([a_ak, a_rk], axis=0)
    for pr in pairs:
        pr["spv"] = pr["sp"] + _bdot(pr["a_k"], bd(pr["v"]))
    for pr in pairs:
        x = _bdot(jnp.concatenate([pr["a_ab"], pr["a_rb"]], axis=0), bd(pr["a_ab"]))
        pr["tinv"] = eye + pr["a_ab"]
        pr["pw"] = x[:L]
        pr["rt"] = pr["a_rb"] + x[L:]
    for lvl in range(1, n_double + 1):
        last = lvl == n_double
        for pr in pairs:
            rows = [pr["tinv"], pr["rt"]] if last else [pr["tinv"], pr["rt"], pr["pw"]]
            x = _bdot(jnp.concatenate(rows, axis=0), bd(pr["pw"]))
            pr["tinv"] = pr["tinv"] + x[:L]
            pr["rt"] = pr["rt"] + x[L:2 * L]
            if not last:
                pr["pw"] = x[2 * L:]
    for pr in pairs:
        z = _bdot(jnp.concatenate([pr["tinv"], pr["rt"]], axis=0), bd(pr["spv"][:L]))
        pr["u"] = z[:L]
        pr["y"] = pr["spv"][L:] + z[L:]
    for p, pr in enumerate(pairs):
        y_ref[:, pr["sl"]] = pr["y"].astype(BF16)
        uv = jnp.concatenate([pr["u"], pr["v"]], axis=0).astype(BF16)
        bk = jnp.concatenate([pr["bt"], pr["kt"]], axis=0).astype(BF16)
        upd = lax.dot_general(uv, bk, _TN, preferred_element_type=F32)
        s_ref[p] = (pr["st"] + jnp.where(same_head, upd, 0.0)) * pr["wend"]


def _wkv(r, k, v, ld, a, k_k, k_a, vres, seq_starts, seq_ends):
    T, D = r.shape
    L = WKV_CHUNK
    npairs = D // LANES
    assert 2 * L == LANES and RWKV_HEAD == L
    G = math.gcd(npairs, WKV_PAIRS_PER_STEP)
    nc = T // L
    W = G * LANES
    resets_f = tuple(s // L for s in seq_starts)
    resets_r = tuple(e // L - 1 for e in seq_ends)
    has_vres = vres is not None

    def cidx(d, c):
        return c + d * (nc - 1 - 2 * c)

    tok = pl.BlockSpec((L, W), lambda d, p, c: (cidx(d, c), p))
    tokd = pl.BlockSpec((None, L, W), lambda d, p, c: (d, cidx(d, c), p))
    par = pl.BlockSpec((1, W), lambda d, p, c: (0, p))
    args = [r, k, v, ld, a, k_k, k_a]
    in_specs = [tok, tok, tok, tokd, tokd, par, par]
    if has_vres:
        args += list(vres)
        in_specs += [tok, tok]
    return pl.pallas_call(
        functools.partial(_wkv_kernel, L, G, nc, resets_f, resets_r, has_vres),
        grid=(2, npairs // G, nc),
        in_specs=in_specs,
        out_specs=tokd,
        out_shape=jax.ShapeDtypeStruct((2, T, D), BF16),
        scratch_shapes=[pltpu.VMEM((G, LANES, LANES), F32)],
        compiler_params=_params(("arbitrary", "arbitrary", "arbitrary")),
    )(*args)


def _rwkv_out_kernel(has_vres, *refs):
    (x_ref, y_ref, r_ref, k_ref, v_ref, a_ref, g_ref, ka_ref, rk_ref, gnw_ref, gnb_ref,
     wo_ref, gpost_ref) = refs[:13]
    refs = refs[13:]
    if has_vres:
        vf_ref, vg_ref = refs[:2]
        refs = refs[2:]
    o_ref, act_ref = refs
    D = x_ref.shape[1]

    bi = lax.broadcasted_iota(jnp.int32, (LANES, LANES), 0) // RWKV_HEAD
    bj = lax.broadcasted_iota(jnp.int32, (LANES, LANES), 1) // RWKV_HEAD
    head_ones = (bi == bj).astype(BF16)
    inv_n = 1.0 / RWKV_HEAD

    slabs = [slice(s * LANES, (s + 1) * LANES) for s in range(D // LANES)]
    ys = [y_ref[0, :, sl].astype(F32) + y_ref[1, :, sl].astype(F32) for sl in slabs]
    means = [_bdot(y, head_ones) * inv_n for y in ys]
    dys = [y - mu for y, mu in zip(ys, means)]
    vars_ = [_bdot(dy * dy, head_ones) * inv_n for dy in dys]
    rks = []
    for sl in slabs:
        a_mean = 0.5 * (a_ref[0, :, sl].astype(F32) + a_ref[1, :, sl].astype(F32))
        k_mean = k_ref[:, sl].astype(F32) * (1.0 + (a_mean - 1.0) * ka_ref[:, sl])
        rks.append(_bdot(r_ref[:, sl].astype(F32) * k_mean * rk_ref[:, sl], head_ones))
    h = None
    slabs_per_dot = MXU_COLS // LANES
    for s, (sl, dy, var, rk) in enumerate(zip(slabs, dys, vars_, rks)):
        yn = dy * lax.rsqrt(var + GN_EPS) * gnw_ref[:, sl] + gnb_ref[:, sl]
        v_ = v_ref[:, sl].astype(F32)
        if has_vres:
            v_ = v_ + (vf_ref[:, sl].astype(F32) - v_) * vg_ref[:, sl].astype(F32)
        act_ref[:, sl] = ((yn + rk * v_) * g_ref[:, sl].astype(F32)).astype(BF16)
        if (s + 1) % slabs_per_dot == 0:
            ks = slice((s + 1 - slabs_per_dot) * LANES, (s + 1) * LANES)
            part = jnp.dot(act_ref[:, ks], wo_ref[ks, :], preferred_element_type=F32)
            h = part if h is None else h + part
    o_ref[...] = x_ref[...] + _rms(h, gpost_ref[...])


def _rwkv_out(x, y, r, k, v, a, g, k_a, r_k, gn_w, gn_b, w_o, gpost, vres):
    T, D = x.shape
    tm = _tile(T, RWKV_OUT_ROWS, BF16_ROWS)
    has_vres = vres is not None
    tok = pl.BlockSpec((tm, D), lambda i: (i, 0))
    tok2 = pl.BlockSpec((2, tm, D), lambda i: (0, i, 0))
    par = pl.BlockSpec((1, D), lambda i: (0, 0))
    args = [x, y, r, k, v, a, g, k_a, r_k, gn_w, gn_b, w_o, gpost]
    in_specs = [tok, tok2, tok, tok, tok, tok2, tok, par, par, par, par,
                pl.BlockSpec((D, D), lambda i: (0, 0)), par]
    if has_vres:
        args += list(vres)
        in_specs += [tok, tok]
    return pl.pallas_call(
        functools.partial(_rwkv_out_kernel, has_vres),
        grid=(T // tm,),
        in_specs=in_specs,
        out_specs=tok,
        out_shape=jax.ShapeDtypeStruct((T, D), F32),
        scratch_shapes=[pltpu.VMEM((tm, D), BF16)],
        compiler_params=_params(("parallel",)),
    )(*args)


def _qkv_kernel(tm, tn, deint, x_ref, gpre_ref, w_ref, cos_ref, sin_ref, o_ref, xn_ref, *acc):
    j = pl.program_id(1)

    @pl.when(j == 0)
    def _():
        def put(rows, y):
            xn_ref[rows, :] = y.astype(BF16)
        _rms_rows(x_ref, gpre_ref[...], put)

    cos = cos_ref[...]
    sin = sin_ref[...]
    xn = xn_ref[...]
    per_dot = MXU_COLS // ATT_HEAD
    for s in range(tn // ATT_HEAD):
        sl = slice(s * ATT_HEAD, (s + 1) * ATT_HEAD)
        if s % per_dot == 0:
            prod = jnp.dot(xn, w_ref[:, s * ATT_HEAD:s * ATT_HEAD + MXU_COLS],
                           preferred_element_type=F32)
        t = prod[:, (s % per_dot) * ATT_HEAD:(s % per_dot + 1) * ATT_HEAD]
        rot = t * cos + pltpu.roll(t, ATT_HEAD // 2, 1) * sin
        if deint == 1:
            o_ref[:, sl] = rot.astype(BF16)
        else:
            acc_ref, = acc
            slot = s % acc_ref.shape[0]
            acc_ref[slot] = rot
            for c in range(deint):
                o_ref[c, :, sl] = acc_ref[slot, pl.ds(c, tm // deint, stride=deint), :].astype(BF16)


def _rope_tables(pos):
    half = ROPE_DIM // 2
    inv = ROPE_THETA ** (-jnp.arange(half, dtype=F32) / half)
    ang = pos.astype(F32)[:, None] * inv[None, :]
    cos, sin = jnp.cos(ang), jnp.sin(ang)
    T = pos.shape[0]
    gap = ATT_HEAD // 2 - half
    ones = jnp.ones((T, gap), F32)
    zeros = jnp.zeros((T, gap), F32)
    cos_t = jnp.concatenate([cos, ones, cos, ones], axis=1)
    sin_t = jnp.concatenate([-sin, zeros, sin, zeros], axis=1)
    return (jnp.stack([cos_t, jnp.ones_like(cos_t)]), jnp.stack([sin_t, jnp.zeros_like(sin_t)]))


def _rope_perm(w_qkv, D):
    half = ROPE_DIM // 2
    mid = ATT_HEAD // 2
    idx = list(range(ATT_HEAD))
    idx[half:2 * half], idx[mid:mid + half] = idx[mid:mid + half], idx[half:2 * half]
    lead = w_qkv.shape[:-1]
    w = w_qkv.reshape(lead + (N_GROUPS, 3, D // ATT_HEAD, ATT_HEAD))
    qk = jnp.take(w[..., :2, :, :], jnp.asarray(idx, jnp.int32), axis=-1)
    return jnp.concatenate([qk, w[..., 2:, :, :]], axis=-3).reshape(w_qkv.shape)


def _qkv(x, gpre, w, layer, tables, gi, dil):
    T, D = x.shape
    tn = _tile(D, QKV_COLS, MXU_COLS)
    nj = 3 * D // tn

    def plane(j):
        return (j * tn) // (2 * D)

    tm = _tile(T, QKV_ROWS, BF16_ROWS * dil)
    tab = pl.BlockSpec((None, tm, ATT_HEAD), lambda i, j: (plane(j), i, 0))
    if dil == 1:
        out = pl.BlockSpec((None, tm, tn), lambda i, j: (0, i, j))
        deint_scratch = []
    else:
        out = pl.BlockSpec((dil, tm // dil, tn), lambda i, j: (0, i, j))
        deint_scratch = [pltpu.VMEM((2, tm, ATT_HEAD), F32)]
    return pl.pallas_call(
        functools.partial(_qkv_kernel, tm, tn, dil),
        grid=(T // tm, nj),
        in_specs=[
            pl.BlockSpec((tm, D), lambda i, j: (i, 0)),
            pl.BlockSpec((1, D), lambda i, j: (0, 0)),
            pl.BlockSpec((None, D, tn), lambda i, j: (layer, 0, gi * nj + j)),
            tab, tab,
        ],
        out_specs=out,
        out_shape=jax.ShapeDtypeStruct((dil, T // dil, 3 * D), BF16),
        scratch_shapes=[pltpu.VMEM((tm, D), BF16)] + deint_scratch,
        compiler_params=_params(("parallel", "arbitrary")),
    )(x, gpre, w, *tables)


def _attn_kernel(bq, n_heads, bounds, q_ref, kp_ref, kc_ref, kn_ref, vp_ref, vc_ref, vn_ref,
                 o_ref, lse_ref):
    row0 = pl.program_id(0) * bq
    lo = jnp.int32(bounds[0])
    hi = jnp.int32(bounds[-1])
    for b in bounds[1:-1]:
        lo = jnp.where(row0 >= b, b, lo)
    for b in reversed(bounds[1:-1]):
        hi = jnp.where(row0 < b, b, hi)
    nk = bq + 2 * ATT_HALF
    qrow = row0 + lax.broadcasted_iota(jnp.int32, (bq, nk), 0)
    krow = row0 - ATT_HALF + lax.broadcasted_iota(jnp.int32, (bq, nk), 1)
    valid = (jnp.abs(krow - qrow) <= ATT_HALF) & (krow >= lo) & (krow < hi)
    scale = ATT_HEAD ** -0.5
    lane = lax.broadcasted_iota(jnp.int32, (bq, LANES), 1)
    lse_all = jnp.zeros((bq, LANES), F32)
    for h in range(n_heads):
        sl = slice(h * ATT_HEAD, (h + 1) * ATT_HEAD)
        q = q_ref[:, sl]
        kcat = jnp.concatenate([kp_ref[:, sl], kc_ref[:, sl], kn_ref[:, sl]], axis=0)
        vcat = jnp.concatenate([vp_ref[:, sl], vc_ref[:, sl], vn_ref[:, sl]], axis=0)
        s = lax.dot_general(q, kcat, _NT, preferred_element_type=F32) * (scale * LOG2E)
        s = jnp.where(valid, s, -1e30)
        m = jnp.max(s, axis=-1, keepdims=True)
        p = jnp.exp2(s - m)
        l = jnp.sum(p, axis=-1, keepdims=True)
        o = jnp.dot(p.astype(BF16), vcat, preferred_element_type=F32) / l
        o_ref[:, sl] = o.astype(BF16)
        lse_all = jnp.where(lane == h, (m + jnp.log2(l)) * (1.0 / LOG2E), lse_all)
    lse_ref[...] = lse_all


def _attn_group(qkv, D, seq_starts, seq_ends):
    dil, rows, _ = qkv.shape
    bounds = tuple(s // dil for s in seq_starts) + (seq_ends[-1] // dil,)
    seg = [b - a for a, b in zip(bounds[:-1], bounds[1:])]
    bq = ATTN_QUERY_ROWS
    for s in seg:
        bq = math.gcd(bq, s)
    assert bq >= ATT_HALF, "sequence too short for the dilated band blocks"
    nb = rows // bq

    per = bq // ATT_HALF
    nh = rows // ATT_HALF

    def cur(kind):
        return pl.BlockSpec((None, bq, D), lambda i, c: (c, i, kind))

    def halo(kind, after):
        def imap(i, c):
            blk = (i + 1) * per if after else i * per - 1
            return (c, jnp.clip(blk, 0, nh - 1), kind)
        return pl.BlockSpec((None, ATT_HALF, D), imap)

    return pl.pallas_call(
        functools.partial(_attn_kernel, bq, D // ATT_HEAD, bounds),
        grid=(nb, dil),
        in_specs=[cur(0), halo(1, False), cur(1), halo(1, True),
                  halo(2, False), cur(2), halo(2, True)],
        out_specs=[pl.BlockSpec((None, bq, D), lambda i, c: (c, i, 0)),
                   pl.BlockSpec((None, bq, LANES), lambda i, c: (c, i, 0))],
        out_shape=[jax.ShapeDtypeStruct((dil, rows, D), BF16),
                   jax.ShapeDtypeStruct((dil, rows, LANES), F32)],
        compiler_params=_params(("parallel", "arbitrary")),
    )(qkv, qkv, qkv, qkv, qkv, qkv, qkv)


def _attn_out_kernel(tm, x_ref, o0_ref, o1_ref, o2_ref, l0_ref, l1_ref, l2_ref, wo_ref, gpost_ref,
                     out_ref, act_ref, *scratch):
    D = x_ref.shape[1]

    def token_major(ref, sl, s):
        dil = ref.shape[0]
        if dil == 1:
            return ref[0, :, sl].astype(F32)
        for c in range(dil):
            s[pl.ds(c, tm // dil, stride=dil), :] = ref[c, :, sl].astype(F32)
        return s[...]

    free = list(scratch)
    l0, l1, l2 = [token_major(r, slice(0, LANES), free.pop() if r.shape[0] > 1 else None)
                  for r in (l0_ref, l1_ref, l2_ref)]
    m = jnp.maximum(jnp.maximum(l0, l1), l2)
    e0, e1, e2 = jnp.exp(l0 - m), jnp.exp(l1 - m), jnp.exp(l2 - m)
    inv = 1.0 / (e0 + e1 + e2)
    w0, w1, w2 = e0 * inv, e1 * inv, e2 * inv
    h = None
    heads_per_dot = MXU_COLS // ATT_HEAD
    for k in range(D // ATT_HEAD):
        sl = slice(k * ATT_HEAD, (k + 1) * ATT_HEAD)
        pool = list(free)
        o0, o1, o2 = [token_major(r, sl, pool.pop() if r.shape[0] > 1 else None)
                      for r in (o0_ref, o1_ref, o2_ref)]
        act_ref[:, sl] = (w0[:, k:k + 1] * o0 + w1[:, k:k + 1] * o1
                          + w2[:, k:k + 1] * o2).astype(BF16)
        if (k + 1) % heads_per_dot == 0:
            ks = slice((k + 1 - heads_per_dot) * ATT_HEAD, (k + 1) * ATT_HEAD)
            part = jnp.dot(act_ref[:, ks], wo_ref[ks, :], preferred_element_type=F32)
            h = part if h is None else h + part
    out_ref[...] = x_ref[...] + _rms(h, gpost_ref[...])


def _attn_out(x, os_, lses, w_o, gpost):
    T, D = x.shape
    tm = _tile(T, ATTN_OUT_ROWS, BF16_ROWS * max(o.shape[0] for o in os_))
    tok = pl.BlockSpec((tm, D), lambda i: (i, 0))

    def planes(a):
        dil = a.shape[0]
        return pl.BlockSpec((dil, tm // dil, a.shape[2]), lambda i: (0, i, 0))

    n_scratch = sum(2 for o in os_ if o.shape[0] > 1)
    return pl.pallas_call(
        functools.partial(_attn_out_kernel, tm),
        grid=(T // tm,),
        in_specs=[tok] + [planes(a) for a in os_] + [planes(a) for a in lses]
        + [pl.BlockSpec((D, D), lambda i: (0, 0)), pl.BlockSpec((1, D), lambda i: (0, 0))],
        out_specs=tok,
        out_shape=jax.ShapeDtypeStruct((T, D), F32),
        scratch_shapes=[pltpu.VMEM((tm, D), BF16)] + [pltpu.VMEM((tm, LANES), F32)] * n_scratch,
        compiler_params=_params(("parallel",)),
    )(x, *os_, *lses, w_o, gpost)


def kernel(x_prompt, x_sample, norm_pre, norm_post, ffn_w_gate, ffn_w_up, ffn_w_down, rwkv_mu, rwkv_w_rkv, rwkv_w0, rwkv_w1, rwkv_w2, rwkv_a0, rwkv_a1, rwkv_a2, rwkv_v0, rwkv_v1, rwkv_v2, rwkv_g1, rwkv_g2, rwkv_k_k, rwkv_k_a, rwkv_r_k, rwkv_gn_w, rwkv_gn_b, rwkv_w_o, attn_w_qkv, attn_w_o):
    D = x_prompt.shape[-1]
    depth = norm_pre.shape[0]
    seqs = [x_prompt.shape[1]] * x_prompt.shape[0] + [x_sample.shape[1]] * x_sample.shape[0]
    seq_starts, seq_ends, t = [], [], 0
    for s in seqs:
        seq_starts.append(t)
        t += s
        seq_ends.append(t)
    seq_starts, seq_ends = tuple(seq_starts), tuple(seq_ends)
    x = jnp.concatenate([x_prompt.reshape(-1, D), x_sample.reshape(-1, D)], axis=0)
    pos = jnp.concatenate([jnp.arange(s, dtype=jnp.int32) for s in seqs])
    tables = _rope_tables(pos)

    def vec(a):
        return a.reshape(1, D)

    wg, wu, wd = ffn_w_gate.astype(BF16), ffn_w_up.astype(BF16), ffn_w_down.astype(BF16)
    w_rkv = rwkv_w_rkv.astype(BF16)
    w_qkv = _rope_perm(attn_w_qkv.astype(BF16), D)
    v_first = None
    for layer in range(depth):
        x = _ffn(x, vec(norm_pre[layer, 0]), wg, wu, wd, vec(norm_post[layer, 0]), layer, 0)
        j = layer // 2
        if layer % 2 == 0:
            vres_w = None if j == 0 else (rwkv_v0[j - 1], rwkv_v1[j - 1], rwkv_v2[j - 1])
            pre = _rwkv_pre(x, seq_starts, seq_ends, vec(norm_pre[layer, 1]), rwkv_mu[j],
                            rwkv_w0[j], rwkv_w1[j], rwkv_w2[j], rwkv_a0[j], rwkv_a1[j], rwkv_a2[j],
                            rwkv_g1[j], rwkv_g2[j], vres_w)
            xr, xk, xv, ld, a, g = pre[:6]
            r = _mm(xr, w_rkv, (j, 0))
            k = _mm(xk, w_rkv, (j, 1))
            v = _mm(xv, w_rkv, (j, 2))
            if j == 0:
                v_first, vres = v, None
            else:
                vres = (v_first, pre[6])
            y = _wkv(r, k, v, ld, a, vec(rwkv_k_k[j]), vec(rwkv_k_a[j]), vres, seq_starts, seq_ends)
            x = _rwkv_out(x, y, r, k, v, a, g, vec(rwkv_k_a[j]), vec(rwkv_r_k[j]),
                          vec(rwkv_gn_w[j]), vec(rwkv_gn_b[j]), rwkv_w_o[j].astype(BF16),
                          vec(norm_post[layer, 1]), vres)
        else:
            os_, lses = [], []
            for gi, (_, dil) in enumerate(DILATED_GROUPS):
                qkv = _qkv(x, vec(norm_pre[layer, 1]), w_qkv, j, tables, gi, dil)
                o, lse = _attn_group(qkv, D, seq_starts, seq_ends)
                os_.append(o)
                lses.append(lse)
            x = _attn_out(x, os_, lses, attn_w_o[j].astype(BF16), vec(norm_post[layer, 1]))
        x = _ffn(x, vec(norm_pre[layer, 2]), wg, wu, wd, vec(norm_post[layer, 2]), layer, 1)
    n_p = x_prompt.shape[0] * x_prompt.shape[1]
    return (x[:n_p].reshape(x_prompt.shape), x[n_p:].reshape(x_sample.shape))
```

```python
import functools
import math

import jax
import jax.numpy as jnp
from jax import lax
from jax.experimental import pallas as pl
from jax.experimental.pallas import tpu as pltpu

F32 = jnp.float32
BF16 = jnp.bfloat16

NORM_EPS = 1e-6
GN_EPS = 64e-5
RWKV_HEAD = 64
ATT_HEAD = 128
ROPE_DIM = ATT_HEAD // 4
ROPE_THETA = 500000.0
DILATED_GROUPS = ((128, 1), (512, 4), (2048, 16))
N_GROUPS = len(DILATED_GROUPS)
ATT_HALF = 64
LANES = 128
MXU_COLS = 256
LOG2E = math.log2(math.e)
WKV_CHUNK = 64
WKV_PAIRS_PER_STEP = 16
VMEM_LIMIT = 56 * 1024 * 1024
BF16_ROWS = 16

FFN_ROWS, FFN_COLS = 768, 512
MM_ROWS, MM_COLS = 1024, 1024
QKV_ROWS, QKV_COLS = 1024, 2048
RWKV_PRE_ROWS = 256
RWKV_OUT_ROWS = 256
ATTN_OUT_ROWS = 256
ATTN_QUERY_ROWS = 128

_NT = (((1,), (1,)), ((), ()))
_TN = (((0,), (0,)), ((), ()))


def _tile(n, pref, mult=8):
    if n <= pref:
        return n
    t = (pref // mult) * mult
    while t >= mult:
        if n % t == 0:
            return t
        t -= mult
    return n


def _in_list(v, lst):
    r = v == lst[0]
    for s in lst[1:]:
        r = r | (v == s)
    return r


def _rms(x, g):
    return x * lax.rsqrt(jnp.mean(x * x, axis=-1, keepdims=True) + NORM_EPS) * g


NORM_ROWS = 32


def _rms_rows(src_ref, g, emit):
    for r0 in range(0, src_ref.shape[0], NORM_ROWS):
        rows = slice(r0, min(r0 + NORM_ROWS, src_ref.shape[0]))
        emit(rows, _rms(src_ref[rows, :], g))


def _bdot(a, b):
    return jnp.dot(a.astype(BF16), b.astype(BF16), preferred_element_type=F32)


def _ntdot(a, b):
    return lax.dot_general(a, b, _NT, preferred_element_type=F32)


def _params(sem):
    return pltpu.CompilerParams(dimension_semantics=sem, vmem_limit_bytes=VMEM_LIMIT)


def _ffn_kernel(x_ref, gpre_ref, wg_ref, wu_ref, wd_ref, gpost_ref, o_ref, xn_ref, acc_ref):
    j = pl.program_id(1)

    @pl.when(j == 0)
    def _():
        def put(rows, y):
            xn_ref[rows, :] = y.astype(BF16)
        _rms_rows(x_ref, gpre_ref[...], put)
        acc_ref[...] = jnp.zeros_like(acc_ref)

    xn = xn_ref[...]
    g = jnp.dot(xn, wg_ref[...], preferred_element_type=F32)
    u = jnp.dot(xn, wu_ref[...], preferred_element_type=F32)
    h = (g * jax.nn.sigmoid(g)) * u
    acc_ref[...] += jnp.dot(h.astype(BF16), wd_ref[...], preferred_element_type=F32)

    @pl.when(j == pl.num_programs(1) - 1)
    def _():
        def put(rows, y):
            o_ref[rows, :] = x_ref[rows, :] + y
        _rms_rows(acc_ref, 0.5 * gpost_ref[...], put)


def _ffn(x, gpre, wg, wu, wd, gpost, layer, k):
    T, D = x.shape
    F = wg.shape[-1]
    tm = _tile(T, FFN_ROWS, MXU_COLS)
    tf = _tile(F, FFN_COLS, LANES)
    return pl.pallas_call(
        _ffn_kernel,
        grid=(T // tm, F // tf),
        in_specs=[
            pl.BlockSpec((tm, D), lambda i, j: (i, 0)),
            pl.BlockSpec((1, D), lambda i, j: (0, 0)),
            pl.BlockSpec((None, None, D, tf), lambda i, j: (layer, k, 0, j)),
            pl.BlockSpec((None, None, D, tf), lambda i, j: (layer, k, 0, j)),
            pl.BlockSpec((None, None, tf, D), lambda i, j: (layer, k, j, 0)),
            pl.BlockSpec((1, D), lambda i, j: (0, 0)),
        ],
        out_specs=pl.BlockSpec((tm, D), lambda i, j: (i, 0)),
        out_shape=jax.ShapeDtypeStruct((T, D), F32),
        scratch_shapes=[pltpu.VMEM((tm, D), BF16), pltpu.VMEM((tm, D), F32)],
        compiler_params=_params(("parallel", "arbitrary")),
    )(x, gpre, wg, wu, wd, gpost)


def _mm_kernel(a_ref, w_ref, o_ref):
    a = a_ref[...]
    for c in range(0, o_ref.shape[1], MXU_COLS):
        o_ref[:, c:c + MXU_COLS] = jnp.dot(
            a, w_ref[:, c:c + MXU_COLS], preferred_element_type=F32).astype(o_ref.dtype)


def _mm(a, w, lead):
    M, K = a.shape
    N = w.shape[-1]
    tm = _tile(M, MM_ROWS, BF16_ROWS)
    tn = _tile(N, MM_COLS, MXU_COLS)
    return pl.pallas_call(
        _mm_kernel,
        grid=(M // tm, N // tn),
        in_specs=[
            pl.BlockSpec((tm, K), lambda i, j: (i, 0)),
            pl.BlockSpec((None,) * len(lead) + (K, tn), lambda i, j: tuple(lead) + (0, j)),
        ],
        out_specs=pl.BlockSpec((tm, tn), lambda i, j: (i, j)),
        out_shape=jax.ShapeDtypeStruct((M, N), BF16),
        compiler_params=_params(("parallel", "arbitrary")),
    )(a, w)


def _rwkv_pre_kernel(seq_starts, seq_ends, has_vres, tm, *refs):
    (x_ref, xp_ref, xn_ref, gpre_ref, mu_ref, w0_ref, w1_ref, w2_ref,
     a0_ref, a1_ref, a2_ref, g1_ref, g2_ref) = refs[:13]
    refs = refs[13:]
    if has_vres:
        v0_ref, v1_ref, v2_ref = refs[:3]
        refs = refs[3:]
    xr_ref, xk_ref, xv_ref, ld_ref, a_ref, g_ref = refs[:6]

    t0 = pl.program_id(0) * tm
    gpre = gpre_ref[...]
    h = _rms(x_ref[...], gpre)
    keep_p = 1.0 - _in_list(t0, seq_starts).astype(F32)
    keep_n = 1.0 - _in_list(t0 + tm, seq_ends).astype(F32)
    hp = _rms(xp_ref[7:8, :], gpre) * keep_p
    hn = _rms(xn_ref[0:1, :], gpre) * keep_n
    row = lax.broadcasted_iota(jnp.int32, h.shape, 0)
    prev = jnp.where(row == 0, hp, pltpu.roll(h, 1, 0))
    nxt = jnp.where(row == tm - 1, hn, pltpu.roll(h, tm - 1, 0))
    xx = 0.5 * (prev + nxt) - h

    def mix(i):
        return (h + xx * mu_ref[i:i + 1, :]).astype(BF16)

    xr_ref[...] = mix(0)
    xw = mix(1)
    xk_ref[...] = mix(2)
    xv = mix(3)
    xv_ref[...] = xv
    xa = mix(4)
    xg = mix(5)
    for d in range(2):
        tw = jnp.tanh(jnp.dot(xw, w1_ref[d], preferred_element_type=F32))
        z = w0_ref[d:d + 1, :] + _bdot(tw, w2_ref[d])
        ld_ref[d] = (-math.exp(-0.5) * jax.nn.sigmoid(z)).astype(BF16)
        ta = jnp.dot(xa, a1_ref[d], preferred_element_type=F32)
        a_ref[d] = jax.nn.sigmoid(a0_ref[d:d + 1, :] + _bdot(ta, a2_ref[d])).astype(BF16)
    tg = jax.nn.sigmoid(jnp.dot(xg, g1_ref[...], preferred_element_type=F32))
    g_ref[...] = _bdot(tg, g2_ref[...]).astype(BF16)
    if has_vres:
        vg_ref = refs[6]
        tv = jnp.dot(xv, v1_ref[...], preferred_element_type=F32)
        vg_ref[...] = jax.nn.sigmoid(v0_ref[...] + _bdot(tv, v2_ref[...])).astype(BF16)


def _pad_lora(w_in, w_out):
    r = w_in.shape[-1]
    rp = -(-r // LANES) * LANES
    pin = [(0, 0)] * (w_in.ndim - 1) + [(0, rp - r)]
    pout = [(0, 0)] * (w_out.ndim - 2) + [(0, rp - r), (0, 0)]
    return jnp.pad(w_in, pin).astype(BF16), jnp.pad(w_out, pout).astype(BF16)


def _rwkv_pre(x, seq_starts, seq_ends, gpre, mu, w0, w1, w2, a0, a1, a2, g1, g2, vres):
    T, D = x.shape
    tm = _tile(T, RWKV_PRE_ROWS, BF16_ROWS)
    nb8 = T // 8
    has_vres = vres is not None
    w1p, w2p = _pad_lora(w1, w2)
    a1p, a2p = _pad_lora(a1, a2)
    g1p, g2p = _pad_lora(g1, g2)

    def full(a):
        nd = a.ndim
        return pl.BlockSpec(a.shape, lambda i, _nd=nd: (0,) * _nd)

    args = [x, x, x, gpre, mu, w0, w1p, w2p, a0, a1p, a2p, g1p, g2p]
    in_specs = [
        pl.BlockSpec((tm, D), lambda i: (i, 0)),
        pl.BlockSpec((8, D), lambda i: (jnp.maximum(i * (tm // 8) - 1, 0), 0)),
        pl.BlockSpec((8, D), lambda i: (jnp.minimum((i + 1) * (tm // 8), nb8 - 1), 0)),
    ] + [full(a) for a in args[3:]]
    if has_vres:
        v0, v1, v2 = vres
        v1p, v2p = _pad_lora(v1, v2)
        extra = [v0.reshape(1, D), v1p, v2p]
        args += extra
        in_specs += [full(a) for a in extra]
    tok = pl.BlockSpec((tm, D), lambda i: (i, 0))
    tok2 = pl.BlockSpec((2, tm, D), lambda i: (0, i, 0))
    out_specs = [tok, tok, tok, tok2, tok2, tok]
    out_shape = [jax.ShapeDtypeStruct((T, D), BF16)] * 3 + [
        jax.ShapeDtypeStruct((2, T, D), BF16), jax.ShapeDtypeStruct((2, T, D), BF16),
        jax.ShapeDtypeStruct((T, D), BF16)]
    if has_vres:
        out_specs.append(tok)
        out_shape.append(jax.ShapeDtypeStruct((T, D), BF16))
    return pl.pallas_call(
        functools.partial(_rwkv_pre_kernel, seq_starts, seq_ends, has_vres, tm),
        grid=(T // tm,),
        in_specs=in_specs,
        out_specs=out_specs,
        out_shape=out_shape,
        compiler_params=_params(("parallel",)),
    )(*args)


def _wkv_kernel(L, G, nc, resets_f, resets_r, has_vres, *refs):
    r_ref, k_ref, v_ref, ld_ref, a_ref, kk_ref, ka_ref = refs[:7]
    refs = refs[7:]
    if has_vres:
        vf_ref, vg_ref = refs[:2]
        refs = refs[2:]
    y_ref, s_ref = refs

    d = pl.program_id(0)
    c = pl.program_id(2)
    ci = c + d * (nc - 1 - 2 * c)
    reset = ((d == 0) & _in_list(ci, resets_f)) | ((d == 1) & _in_list(ci, resets_r))

    @pl.when(reset)
    def _():
        s_ref[...] = jnp.zeros_like(s_ref)

    sgn = 1 - 2 * d
    tt = lax.broadcasted_iota(jnp.int32, (L, LANES), 0)
    ss = lax.broadcasted_iota(jnp.int32, (L, LANES), 1) % L
    delta = sgn * (tt - ss)
    incl = delta >= 0
    strict = delta > 0
    eye = (tt == ss).astype(F32)
    tri = (sgn * (lax.broadcasted_iota(jnp.int32, (L, L), 0)
                  - lax.broadcasted_iota(jnp.int32, (L, L), 1)) >= 0).astype(BF16)
    first_head = lax.broadcasted_iota(jnp.int32, (L, LANES), 1) < RWKV_HEAD
    bi = lax.broadcasted_iota(jnp.int32, (LANES, LANES), 0) // RWKV_HEAD
    bj = lax.broadcasted_iota(jnp.int32, (LANES, LANES), 1) // RWKV_HEAD
    same_head = bi == bj
    head_ones = same_head.astype(BF16)
    n_double = int(math.log2(L)) - 1

    def bd(x):
        xb = x.astype(BF16)
        zero = jnp.zeros_like(xb)
        return jnp.concatenate([jnp.where(first_head, xb, zero), jnp.where(first_head, zero, xb)], axis=0)

    kkcs = [k_ref[:, p * LANES:(p + 1) * LANES].astype(F32) * kk_ref[:, p * LANES:(p + 1) * LANES]
            for p in range(G)]
    n2_all = jnp.dot(jnp.concatenate([(kkc * kkc).astype(BF16) for kkc in kkcs], axis=0), head_ones,
                     preferred_element_type=F32)
    pairs = []
    for p in range(G):
        sl = slice(p * LANES, (p + 1) * LANES)
        r_ = r_ref[:, sl].astype(F32)
        k_ = k_ref[:, sl].astype(F32)
        v_ = v_ref[:, sl].astype(F32)
        ld_b = ld_ref[:, sl]
        ld_ = ld_b.astype(F32)
        a_ = a_ref[:, sl].astype(F32)
        if has_vres:
            v_ = v_ + (vf_ref[:, sl].astype(F32) - v_) * vg_ref[:, sl].astype(F32)
        kkn = kkcs[p] / jnp.maximum(jnp.sqrt(n2_all[p * L:(p + 1) * L]), 1e-12)
        kd = k_ * (1.0 + (a_ - 1.0) * ka_ref[:, sl])
        cs = jnp.dot(tri, ld_b, preferred_element_type=F32)
        w_inv = jnp.exp(-cs)
        bt = kkn * a_ * w_inv
        kt = kd * w_inv
        pairs.append(dict(
            sl=sl, v=v_, bt=bt, kt=kt, st=s_ref[p],
            lhs=jnp.concatenate([-kkn * jnp.exp(cs - ld_), r_ * jnp.exp(cs)], axis=0).astype(BF16),
            wend=jnp.exp(jnp.sum(ld_, axis=0, keepdims=True))))

    for pr in pairs:
        rhs_t = jnp.concatenate([bd(pr["bt"]), bd(pr["kt"]), pr["st"].astype(BF16)], axis=0)
        prod = _ntdot(pr["lhs"], rhs_t)
        pr["a_ab"] = jnp.where(strict, prod[:L, :LANES], 0.0)
        pr["a_rb"] = jnp.where(incl, prod[L:, :LANES], 0.0)
        a_ak = jnp.where(strict, prod[:L, LANES:2 * LANES], 0.0)
        a_rk = jnp.where(incl, prod[L:, LANES:2 * LANES], 0.0)
        pr["sp"] = prod[:, 2 * LANES:]
        pr["a_k"] = jnp.concatenate([a_ak, a_rk], axis=0)
    for pr in pairs:
        pr["spv"] = pr["sp"] + _bdot(pr["a_k"], bd(pr["v"]))
    for pr in pairs:
        x = _bdot(jnp.concatenate([pr["a_ab"], pr["a_rb"]], axis=0), bd(pr["a_ab"]))
        pr["tinv"] = eye + pr["a_ab"]
        pr["pw"] = x[:L]
        pr["rt"] = pr["a_rb"] + x[L:]
    for lvl in range(1, n_double + 1):
        last = lvl == n_double
        for pr in pairs:
            rows = [pr["tinv"], pr["rt"]] if last else [pr["tinv"], pr["rt"], pr["pw"]]
            x = _bdot(jnp.concatenate(rows, axis=0), bd(pr["pw"]))
            pr["tinv"] = pr["tinv"] + x[:L]
            pr["rt"] = pr["rt"] + x[L:2 * L]
            if not last:
                pr["pw"] = x[2 * L:]
    for pr in pairs:
        z = _bdot(jnp.concatenate([pr["tinv"], pr["rt"]], axis=0), bd(pr["spv"][:L]))
        pr["u"] = z[:L]
        pr["y"] = pr["spv"][L:] + z[L:]
    for p, pr in enumerate(pairs):
        y_ref[:, pr["sl"]] = pr["y"].astype(BF16)
        uv = jnp.concatenate([pr["u"], pr["v"]], axis=0).astype(BF16)
        bk = jnp.concatenate([pr["bt"], pr["kt"]], axis=0).astype(BF16)
        upd = lax.dot_general(uv, bk, _TN, preferred_element_type=F32)
        s_ref[p] = (pr["st"] + jnp.where(same_head, upd, 0.0)) * pr["wend"]


def _wkv(r, k, v, ld, a, k_k, k_a, vres, seq_starts, seq_ends):
    T, D = r.shape
    L = WKV_CHUNK
    npairs = D // LANES
    assert 2 * L == LANES and RWKV_HEAD == L
    G = math.gcd(npairs, WKV_PAIRS_PER_STEP)
    nc = T // L
    W = G * LANES
    resets_f = tuple(s // L for s in seq_starts)
    resets_r = tuple(e // L - 1 for e in seq_ends)
    has_vres = vres is not None

    def cidx(d, c):
        return c + d * (nc - 1 - 2 * c)

    tok = pl.BlockSpec((L, W), lambda d, p, c: (cidx(d, c), p))
    tokd = pl.BlockSpec((None, L, W), lambda d, p, c: (d, cidx(d, c), p))
    par = pl.BlockSpec((1, W), lambda d, p, c: (0, p))
    args = [r, k, v, ld, a, k_k, k_a]
    in_specs = [tok, tok, tok, tokd, tokd, par, par]
    if has_vres:
        args += list(vres)
        in_specs += [tok, tok]
    return pl.pallas_call(
        functools.partial(_wkv_kernel, L, G, nc, resets_f, resets_r, has_vres),
        grid=(2, npairs // G, nc),
        in_specs=in_specs,
        out_specs=tokd,
        out_shape=jax.ShapeDtypeStruct((2, T, D), BF16),
        scratch_shapes=[pltpu.VMEM((G, LANES, LANES), F32)],
        compiler_params=_params(("arbitrary", "arbitrary", "arbitrary")),
    )(*args)


def _rwkv_out_kernel(has_vres, *refs):
    (x_ref, y_ref, r_ref, k_ref, v_ref, a_ref, g_ref, ka_ref, rk_ref, gnw_ref, gnb_ref,
     wo_ref, gpost_ref) = refs[:13]
    refs = refs[13:]
    if has_vres:
        vf_ref, vg_ref = refs[:2]
        refs = refs[2:]
    o_ref, act_ref = refs
    D = x_ref.shape[1]

    bi = lax.broadcasted_iota(jnp.int32, (LANES, LANES), 0) // RWKV_HEAD
    bj = lax.broadcasted_iota(jnp.int32, (LANES, LANES), 1) // RWKV_HEAD
    head_ones = (bi == bj).astype(BF16)
    inv_n = 1.0 / RWKV_HEAD

    slabs = [slice(s * LANES, (s + 1) * LANES) for s in range(D // LANES)]
    ys = [y_ref[0, :, sl].astype(F32) + y_ref[1, :, sl].astype(F32) for sl in slabs]
    means = [_bdot(y, head_ones) * inv_n for y in ys]
    dys = [y - mu for y, mu in zip(ys, means)]
    vars_ = [_bdot(dy * dy, head_ones) * inv_n for dy in dys]
    rks = []
    for sl in slabs:
        a_mean = 0.5 * (a_ref[0, :, sl].astype(F32) + a_ref[1, :, sl].astype(F32))
        k_mean = k_ref[:, sl].astype(F32) * (1.0 + (a_mean - 1.0) * ka_ref[:, sl])
        rks.append(_bdot(r_ref[:, sl].astype(F32) * k_mean * rk_ref[:, sl], head_ones))
    h = None
    slabs_per_dot = MXU_COLS // LANES
    for s, (sl, dy, var, rk) in enumerate(zip(slabs, dys, vars_, rks)):
        yn = dy * lax.rsqrt(var + GN_EPS) * gnw_ref[:, sl] + gnb_ref[:, sl]
        v_ = v_ref[:, sl].astype(F32)
        if has_vres:
            v_ = v_ + (vf_ref[:, sl].astype(F32) - v_) * vg_ref[:, sl].astype(F32)
        act_ref[:, sl] = ((yn + rk * v_) * g_ref[:, sl].astype(F32)).astype(BF16)
        if (s + 1) % slabs_per_dot == 0:
            ks = slice((s + 1 - slabs_per_dot) * LANES, (s + 1) * LANES)
            part = jnp.dot(act_ref[:, ks], wo_ref[ks, :], preferred_element_type=F32)
            h = part if h is None else h + part
    o_ref[...] = x_ref[...] + _rms(h, gpost_ref[...])


def _rwkv_out(x, y, r, k, v, a, g, k_a, r_k, gn_w, gn_b, w_o, gpost, vres):
    T, D = x.shape
    tm = _tile(T, RWKV_OUT_ROWS, BF16_ROWS)
    has_vres = vres is not None
    tok = pl.BlockSpec((tm, D), lambda i: (i, 0))
    tok2 = pl.BlockSpec((2, tm, D), lambda i: (0, i, 0))
    par = pl.BlockSpec((1, D), lambda i: (0, 0))
    args = [x, y, r, k, v, a, g, k_a, r_k, gn_w, gn_b, w_o, gpost]
    in_specs = [tok, tok2, tok, tok, tok, tok2, tok, par, par, par, par,
                pl.BlockSpec((D, D), lambda i: (0, 0)), par]
    if has_vres:
        args += list(vres)
        in_specs += [tok, tok]
    return pl.pallas_call(
        functools.partial(_rwkv_out_kernel, has_vres),
        grid=(T // tm,),
        in_specs=in_specs,
        out_specs=tok,
        out_shape=jax.ShapeDtypeStruct((T, D), F32),
        scratch_shapes=[pltpu.VMEM((tm, D), BF16)],
        compiler_params=_params(("parallel",)),
    )(*args)


def _qkv_kernel(tm, tn, deint, x_ref, gpre_ref, w_ref, cos_ref, sin_ref, o_ref, xn_ref, *acc):
    j = pl.program_id(1)

    @pl.when(j == 0)
    def _():
        def put(rows, y):
            xn_ref[rows, :] = y.astype(BF16)
        _rms_rows(x_ref, gpre_ref[...], put)

    cos = cos_ref[...]
    sin = sin_ref[...]
    xn = xn_ref[...]
    per_dot = MXU_COLS // ATT_HEAD
    for s in range(tn // ATT_HEAD):
        sl = slice(s * ATT_HEAD, (s + 1) * ATT_HEAD)
        if s % per_dot == 0:
            prod = jnp.dot(xn, w_ref[:, s * ATT_HEAD:s * ATT_HEAD + MXU_COLS],
                           preferred_element_type=F32)
        t = prod[:, (s % per_dot) * ATT_HEAD:(s % per_dot + 1) * ATT_HEAD]
        rot = t * cos + pltpu.roll(t, ATT_HEAD // 2, 1) * sin
        if deint == 1:
            o_ref[:, sl] = rot.astype(BF16)
        else:
            acc_ref, = acc
            slot = s % acc_ref.shape[0]
            acc_ref[slot] = rot
            for c in range(deint):
                o_ref[c, :, sl] = acc_ref[slot, pl.ds(c, tm // deint, stride=deint), :].astype(BF16)


def _rope_tables(pos):
    half = ROPE_DIM // 2
    inv = ROPE_THETA ** (-jnp.arange(half, dtype=F32) / half)
    ang = pos.astype(F32)[:, None] * inv[None, :]
    cos, sin = jnp.cos(ang), jnp.sin(ang)
    T = pos.shape[0]
    gap = ATT_HEAD // 2 - half
    ones = jnp.ones((T, gap), F32)
    zeros = jnp.zeros((T, gap), F32)
    cos_t = jnp.concatenate([cos, ones, cos, ones], axis=1)
    sin_t = jnp.concatenate([-sin, zeros, sin, zeros], axis=1)
    return (jnp.stack([cos_t, jnp.ones_like(cos_t)]), jnp.stack([sin_t, jnp.zeros_like(sin_t)]))


def _rope_perm(w_qkv, D):
    half = ROPE_DIM // 2
    mid = ATT_HEAD // 2
    head = list(range(ATT_HEAD))
    head[half:2 * half], head[mid:mid + half] = head[mid:mid + half], head[half:2 * half]
    cols = []
    for c0 in range(0, w_qkv.shape[-1], ATT_HEAD):
        is_v = (c0 // D) % 3 == 2
        cols += [c0 + (ch if is_v else head[ch]) for ch in range(ATT_HEAD)]
    return jnp.take(w_qkv, jnp.asarray(cols, jnp.int32), axis=-1)


def _qkv(x, gpre, w, layer, tables, gi, dil):
    T, D = x.shape
    tn = _tile(D, QKV_COLS, MXU_COLS)
    nj = 3 * D // tn

    def plane(j):
        return (j * tn) // (2 * D)

    tm = _tile(T, QKV_ROWS, BF16_ROWS * dil)
    tab = pl.BlockSpec((None, tm, ATT_HEAD), lambda i, j: (plane(j), i, 0))
    if dil == 1:
        out = pl.BlockSpec((None, tm, tn), lambda i, j: (0, i, j))
        deint_scratch = []
    else:
        out = pl.BlockSpec((dil, tm // dil, tn), lambda i, j: (0, i, j))
        deint_scratch = [pltpu.VMEM((2, tm, ATT_HEAD), F32)]
    return pl.pallas_call(
        functools.partial(_qkv_kernel, tm, tn, dil),
        grid=(T // tm, nj),
        in_specs=[
            pl.BlockSpec((tm, D), lambda i, j: (i, 0)),
            pl.BlockSpec((1, D), lambda i, j: (0, 0)),
            pl.BlockSpec((None, D, tn), lambda i, j: (layer, 0, gi * nj + j)),
            tab, tab,
        ],
        out_specs=out,
        out_shape=jax.ShapeDtypeStruct((dil, T // dil, 3 * D), BF16),
        scratch_shapes=[pltpu.VMEM((tm, D), BF16)] + deint_scratch,
        compiler_params=_params(("parallel", "arbitrary")),
    )(x, gpre, w, *tables)


def _attn_kernel(bq, n_heads, bounds, q_ref, kp_ref, kc_ref, kn_ref, vp_ref, vc_ref, vn_ref,
                 o_ref, lse_ref):
    row0 = pl.program_id(0) * bq
    lo = jnp.int32(bounds[0])
    hi = jnp.int32(bounds[-1])
    for b in bounds[1:-1]:
        lo = jnp.where(row0 >= b, b, lo)
    for b in reversed(bounds[1:-1]):
        hi = jnp.where(row0 < b, b, hi)
    nk = bq + 2 * ATT_HALF
    qrow = row0 + lax.broadcasted_iota(jnp.int32, (bq, nk), 0)
    krow = row0 - ATT_HALF + lax.broadcasted_iota(jnp.int32, (bq, nk), 1)
    valid = (jnp.abs(krow - qrow) <= ATT_HALF) & (krow >= lo) & (krow < hi)
    scale = ATT_HEAD ** -0.5
    lane = lax.broadcasted_iota(jnp.int32, (bq, LANES), 1)
    lse_all = jnp.zeros((bq, LANES), F32)
    for h in range(n_heads):
        sl = slice(h * ATT_HEAD, (h + 1) * ATT_HEAD)
        q = q_ref[:, sl]
        kcat = jnp.concatenate([kp_ref[:, sl], kc_ref[:, sl], kn_ref[:, sl]], axis=0)
        vcat = jnp.concatenate([vp_ref[:, sl], vc_ref[:, sl], vn_ref[:, sl]], axis=0)
        s = lax.dot_general(q, kcat, _NT, preferred_element_type=F32) * (scale * LOG2E)
        s = jnp.where(valid, s, -1e30)
        m = jnp.max(s, axis=-1, keepdims=True)
        p = jnp.exp2(s - m)
        l = jnp.sum(p, axis=-1, keepdims=True)
        o = jnp.dot(p.astype(BF16), vcat, preferred_element_type=F32) / l
        o_ref[:, sl] = o.astype(BF16)
        lse_all = jnp.where(lane == h, (m + jnp.log2(l)) * (1.0 / LOG2E), lse_all)
    lse_ref[...] = lse_all


def _attn_group(qkv, D, seq_starts, seq_ends):
    dil, rows, _ = qkv.shape
    bounds = tuple(s // dil for s in seq_starts) + (seq_ends[-1] // dil,)
    seg = [b - a for a, b in zip(bounds[:-1], bounds[1:])]
    bq = ATTN_QUERY_ROWS
    for s in seg:
        bq = math.gcd(bq, s)
    assert bq >= ATT_HALF, "sequence too short for the dilated band blocks"
    nb = rows // bq

    per = bq // ATT_HALF
    nh = rows // ATT_HALF

    def cur(kind):
        return pl.BlockSpec((None, bq, D), lambda i, c: (c, i, kind))

    def halo(kind, after):
        def imap(i, c):
            blk = (i + 1) * per if after else i * per - 1
            return (c, jnp.clip(blk, 0, nh - 1), kind)
        return pl.BlockSpec((None, ATT_HALF, D), imap)

    return pl.pallas_call(
        functools.partial(_attn_kernel, bq, D // ATT_HEAD, bounds),
        grid=(nb, dil),
        in_specs=[cur(0), halo(1, False), cur(1), halo(1, True),
                  halo(2, False), cur(2), halo(2, True)],
        out_specs=[pl.BlockSpec((None, bq, D), lambda i, c: (c, i, 0)),
                   pl.BlockSpec((None, bq, LANES), lambda i, c: (c, i, 0))],
        out_shape=[jax.ShapeDtypeStruct((dil, rows, D), BF16),
                   jax.ShapeDtypeStruct((dil, rows, LANES), F32)],
        compiler_params=_params(("parallel", "arbitrary")),
    )(qkv, qkv, qkv, qkv, qkv, qkv, qkv)


def _attn_out_kernel(tm, x_ref, o0_ref, o1_ref, o2_ref, l0_ref, l1_ref, l2_ref, wo_ref, gpost_ref,
                     out_ref, act_ref, *scratch):
    D = x_ref.shape[1]

    def token_major(ref, sl, s):
        dil = ref.shape[0]
        if dil == 1:
            return ref[0, :, sl].astype(F32)
        for c in range(dil):
            s[pl.ds(c, tm // dil, stride=dil), :] = ref[c, :, sl].astype(F32)
        return s[...]

    free = list(scratch)
    l0, l1, l2 = [token_major(r, slice(0, LANES), free.pop() if r.shape[0] > 1 else None)
                  for r in (l0_ref, l1_ref, l2_ref)]
    m = jnp.maximum(jnp.maximum(l0, l1), l2)
    e0, e1, e2 = jnp.exp(l0 - m), jnp.exp(l1 - m), jnp.exp(l2 - m)
    inv = 1.0 / (e0 + e1 + e2)
    w0, w1, w2 = e0 * inv, e1 * inv, e2 * inv
    h = None
    heads_per_dot = MXU_COLS // ATT_HEAD
    for k in range(D // ATT_HEAD):
        sl = slice(k * ATT_HEAD, (k + 1) * ATT_HEAD)
        pool = list(free)
        o0, o1, o2 = [token_major(r, sl, pool.pop() if r.shape[0] > 1 else None)
                      for r in (o0_ref, o1_ref, o2_ref)]
        act_ref[:, sl] = (w0[:, k:k + 1] * o0 + w1[:, k:k + 1] * o1
                          + w2[:, k:k + 1] * o2).astype(BF16)
        if (k + 1) % heads_per_dot == 0:
            ks = slice((k + 1 - heads_per_dot) * ATT_HEAD, (k + 1) * ATT_HEAD)
            part = jnp.dot(act_ref[:, ks], wo_ref[ks, :], preferred_element_type=F32)
            h = part if h is None else h + part
    out_ref[...] = x_ref[...] + _rms(h, gpost_ref[...])


def _attn_out(x, os_, lses, w_o, gpost):
    T, D = x.shape
    tm = _tile(T, ATTN_OUT_ROWS, BF16_ROWS * max(o.shape[0] for o in os_))
    tok = pl.BlockSpec((tm, D), lambda i: (i, 0))

    def planes(a):
        dil = a.shape[0]
        return pl.BlockSpec((dil, tm // dil, a.shape[2]), lambda i: (0, i, 0))

    n_scratch = sum(2 for o in os_ if o.shape[0] > 1)
    return pl.pallas_call(
        functools.partial(_attn_out_kernel, tm),
        grid=(T // tm,),
        in_specs=[tok] + [planes(a) for a in os_] + [planes(a) for a in lses]
        + [pl.BlockSpec((D, D), lambda i: (0, 0)), pl.BlockSpec((1, D), lambda i: (0, 0))],
        out_specs=tok,
        out_shape=jax.ShapeDtypeStruct((T, D), F32),
        scratch_shapes=[pltpu.VMEM((tm, D), BF16)] + [pltpu.VMEM((tm, LANES), F32)] * n_scratch,
        compiler_params=_params(("parallel",)),
    )(x, *os_, *lses, w_o, gpost)


def kernel(x_prompt, x_sample, norm_pre, norm_post, ffn_w_gate, ffn_w_up, ffn_w_down, rwkv_mu, rwkv_w_rkv, rwkv_w0, rwkv_w1, rwkv_w2, rwkv_a0, rwkv_a1, rwkv_a2, rwkv_v0, rwkv_v1, rwkv_v2, rwkv_g1, rwkv_g2, rwkv_k_k, rwkv_k_a, rwkv_r_k, rwkv_gn_w, rwkv_gn_b, rwkv_w_o, attn_w_qkv, attn_w_o):
    D = x_prompt.shape[-1]
    depth = norm_pre.shape[0]
    seqs = [x_prompt.shape[1]] * x_prompt.shape[0] + [x_sample.shape[1]] * x_sample.shape[0]
    seq_starts, seq_ends, t = [], [], 0
    for s in seqs:
        seq_starts.append(t)
        t += s
        seq_ends.append(t)
    seq_starts, seq_ends = tuple(seq_starts), tuple(seq_ends)
    x = jnp.concatenate([x_prompt.reshape(-1, D), x_sample.reshape(-1, D)], axis=0)
    pos = jnp.concatenate([jnp.arange(s, dtype=jnp.int32) for s in seqs])
    tables = _rope_tables(pos)

    def vec(a):
        return a.reshape(1, D)

    wg, wu, wd = ffn_w_gate.astype(BF16), ffn_w_up.astype(BF16), ffn_w_down.astype(BF16)
    w_rkv = rwkv_w_rkv.astype(BF16)
    w_qkv = _rope_perm(attn_w_qkv.astype(BF16), D)
    v_first = None
    for layer in range(depth):
        x = _ffn(x, vec(norm_pre[layer, 0]), wg, wu, wd, vec(norm_post[layer, 0]), layer, 0)
        j = layer // 2
        if layer % 2 == 0:
            vres_w = None if j == 0 else (rwkv_v0[j - 1], rwkv_v1[j - 1], rwkv_v2[j - 1])
            pre = _rwkv_pre(x, seq_starts, seq_ends, vec(norm_pre[layer, 1]), rwkv_mu[j],
                            rwkv_w0[j], rwkv_w1[j], rwkv_w2[j], rwkv_a0[j], rwkv_a1[j], rwkv_a2[j],
                            rwkv_g1[j], rwkv_g2[j], vres_w)
            xr, xk, xv, ld, a, g = pre[:6]
            r = _mm(xr, w_rkv, (j, 0))
            k = _mm(xk, w_rkv, (j, 1))
            v = _mm(xv, w_rkv, (j, 2))
            if j == 0:
                v_first, vres = v, None
            else:
                vres = (v_first, pre[6])
            y = _wkv(r, k, v, ld, a, vec(rwkv_k_k[j]), vec(rwkv_k_a[j]), vres, seq_starts, seq_ends)
            x = _rwkv_out(x, y, r, k, v, a, g, vec(rwkv_k_a[j]), vec(rwkv_r_k[j]),
                          vec(rwkv_gn_w[j]), vec(rwkv_gn_b[j]), rwkv_w_o[j].astype(BF16),
                          vec(norm_post[layer, 1]), vres)
        else:
            os_, lses = [], []
            for gi, (_, dil) in enumerate(DILATED_GROUPS):
                qkv = _qkv(x, vec(norm_pre[layer, 1]), w_qkv, j, tables, gi, dil)
                o, lse = _attn_group(qkv, D, seq_starts, seq_ends)
                os_.append(o)
                lses.append(lse)
            x = _attn_out(x, os_, lses, attn_w_o[j].astype(BF16), vec(norm_post[layer, 1]))
        x = _ffn(x, vec(norm_pre[layer, 2]), wg, wu, wd, vec(norm_post[layer, 2]), layer, 1)
    n_p = x_prompt.shape[0] * x_prompt.shape[1]
    return (x[:n_p].reshape(x_prompt.shape), x[n_p:].reshape(x_sample.shape))
```

```python
import functools
import math

import jax
import jax.numpy as jnp
from jax import lax
from jax.experimental import pallas as pl
from jax.experimental.pallas import tpu as pltpu

F32 = jnp.float32
BF16 = jnp.bfloat16

NORM_EPS = 1e-6
GN_EPS = 64e-5
RWKV_HEAD = 64
ATT_HEAD = 128
ROPE_DIM = ATT_HEAD // 4
ROPE_THETA = 500000.0
DILATED_GROUPS = ((128, 1), (512, 4), (2048, 16))
N_GROUPS = len(DILATED_GROUPS)
ATT_HALF = 64
LANES = 128
MXU_COLS = 256
LOG2E = math.log2(math.e)
WKV_CHUNK = 64
WKV_PAIRS_PER_STEP = 16
VMEM_LIMIT = 56 * 1024 * 1024
BF16_ROWS = 16

FFN_ROWS, FFN_COLS = 768, 512
MM_ROWS, MM_COLS = 1024, 1024
QKV_ROWS, QKV_COLS = 1024, 2048
RWKV_PRE_ROWS = 256
RWKV_OUT_ROWS = 256
ATTN_OUT_ROWS = 256
ATTN_QUERY_ROWS = 128

_NT = (((1,), (1,)), ((), ()))
_TN = (((0,), (0,)), ((), ()))


def _tile(n, pref, mult=8):
    if n <= pref:
        return n
    t = (pref // mult) * mult
    while t >= mult:
        if n % t == 0:
            return t
        t -= mult
    return n


def _in_list(v, lst):
    r = v == lst[0]
    for s in lst[1:]:
        r = r | (v == s)
    return r


def _rms(x, g):
    return x * lax.rsqrt(jnp.mean(x * x, axis=-1, keepdims=True) + NORM_EPS) * g


NORM_ROWS = 32


def _rms_rows(src_ref, g, emit):
    for r0 in range(0, src_ref.shape[0], NORM_ROWS):
        rows = slice(r0, min(r0 + NORM_ROWS, src_ref.shape[0]))
        emit(rows, _rms(src_ref[rows, :], g))


def _bdot(a, b):
    return jnp.dot(a.astype(BF16), b.astype(BF16), preferred_element_type=F32)


def _ntdot(a, b):
    return lax.dot_general(a, b, _NT, preferred_element_type=F32)


def _params(sem):
    return pltpu.CompilerParams(dimension_semantics=sem, vmem_limit_bytes=VMEM_LIMIT)


def _ffn_kernel(x_ref, gpre_ref, wg_ref, wu_ref, wd_ref, gpost_ref, o_ref, xn_ref, acc_ref):
    j = pl.program_id(1)

    @pl.when(j == 0)
    def _():
        def put(rows, y):
            xn_ref[rows, :] = y.astype(BF16)
        _rms_rows(x_ref, gpre_ref[...], put)
        acc_ref[...] = jnp.zeros_like(acc_ref)

    xn = xn_ref[...]
    g = jnp.dot(xn, wg_ref[...], preferred_element_type=F32)
    u = jnp.dot(xn, wu_ref[...], preferred_element_type=F32)
    h = (g * jax.nn.sigmoid(g)) * u
    acc_ref[...] += jnp.dot(h.astype(BF16), wd_ref[...], preferred_element_type=F32)

    @pl.when(j == pl.num_programs(1) - 1)
    def _():
        def put(rows, y):
            o_ref[rows, :] = x_ref[rows, :] + y
        _rms_rows(acc_ref, 0.5 * gpost_ref[...], put)


def _ffn(x, gpre, wg, wu, wd, gpost, layer, k):
    T, D = x.shape
    F = wg.shape[-1]
    tm = _tile(T, FFN_ROWS, MXU_COLS)
    tf = _tile(F, FFN_COLS, LANES)
    return pl.pallas_call(
        _ffn_kernel,
        grid=(T // tm, F // tf),
        in_specs=[
            pl.BlockSpec((tm, D), lambda i, j: (i, 0)),
            pl.BlockSpec((1, D), lambda i, j: (0, 0)),
            pl.BlockSpec((None, None, D, tf), lambda i, j: (layer, k, 0, j)),
            pl.BlockSpec((None, None, D, tf), lambda i, j: (layer, k, 0, j)),
            pl.BlockSpec((None, None, tf, D), lambda i, j: (layer, k, j, 0)),
            pl.BlockSpec((1, D), lambda i, j: (0, 0)),
        ],
        out_specs=pl.BlockSpec((tm, D), lambda i, j: (i, 0)),
        out_shape=jax.ShapeDtypeStruct((T, D), F32),
        scratch_shapes=[pltpu.VMEM((tm, D), BF16), pltpu.VMEM((tm, D), F32)],
        compiler_params=_params(("parallel", "arbitrary")),
    )(x, gpre, wg, wu, wd, gpost)


def _mm_kernel(a_ref, w_ref, o_ref):
    a = a_ref[...]
    for c in range(0, o_ref.shape[1], MXU_COLS):
        o_ref[:, c:c + MXU_COLS] = jnp.dot(
            a, w_ref[:, c:c + MXU_COLS], preferred_element_type=F32).astype(o_ref.dtype)


def _mm(a, w, lead):
    M, K = a.shape
    N = w.shape[-1]
    tm = _tile(M, MM_ROWS, BF16_ROWS)
    tn = _tile(N, MM_COLS, MXU_COLS)
    return pl.pallas_call(
        _mm_kernel,
        grid=(M // tm, N // tn),
        in_specs=[
            pl.BlockSpec((tm, K), lambda i, j: (i, 0)),
            pl.BlockSpec((None,) * len(lead) + (K, tn), lambda i, j: tuple(lead) + (0, j)),
        ],
        out_specs=pl.BlockSpec((tm, tn), lambda i, j: (i, j)),
        out_shape=jax.ShapeDtypeStruct((M, N), BF16),
        compiler_params=_params(("parallel", "arbitrary")),
    )(a, w)


def _rwkv_pre_kernel(seq_starts, seq_ends, has_vres, tm, *refs):
    (x_ref, xp_ref, xn_ref, gpre_ref, mu_ref, w0_ref, w1_ref, w2_ref,
     a0_ref, a1_ref, a2_ref, g1_ref, g2_ref) = refs[:13]
    refs = refs[13:]
    if has_vres:
        v0_ref, v1_ref, v2_ref = refs[:3]
        refs = refs[3:]
    xr_ref, xk_ref, xv_ref, ld_ref, a_ref, g_ref = refs[:6]

    t0 = pl.program_id(0) * tm
    gpre = gpre_ref[...]
    h = _rms(x_ref[...], gpre)
    keep_p = 1.0 - _in_list(t0, seq_starts).astype(F32)
    keep_n = 1.0 - _in_list(t0 + tm, seq_ends).astype(F32)
    hp = _rms(xp_ref[7:8, :], gpre) * keep_p
    hn = _rms(xn_ref[0:1, :], gpre) * keep_n
    row = lax.broadcasted_iota(jnp.int32, h.shape, 0)
    prev = jnp.where(row == 0, hp, pltpu.roll(h, 1, 0))
    nxt = jnp.where(row == tm - 1, hn, pltpu.roll(h, tm - 1, 0))
    xx = 0.5 * (prev + nxt) - h

    def mix(i):
        return (h + xx * mu_ref[i:i + 1, :]).astype(BF16)

    xr_ref[...] = mix(0)
    xw = mix(1)
    xk_ref[...] = mix(2)
    xv = mix(3)
    xv_ref[...] = xv
    xa = mix(4)
    xg = mix(5)
    for d in range(2):
        tw = jnp.tanh(jnp.dot(xw, w1_ref[d], preferred_element_type=F32))
        z = w0_ref[d:d + 1, :] + _bdot(tw, w2_ref[d])
        ld_ref[d] = (-math.exp(-0.5) * jax.nn.sigmoid(z)).astype(BF16)
        ta = jnp.dot(xa, a1_ref[d], preferred_element_type=F32)
        a_ref[d] = jax.nn.sigmoid(a0_ref[d:d + 1, :] + _bdot(ta, a2_ref[d])).astype(BF16)
    tg = jax.nn.sigmoid(jnp.dot(xg, g1_ref[...], preferred_element_type=F32))
    g_ref[...] = _bdot(tg, g2_ref[...]).astype(BF16)
    if has_vres:
        vg_ref = refs[6]
        tv = jnp.dot(xv, v1_ref[...], preferred_element_type=F32)
        vg_ref[...] = jax.nn.sigmoid(v0_ref[...] + _bdot(tv, v2_ref[...])).astype(BF16)


def _pad_lora(w_in, w_out):
    r = w_in.shape[-1]
    rp = -(-r // LANES) * LANES
    pin = [(0, 0)] * (w_in.ndim - 1) + [(0, rp - r)]
    pout = [(0, 0)] * (w_out.ndim - 2) + [(0, rp - r), (0, 0)]
    return jnp.pad(w_in, pin).astype(BF16), jnp.pad(w_out, pout).astype(BF16)


def _rwkv_pre(x, seq_starts, seq_ends, gpre, mu, w0, w1, w2, a0, a1, a2, g1, g2, vres):
    T, D = x.shape
    tm = _tile(T, RWKV_PRE_ROWS, BF16_ROWS)
    nb8 = T // 8
    has_vres = vres is not None
    w1p, w2p = _pad_lora(w1, w2)
    a1p, a2p = _pad_lora(a1, a2)
    g1p, g2p = _pad_lora(g1, g2)

    def full(a):
        nd = a.ndim
        return pl.BlockSpec(a.shape, lambda i, _nd=nd: (0,) * _nd)

    args = [x, x, x, gpre, mu, w0, w1p, w2p, a0, a1p, a2p, g1p, g2p]
    in_specs = [
        pl.BlockSpec((tm, D), lambda i: (i, 0)),
        pl.BlockSpec((8, D), lambda i: (jnp.maximum(i * (tm // 8) - 1, 0), 0)),
        pl.BlockSpec((8, D), lambda i: (jnp.minimum((i + 1) * (tm // 8), nb8 - 1), 0)),
    ] + [full(a) for a in args[3:]]
    if has_vres:
        v0, v1, v2 = vres
        v1p, v2p = _pad_lora(v1, v2)
        extra = [v0.reshape(1, D), v1p, v2p]
        args += extra
        in_specs += [full(a) for a in extra]
    tok = pl.BlockSpec((tm, D), lambda i: (i, 0))
    tok2 = pl.BlockSpec((2, tm, D), lambda i: (0, i, 0))
    out_specs = [tok, tok, tok, tok2, tok2, tok]
    out_shape = [jax.ShapeDtypeStruct((T, D), BF16)] * 3 + [
        jax.ShapeDtypeStruct((2, T, D), BF16), jax.ShapeDtypeStruct((2, T, D), BF16),
        jax.ShapeDtypeStruct((T, D), BF16)]
    if has_vres:
        out_specs.append(tok)
        out_shape.append(jax.ShapeDtypeStruct((T, D), BF16))
    return pl.pallas_call(
        functools.partial(_rwkv_pre_kernel, seq_starts, seq_ends, has_vres, tm),
        grid=(T // tm,),
        in_specs=in_specs,
        out_specs=out_specs,
        out_shape=out_shape,
        compiler_params=_params(("parallel",)),
    )(*args)


def _wkv_kernel(L, G, nc, resets_f, resets_r, has_vres, *refs):
    r_ref, k_ref, v_ref, ld_ref, a_ref, kk_ref, ka_ref = refs[:7]
    refs = refs[7:]
    if has_vres:
        vf_ref, vg_ref = refs[:2]
        refs = refs[2:]
    y_ref, s_ref = refs

    d = pl.program_id(0)
    c = pl.program_id(2)
    ci = c + d * (nc - 1 - 2 * c)
    reset = ((d == 0) & _in_list(ci, resets_f)) | ((d == 1) & _in_list(ci, resets_r))

    @pl.when(reset)
    def _():
        s_ref[...] = jnp.zeros_like(s_ref)

    sgn = 1 - 2 * d
    tt = lax.broadcasted_iota(jnp.int32, (L, LANES), 0)
    ss = lax.broadcasted_iota(jnp.int32, (L, LANES), 1) % L
    delta = sgn * (tt - ss)
    incl = delta >= 0
    strict = delta > 0
    eye = (tt == ss).astype(F32)
    tri = (sgn * (lax.broadcasted_iota(jnp.int32, (L, L), 0)
                  - lax.broadcasted_iota(jnp.int32, (L, L), 1)) >= 0).astype(BF16)
    first_head = lax.broadcasted_iota(jnp.int32, (L, LANES), 1) < RWKV_HEAD
    bi = lax.broadcasted_iota(jnp.int32, (LANES, LANES), 0) // RWKV_HEAD
    bj = lax.broadcasted_iota(jnp.int32, (LANES, LANES), 1) // RWKV_HEAD
    same_head = bi == bj
    head_ones = same_head.astype(BF16)
    n_double = int(math.log2(L)) - 1

    def bd(x):
        xb = x.astype(BF16)
        zero = jnp.zeros_like(xb)
        return jnp.concatenate([jnp.where(first_head, xb, zero), jnp.where(first_head, zero, xb)], axis=0)

    kkcs = [k_ref[:, p * LANES:(p + 1) * LANES].astype(F32) * kk_ref[:, p * LANES:(p + 1) * LANES]
            for p in range(G)]
    n2_all = jnp.dot(jnp.concatenate([(kkc * kkc).astype(BF16) for kkc in kkcs], axis=0), head_ones,
                     preferred_element_type=F32)
    pairs = []
    for p in range(G):
        sl = slice(p * LANES, (p + 1) * LANES)
        r_ = r_ref[:, sl].astype(F32)
        k_ = k_ref[:, sl].astype(F32)
        v_ = v_ref[:, sl].astype(F32)
        ld_b = ld_ref[:, sl]
        ld_ = ld_b.astype(F32)
        a_ = a_ref[:, sl].astype(F32)
        if has_vres:
            v_ = v_ + (vf_ref[:, sl].astype(F32) - v_) * vg_ref[:, sl].astype(F32)
        kkn = kkcs[p] / jnp.maximum(jnp.sqrt(n2_all[p * L:(p + 1) * L]), 1e-12)
        kd = k_ * (1.0 + (a_ - 1.0) * ka_ref[:, sl])
        cs = jnp.dot(tri, ld_b, preferred_element_type=F32)
        w_inv = jnp.exp(-cs)
        bt = kkn * a_ * w_inv
        kt = kd * w_inv
        pairs.append(dict(
            sl=sl, v=v_, bt=bt, kt=kt, st=s_ref[p],
            lhs=jnp.concatenate([-kkn * jnp.exp(cs - ld_), r_ * jnp.exp(cs)], axis=0).astype(BF16),
            wend=jnp.exp(jnp.sum(ld_, axis=0, keepdims=True))))

    for pr in pairs:
        rhs_t = jnp.concatenate([bd(pr["bt"]), bd(pr["kt"]), pr["st"].astype(BF16)], axis=0)
        prod = _ntdot(pr["lhs"], rhs_t)
        pr["a_ab"] = jnp.where(strict, prod[:L, :LANES], 0.0)
        pr["a_rb"] = jnp.where(incl, prod[L:, :LANES], 0.0)
        a_ak = jnp.where(strict, prod[:L, LANES:2 * LANES], 0.0)
        a_rk = jnp.where(incl, prod[L:, LANES:2 * LANES], 0.0)
        pr["sp"] = prod[:, 2 * LANES:]
        pr["a_k"] = jnp.concatenate([a_ak, a_rk], axis=0)
    for pr in pairs:
        pr["spv"] = pr["sp"] + _bdot(pr["a_k"], bd(pr["v"]))
    for pr in pairs:
        x = _bdot(jnp.concatenate([pr["a_ab"], pr["a_rb"]], axis=0), bd(pr["a_ab"]))
        pr["tinv"] = eye + pr["a_ab"]
        pr["pw"] = x[:L]
        pr["rt"] = pr["a_rb"] + x[L:]
    for lvl in range(1, n_double + 1):
        last = lvl == n_double
        for pr in pairs:
            rows = [pr["tinv"], pr["rt"]] if last else [pr["tinv"], pr["rt"], pr["pw"]]
            x = _bdot(jnp.concatenate(rows, axis=0), bd(pr["pw"]))
            pr["tinv"] = pr["tinv"] + x[:L]
            pr["rt"] = pr["rt"] + x[L:2 * L]
            if not last:
                pr["pw"] = x[2 * L:]
    for pr in pairs:
        z = _bdot(jnp.concatenate([pr["tinv"], pr["rt"]], axis=0), bd(pr["spv"][:L]))
        pr["u"] = z[:L]
        pr["y"] = pr["spv"][L:] + z[L:]
    for p, pr in enumerate(pairs):
        y_ref[:, pr["sl"]] = pr["y"].astype(BF16)
        uv = jnp.concatenate([pr["u"], pr["v"]], axis=0).astype(BF16)
        bk = jnp.concatenate([pr["bt"], pr["kt"]], axis=0).astype(BF16)
        upd = lax.dot_general(uv, bk, _TN, preferred_element_type=F32)
        s_ref[p] = (pr["st"] + jnp.where(same_head, upd, 0.0)) * pr["wend"]


def _wkv(r, k, v, ld, a, k_k, k_a, vres, seq_starts, seq_ends):
    T, D = r.shape
    L = WKV_CHUNK
    npairs = D // LANES
    assert 2 * L == LANES and RWKV_HEAD == L
    G = math.gcd(npairs, WKV_PAIRS_PER_STEP)
    nc = T // L
    W = G * LANES
    resets_f = tuple(s // L for s in seq_starts)
    resets_r = tuple(e // L - 1 for e in seq_ends)
    has_vres = vres is not None

    def cidx(d, c):
        return c + d * (nc - 1 - 2 * c)

    tok = pl.BlockSpec((L, W), lambda d, p, c: (cidx(d, c), p))
    tokd = pl.BlockSpec((None, L, W), lambda d, p, c: (d, cidx(d, c), p))
    par = pl.BlockSpec((1, W), lambda d, p, c: (0, p))
    args = [r, k, v, ld, a, k_k, k_a]
    in_specs = [tok, tok, tok, tokd, tokd, par, par]
    if has_vres:
        args += list(vres)
        in_specs += [tok, tok]
    return pl.pallas_call(
        functools.partial(_wkv_kernel, L, G, nc, resets_f, resets_r, has_vres),
        grid=(2, npairs // G, nc),
        in_specs=in_specs,
        out_specs=tokd,
        out_shape=jax.ShapeDtypeStruct((2, T, D), BF16),
        scratch_shapes=[pltpu.VMEM((G, LANES, LANES), F32)],
        compiler_params=_params(("arbitrary", "arbitrary", "arbitrary")),
    )(*args)


def _rwkv_out_kernel(has_vres, *refs):
    (x_ref, y_ref, r_ref, k_ref, v_ref, a_ref, g_ref, ka_ref, rk_ref, gnw_ref, gnb_ref,
     wo_ref, gpost_ref) = refs[:13]
    refs = refs[13:]
    if has_vres:
        vf_ref, vg_ref = refs[:2]
        refs = refs[2:]
    o_ref, act_ref = refs
    D = x_ref.shape[1]

    bi = lax.broadcasted_iota(jnp.int32, (LANES, LANES), 0) // RWKV_HEAD
    bj = lax.broadcasted_iota(jnp.int32, (LANES, LANES), 1) // RWKV_HEAD
    head_ones = (bi == bj).astype(BF16)
    inv_n = 1.0 / RWKV_HEAD

    slabs = [slice(s * LANES, (s + 1) * LANES) for s in range(D // LANES)]
    ys = [y_ref[0, :, sl].astype(F32) + y_ref[1, :, sl].astype(F32) for sl in slabs]
    means = [_bdot(y, head_ones) * inv_n for y in ys]
    dys = [y - mu for y, mu in zip(ys, means)]
    vars_ = [_bdot(dy * dy, head_ones) * inv_n for dy in dys]
    rks = []
    for sl in slabs:
        a_mean = 0.5 * (a_ref[0, :, sl].astype(F32) + a_ref[1, :, sl].astype(F32))
        k_mean = k_ref[:, sl].astype(F32) * (1.0 + (a_mean - 1.0) * ka_ref[:, sl])
        rks.append(_bdot(r_ref[:, sl].astype(F32) * k_mean * rk_ref[:, sl], head_ones))
    h = None
    slabs_per_dot = MXU_COLS // LANES
    for s, (sl, dy, var, rk) in enumerate(zip(slabs, dys, vars_, rks)):
        yn = dy * lax.rsqrt(var + GN_EPS) * gnw_ref[:, sl] + gnb_ref[:, sl]
        v_ = v_ref[:, sl].astype(F32)
        if has_vres:
            v_ = v_ + (vf_ref[:, sl].astype(F32) - v_) * vg_ref[:, sl].astype(F32)
        act_ref[:, sl] = ((yn + rk * v_) * g_ref[:, sl].astype(F32)).astype(BF16)
        if (s + 1) % slabs_per_dot == 0:
            ks = slice((s + 1 - slabs_per_dot) * LANES, (s + 1) * LANES)
            part = jnp.dot(act_ref[:, ks], wo_ref[ks, :], preferred_element_type=F32)
            h = part if h is None else h + part
    o_ref[...] = x_ref[...] + _rms(h, gpost_ref[...])


def _rwkv_out(x, y, r, k, v, a, g, k_a, r_k, gn_w, gn_b, w_o, gpost, vres):
    T, D = x.shape
    tm = _tile(T, RWKV_OUT_ROWS, BF16_ROWS)
    has_vres = vres is not None
    tok = pl.BlockSpec((tm, D), lambda i: (i, 0))
    tok2 = pl.BlockSpec((2, tm, D), lambda i: (0, i, 0))
    par = pl.BlockSpec((1, D), lambda i: (0, 0))
    args = [x, y, r, k, v, a, g, k_a, r_k, gn_w, gn_b, w_o, gpost]
    in_specs = [tok, tok2, tok, tok, tok, tok2, tok, par, par, par, par,
                pl.BlockSpec((D, D), lambda i: (0, 0)), par]
    if has_vres:
        args += list(vres)
        in_specs += [tok, tok]
    return pl.pallas_call(
        functools.partial(_rwkv_out_kernel, has_vres),
        grid=(T // tm,),
        in_specs=in_specs,
        out_specs=tok,
        out_shape=jax.ShapeDtypeStruct((T, D), F32),
        scratch_shapes=[pltpu.VMEM((tm, D), BF16)],
        compiler_params=_params(("parallel",)),
    )(*args)


def _qkv_kernel(tm, tn, deint, x_ref, gpre_ref, w_ref, cos_ref, sin_ref, o_ref, xn_ref, *acc):
    j = pl.program_id(1)

    @pl.when(j == 0)
    def _():
        def put(rows, y):
            xn_ref[rows, :] = y.astype(BF16)
        _rms_rows(x_ref, gpre_ref[...], put)

    cos = cos_ref[...]
    sin = sin_ref[...]
    xn = xn_ref[...]
    per_dot = MXU_COLS // ATT_HEAD
    for s in range(tn // ATT_HEAD):
        sl = slice(s * ATT_HEAD, (s + 1) * ATT_HEAD)
        if s % per_dot == 0:
            prod = jnp.dot(xn, w_ref[:, s * ATT_HEAD:s * ATT_HEAD + MXU_COLS],
                           preferred_element_type=F32)
        t = prod[:, (s % per_dot) * ATT_HEAD:(s % per_dot + 1) * ATT_HEAD]
        rot = t * cos + pltpu.roll(t, ATT_HEAD // 2, 1) * sin
        if deint == 1:
            o_ref[:, sl] = rot.astype(BF16)
        else:
            acc_ref, = acc
            slot = s % acc_ref.shape[0]
            acc_ref[slot] = rot
            for c in range(deint):
                o_ref[c, :, sl] = acc_ref[slot, pl.ds(c, tm // deint, stride=deint), :].astype(BF16)


def _rope_tables(pos):
    half = ROPE_DIM // 2
    inv = ROPE_THETA ** (-jnp.arange(half, dtype=F32) / half)
    ang = pos.astype(F32)[:, None] * inv[None, :]
    cos, sin = jnp.cos(ang), jnp.sin(ang)
    T = pos.shape[0]
    gap = ATT_HEAD // 2 - half
    ones = jnp.ones((T, gap), F32)
    zeros = jnp.zeros((T, gap), F32)
    cos_t = jnp.concatenate([cos, ones, cos, ones], axis=1)
    sin_t = jnp.concatenate([-sin, zeros, sin, zeros], axis=1)
    return (jnp.stack([cos_t, jnp.ones_like(cos_t)]), jnp.stack([sin_t, jnp.zeros_like(sin_t)]))


def _rope_perm_kernel(w_ref, o_ref):
    kind = pl.program_id(2) % 3

    @pl.when(kind == 2)
    def _():
        o_ref[...] = w_ref[...].astype(BF16)

    @pl.when(kind < 2)
    def _():
        half = ROPE_DIM // 2
        mid = ATT_HEAD // 2
        lane = lax.broadcasted_iota(jnp.int32, (w_ref.shape[0], ATT_HEAD), 1)
        from_hi = (lane >= half) & (lane < 2 * half)
        from_lo = (lane >= mid) & (lane < mid + half)
        for c0 in range(0, w_ref.shape[1], ATT_HEAD):
            w = w_ref[:, c0:c0 + ATT_HEAD]
            swapped = jnp.where(from_hi, pltpu.roll(w, ATT_HEAD - (mid - half), 1),
                                jnp.where(from_lo, pltpu.roll(w, mid - half, 1), w))
            o_ref[:, c0:c0 + ATT_HEAD] = swapped.astype(BF16)


def _rope_perm(w_qkv, D):
    n, K, N = w_qkv.shape
    tr = _tile(K, 512)
    spec = pl.BlockSpec((None, tr, D), lambda l, i, j: (l, i, j))
    return pl.pallas_call(
        _rope_perm_kernel,
        grid=(n, K // tr, N // D),
        in_specs=[spec],
        out_specs=spec,
        out_shape=jax.ShapeDtypeStruct(w_qkv.shape, BF16),
        compiler_params=_params(("parallel", "parallel", "arbitrary")),
    )(w_qkv)


def _qkv(x, gpre, w, layer, tables, gi, dil):
    T, D = x.shape
    tn = _tile(D, QKV_COLS, MXU_COLS)
    nj = 3 * D // tn

    def plane(j):
        return (j * tn) // (2 * D)

    tm = _tile(T, QKV_ROWS, BF16_ROWS * dil)
    tab = pl.BlockSpec((None, tm, ATT_HEAD), lambda i, j: (plane(j), i, 0))
    if dil == 1:
        out = pl.BlockSpec((None, tm, tn), lambda i, j: (0, i, j))
        deint_scratch = []
    else:
        out = pl.BlockSpec((dil, tm // dil, tn), lambda i, j: (0, i, j))
        deint_scratch = [pltpu.VMEM((2, tm, ATT_HEAD), F32)]
    return pl.pallas_call(
        functools.partial(_qkv_kernel, tm, tn, dil),
        grid=(T // tm, nj),
        in_specs=[
            pl.BlockSpec((tm, D), lambda i, j: (i, 0)),
            pl.BlockSpec((1, D), lambda i, j: (0, 0)),
            pl.BlockSpec((None, D, tn), lambda i, j: (layer, 0, gi * nj + j)),
            tab, tab,
        ],
        out_specs=out,
        out_shape=jax.ShapeDtypeStruct((dil, T // dil, 3 * D), BF16),
        scratch_shapes=[pltpu.VMEM((tm, D), BF16)] + deint_scratch,
        compiler_params=_params(("parallel", "arbitrary")),
    )(x, gpre, w, *tables)


def _attn_kernel(bq, n_heads, bounds, q_ref, kp_ref, kc_ref, kn_ref, vp_ref, vc_ref, vn_ref,
                 o_ref, lse_ref):
    row0 = pl.program_id(0) * bq
    lo = jnp.int32(bounds[0])
    hi = jnp.int32(bounds[-1])
    for b in bounds[1:-1]:
        lo = jnp.where(row0 >= b, b, lo)
    for b in reversed(bounds[1:-1]):
        hi = jnp.where(row0 < b, b, hi)
    nk = bq + 2 * ATT_HALF
    qrow = row0 + lax.broadcasted_iota(jnp.int32, (bq, nk), 0)
    krow = row0 - ATT_HALF + lax.broadcasted_iota(jnp.int32, (bq, nk), 1)
    valid = (jnp.abs(krow - qrow) <= ATT_HALF) & (krow >= lo) & (krow < hi)
    scale = ATT_HEAD ** -0.5
    lane = lax.broadcasted_iota(jnp.int32, (bq, LANES), 1)
    lse_all = jnp.zeros((bq, LANES), F32)
    for h in range(n_heads):
        sl = slice(h * ATT_HEAD, (h + 1) * ATT_HEAD)
        q = q_ref[:, sl]
        kcat = jnp.concatenate([kp_ref[:, sl], kc_ref[:, sl], kn_ref[:, sl]], axis=0)
        vcat = jnp.concatenate([vp_ref[:, sl], vc_ref[:, sl], vn_ref[:, sl]], axis=0)
        s = lax.dot_general(q, kcat, _NT, preferred_element_type=F32) * (scale * LOG2E)
        s = jnp.where(valid, s, -1e30)
        m = jnp.max(s, axis=-1, keepdims=True)
        p = jnp.exp2(s - m)
        l = jnp.sum(p, axis=-1, keepdims=True)
        o = jnp.dot(p.astype(BF16), vcat, preferred_element_type=F32) / l
        o_ref[:, sl] = o.astype(BF16)
        lse_all = jnp.where(lane == h, (m + jnp.log2(l)) * (1.0 / LOG2E), lse_all)
    lse_ref[...] = lse_all


def _attn_group(qkv, D, seq_starts, seq_ends):
    dil, rows, _ = qkv.shape
    bounds = tuple(s // dil for s in seq_starts) + (seq_ends[-1] // dil,)
    seg = [b - a for a, b in zip(bounds[:-1], bounds[1:])]
    bq = ATTN_QUERY_ROWS
    for s in seg:
        bq = math.gcd(bq, s)
    assert bq >= ATT_HALF, "sequence too short for the dilated band blocks"
    nb = rows // bq

    per = bq // ATT_HALF
    nh = rows // ATT_HALF

    def cur(kind):
        return pl.BlockSpec((None, bq, D), lambda i, c: (c, i, kind))

    def halo(kind, after):
        def imap(i, c):
            blk = (i + 1) * per if after else i * per - 1
            return (c, jnp.clip(blk, 0, nh - 1), kind)
        return pl.BlockSpec((None, ATT_HALF, D), imap)

    return pl.pallas_call(
        functools.partial(_attn_kernel, bq, D // ATT_HEAD, bounds),
        grid=(nb, dil),
        in_specs=[cur(0), halo(1, False), cur(1), halo(1, True),
                  halo(2, False), cur(2), halo(2, True)],
        out_specs=[pl.BlockSpec((None, bq, D), lambda i, c: (c, i, 0)),
                   pl.BlockSpec((None, bq, LANES), lambda i, c: (c, i, 0))],
        out_shape=[jax.ShapeDtypeStruct((dil, rows, D), BF16),
                   jax.ShapeDtypeStruct((dil, rows, LANES), F32)],
        compiler_params=_params(("parallel", "arbitrary")),
    )(qkv, qkv, qkv, qkv, qkv, qkv, qkv)


def _attn_out_kernel(tm, x_ref, o0_ref, o1_ref, o2_ref, l0_ref, l1_ref, l2_ref, wo_ref, gpost_ref,
                     out_ref, act_ref, *scratch):
    D = x_ref.shape[1]

    def token_major(ref, sl, s):
        dil = ref.shape[0]
        if dil == 1:
            return ref[0, :, sl].astype(F32)
        for c in range(dil):
            s[pl.ds(c, tm // dil, stride=dil), :] = ref[c, :, sl].astype(F32)
        return s[...]

    free = list(scratch)
    l0, l1, l2 = [token_major(r, slice(0, LANES), free.pop() if r.shape[0] > 1 else None)
                  for r in (l0_ref, l1_ref, l2_ref)]
    m = jnp.maximum(jnp.maximum(l0, l1), l2)
    e0, e1, e2 = jnp.exp(l0 - m), jnp.exp(l1 - m), jnp.exp(l2 - m)
    inv = 1.0 / (e0 + e1 + e2)
    w0, w1, w2 = e0 * inv, e1 * inv, e2 * inv
    h = None
    heads_per_dot = MXU_COLS // ATT_HEAD
    for k in range(D // ATT_HEAD):
        sl = slice(k * ATT_HEAD, (k + 1) * ATT_HEAD)
        pool = list(free)
        o0, o1, o2 = [token_major(r, sl, pool.pop() if r.shape[0] > 1 else None)
                      for r in (o0_ref, o1_ref, o2_ref)]
        act_ref[:, sl] = (w0[:, k:k + 1] * o0 + w1[:, k:k + 1] * o1
                          + w2[:, k:k + 1] * o2).astype(BF16)
        if (k + 1) % heads_per_dot == 0:
            ks = slice((k + 1 - heads_per_dot) * ATT_HEAD, (k + 1) * ATT_HEAD)
            part = jnp.dot(act_ref[:, ks], wo_ref[ks, :], preferred_element_type=F32)
            h = part if h is None else h + part
    out_ref[...] = x_ref[...] + _rms(h, gpost_ref[...])


def _attn_out(x, os_, lses, w_o, gpost):
    T, D = x.shape
    tm = _tile(T, ATTN_OUT_ROWS, BF16_ROWS * max(o.shape[0] for o in os_))
    tok = pl.BlockSpec((tm, D), lambda i: (i, 0))

    def planes(a):
        dil = a.shape[0]
        return pl.BlockSpec((dil, tm // dil, a.shape[2]), lambda i: (0, i, 0))

    n_scratch = sum(2 for o in os_ if o.shape[0] > 1)
    return pl.pallas_call(
        functools.partial(_attn_out_kernel, tm),
        grid=(T // tm,),
        in_specs=[tok] + [planes(a) for a in os_] + [planes(a) for a in lses]
        + [pl.BlockSpec((D, D), lambda i: (0, 0)), pl.BlockSpec((1, D), lambda i: (0, 0))],
        out_specs=tok,
        out_shape=jax.ShapeDtypeStruct((T, D), F32),
        scratch_shapes=[pltpu.VMEM((tm, D), BF16)] + [pltpu.VMEM((tm, LANES), F32)] * n_scratch,
        compiler_params=_params(("parallel",)),
    )(x, *os_, *lses, w_o, gpost)


def kernel(x_prompt, x_sample, norm_pre, norm_post, ffn_w_gate, ffn_w_up, ffn_w_down, rwkv_mu, rwkv_w_rkv, rwkv_w0, rwkv_w1, rwkv_w2, rwkv_a0, rwkv_a1, rwkv_a2, rwkv_v0, rwkv_v1, rwkv_v2, rwkv_g1, rwkv_g2, rwkv_k_k, rwkv_k_a, rwkv_r_k, rwkv_gn_w, rwkv_gn_b, rwkv_w_o, attn_w_qkv, attn_w_o):
    D = x_prompt.shape[-1]
    depth = norm_pre.shape[0]
    seqs = [x_prompt.shape[1]] * x_prompt.shape[0] + [x_sample.shape[1]] * x_sample.shape[0]
    seq_starts, seq_ends, t = [], [], 0
    for s in seqs:
        seq_starts.append(t)
        t += s
        seq_ends.append(t)
    seq_starts, seq_ends = tuple(seq_starts), tuple(seq_ends)
    x = jnp.concatenate([x_prompt.reshape(-1, D), x_sample.reshape(-1, D)], axis=0)
    pos = jnp.concatenate([jnp.arange(s, dtype=jnp.int32) for s in seqs])
    tables = _rope_tables(pos)

    def vec(a):
        return a.reshape(1, D)

    wg, wu, wd = ffn_w_gate.astype(BF16), ffn_w_up.astype(BF16), ffn_w_down.astype(BF16)
    w_rkv = rwkv_w_rkv.astype(BF16)
    w_qkv = _rope_perm(attn_w_qkv, D)
    v_first = None
    for layer in range(depth):
        x = _ffn(x, vec(norm_pre[layer, 0]), wg, wu, wd, vec(norm_post[layer, 0]), layer, 0)
        j = layer // 2
        if layer % 2 == 0:
            vres_w = None if j == 0 else (rwkv_v0[j - 1], rwkv_v1[j - 1], rwkv_v2[j - 1])
            pre = _rwkv_pre(x, seq_starts, seq_ends, vec(norm_pre[layer, 1]), rwkv_mu[j],
                            rwkv_w0[j], rwkv_w1[j], rwkv_w2[j], rwkv_a0[j], rwkv_a1[j], rwkv_a2[j],
                            rwkv_g1[j], rwkv_g2[j], vres_w)
            xr, xk, xv, ld, a, g = pre[:6]
            r = _mm(xr, w_rkv, (j, 0))
            k = _mm(xk, w_rkv, (j, 1))
            v = _mm(xv, w_rkv, (j, 2))
            if j == 0:
                v_first, vres = v, None
            else:
                vres = (v_first, pre[6])
            y = _wkv(r, k, v, ld, a, vec(rwkv_k_k[j]), vec(rwkv_k_a[j]), vres, seq_starts, seq_ends)
            x = _rwkv_out(x, y, r, k, v, a, g, vec(rwkv_k_a[j]), vec(rwkv_r_k[j]),
                          vec(rwkv_gn_w[j]), vec(rwkv_gn_b[j]), rwkv_w_o[j].astype(BF16),
                          vec(norm_post[layer, 1]), vres)
        else:
            os_, lses = [], []
            for gi, (_, dil) in enumerate(DILATED_GROUPS):
                qkv = _qkv(x, vec(norm_pre[layer, 1]), w_qkv, j, tables, gi, dil)
                o, lse = _attn_group(qkv, D, seq_starts, seq_ends)
                os_.append(o)
                lses.append(lse)
            x = _attn_out(x, os_, lses, attn_w_o[j].astype(BF16), vec(norm_post[layer, 1]))
        x = _ffn(x, vec(norm_pre[layer, 2]), wg, wu, wd, vec(norm_post[layer, 2]), layer, 1)
    n_p = x_prompt.shape[0] * x_prompt.shape[1]
    return (x[:n_p].reshape(x_prompt.shape), x[n_p:].reshape(x_sample.shape))
```

```python
import functools
import math

import jax
import jax.numpy as jnp
from jax import lax
from jax.experimental import pallas as pl
from jax.experimental.pallas import tpu as pltpu

F32 = jnp.float32
BF16 = jnp.bfloat16

NORM_EPS = 1e-6
GN_EPS = 64e-5
RWKV_HEAD = 64
ATT_HEAD = 128
ROPE_DIM = ATT_HEAD // 4
ROPE_THETA = 500000.0
DILATED_GROUPS = ((128, 1), (512, 4), (2048, 16))
N_GROUPS = len(DILATED_GROUPS)
ATT_HALF = 64
LANES = 128
MXU_COLS = 256
LOG2E = math.log2(math.e)
WKV_CHUNK = 64
WKV_PAIRS_PER_STEP = 16
VMEM_LIMIT = 56 * 1024 * 1024
BF16_ROWS = 16

FFN_ROWS, FFN_COLS = 768, 512
MM_ROWS, MM_COLS = 1024, 1024
QKV_ROWS, QKV_COLS = 1024, 2048
RWKV_PRE_ROWS = 256
RWKV_OUT_ROWS = 256
ATTN_OUT_ROWS = 256
ATTN_QUERY_ROWS = 128

_NT = (((1,), (1,)), ((), ()))
_TN = (((0,), (0,)), ((), ()))


def _tile(n, pref, mult=8):
    if n <= pref:
        return n
    t = (pref // mult) * mult
    while t >= mult:
        if n % t == 0:
            return t
        t -= mult
    return n


def _in_list(v, lst):
    r = v == lst[0]
    for s in lst[1:]:
        r = r | (v == s)
    return r


def _rms(x, g):
    return x * lax.rsqrt(jnp.mean(x * x, axis=-1, keepdims=True) + NORM_EPS) * g


NORM_ROWS = 32


def _rms_rows(src_ref, g, emit):
    for r0 in range(0, src_ref.shape[0], NORM_ROWS):
        rows = slice(r0, min(r0 + NORM_ROWS, src_ref.shape[0]))
        emit(rows, _rms(src_ref[rows, :], g))


def _bdot(a, b):
    return jnp.dot(a.astype(BF16), b.astype(BF16), preferred_element_type=F32)


def _ntdot(a, b):
    return lax.dot_general(a, b, _NT, preferred_element_type=F32)


def _params(sem):
    return pltpu.CompilerParams(dimension_semantics=sem, vmem_limit_bytes=VMEM_LIMIT)


def _ffn_kernel(x_ref, gpre_ref, wg_ref, wu_ref, wd_ref, gpost_ref, o_ref, xn_ref, acc_ref):
    j = pl.program_id(1)

    @pl.when(j == 0)
    def _():
        def put(rows, y):
            xn_ref[rows, :] = y.astype(BF16)
        _rms_rows(x_ref, gpre_ref[...], put)
        acc_ref[...] = jnp.zeros_like(acc_ref)

    xn = xn_ref[...]
    g = jnp.dot(xn, wg_ref[...], preferred_element_type=F32)
    u = jnp.dot(xn, wu_ref[...], preferred_element_type=F32)
    h = (g * jax.nn.sigmoid(g)) * u
    acc_ref[...] += jnp.dot(h.astype(BF16), wd_ref[...], preferred_element_type=F32)

    @pl.when(j == pl.num_programs(1) - 1)
    def _():
        def put(rows, y):
            o_ref[rows, :] = x_ref[rows, :] + y
        _rms_rows(acc_ref, 0.5 * gpost_ref[...], put)


def _ffn(x, gpre, wg, wu, wd, gpost, layer, k):
    T, D = x.shape
    F = wg.shape[-1]
    tm = _tile(T, FFN_ROWS, MXU_COLS)
    tf = _tile(F, FFN_COLS, LANES)
    return pl.pallas_call(
        _ffn_kernel,
        grid=(T // tm, F // tf),
        in_specs=[
            pl.BlockSpec((tm, D), lambda i, j: (i, 0)),
            pl.BlockSpec((1, D), lambda i, j: (0, 0)),
            pl.BlockSpec((None, None, D, tf), lambda i, j: (layer, k, 0, j)),
            pl.BlockSpec((None, None, D, tf), lambda i, j: (layer, k, 0, j)),
            pl.BlockSpec((None, None, tf, D), lambda i, j: (layer, k, j, 0)),
            pl.BlockSpec((1, D), lambda i, j: (0, 0)),
        ],
        out_specs=pl.BlockSpec((tm, D), lambda i, j: (i, 0)),
        out_shape=jax.ShapeDtypeStruct((T, D), F32),
        scratch_shapes=[pltpu.VMEM((tm, D), BF16), pltpu.VMEM((tm, D), F32)],
        compiler_params=_params(("parallel", "arbitrary")),
    )(x, gpre, wg, wu, wd, gpost)


def _mm_kernel(a_ref, w_ref, o_ref):
    a = a_ref[...]
    for c in range(0, o_ref.shape[1], MXU_COLS):
        o_ref[:, c:c + MXU_COLS] = jnp.dot(
            a, w_ref[:, c:c + MXU_COLS], preferred_element_type=F32).astype(o_ref.dtype)


def _mm(a, w, lead):
    M, K = a.shape
    N = w.shape[-1]
    tm = _tile(M, MM_ROWS, BF16_ROWS)
    tn = _tile(N, MM_COLS, MXU_COLS)
    return pl.pallas_call(
        _mm_kernel,
        grid=(M // tm, N // tn),
        in_specs=[
            pl.BlockSpec((tm, K), lambda i, j: (i, 0)),
            pl.BlockSpec((None,) * len(lead) + (K, tn), lambda i, j: tuple(lead) + (0, j)),
        ],
        out_specs=pl.BlockSpec((tm, tn), lambda i, j: (i, j)),
        out_shape=jax.ShapeDtypeStruct((M, N), BF16),
        compiler_params=_params(("parallel", "arbitrary")),
    )(a, w)


def _rwkv_pre_kernel(seq_starts, seq_ends, has_vres, tm, *refs):
    (x_ref, xp_ref, xn_ref, gpre_ref, mu_ref, w0_ref, w1_ref, w2_ref,
     a0_ref, a1_ref, a2_ref, g1_ref, g2_ref) = refs[:13]
    refs = refs[13:]
    if has_vres:
        v0_ref, v1_ref, v2_ref = refs[:3]
        refs = refs[3:]
    xr_ref, xk_ref, xv_ref, ld_ref, a_ref, g_ref = refs[:6]

    t0 = pl.program_id(0) * tm
    gpre = gpre_ref[...]
    h = _rms(x_ref[...], gpre)
    keep_p = 1.0 - _in_list(t0, seq_starts).astype(F32)
    keep_n = 1.0 - _in_list(t0 + tm, seq_ends).astype(F32)
    hp = _rms(xp_ref[7:8, :], gpre) * keep_p
    hn = _rms(xn_ref[0:1, :], gpre) * keep_n
    row = lax.broadcasted_iota(jnp.int32, h.shape, 0)
    prev = jnp.where(row == 0, hp, pltpu.roll(h, 1, 0))
    nxt = jnp.where(row == tm - 1, hn, pltpu.roll(h, tm - 1, 0))
    xx = 0.5 * (prev + nxt) - h

    def mix(i):
        return (h + xx * mu_ref[i:i + 1, :]).astype(BF16)

    xr_ref[...] = mix(0)
    xw = mix(1)
    xk_ref[...] = mix(2)
    xv = mix(3)
    xv_ref[...] = xv
    xa = mix(4)
    xg = mix(5)
    for d in range(2):
        tw = jnp.tanh(jnp.dot(xw, w1_ref[d], preferred_element_type=F32))
        z = w0_ref[d:d + 1, :] + _bdot(tw, w2_ref[d])
        ld_ref[d] = (-math.exp(-0.5) * jax.nn.sigmoid(z)).astype(BF16)
        ta = jnp.dot(xa, a1_ref[d], preferred_element_type=F32)
        a_ref[d] = jax.nn.sigmoid(a0_ref[d:d + 1, :] + _bdot(ta, a2_ref[d])).astype(BF16)
    tg = jax.nn.sigmoid(jnp.dot(xg, g1_ref[...], preferred_element_type=F32))
    g_ref[...] = _bdot(tg, g2_ref[...]).astype(BF16)
    if has_vres:
        vg_ref = refs[6]
        tv = jnp.dot(xv, v1_ref[...], preferred_element_type=F32)
        vg_ref[...] = jax.nn.sigmoid(v0_ref[...] + _bdot(tv, v2_ref[...])).astype(BF16)


def _pad_lora(w_in, w_out):
    r = w_in.shape[-1]
    rp = -(-r // LANES) * LANES
    pin = [(0, 0)] * (w_in.ndim - 1) + [(0, rp - r)]
    pout = [(0, 0)] * (w_out.ndim - 2) + [(0, rp - r), (0, 0)]
    return jnp.pad(w_in, pin).astype(BF16), jnp.pad(w_out, pout).astype(BF16)


def _rwkv_pre(x, seq_starts, seq_ends, gpre, mu, w0, w1, w2, a0, a1, a2, g1, g2, vres):
    T, D = x.shape
    tm = _tile(T, RWKV_PRE_ROWS, BF16_ROWS)
    nb8 = T // 8
    has_vres = vres is not None
    w1p, w2p = _pad_lora(w1, w2)
    a1p, a2p = _pad_lora(a1, a2)
    g1p, g2p = _pad_lora(g1, g2)

    def full(a):
        nd = a.ndim
        return pl.BlockSpec(a.shape, lambda i, _nd=nd: (0,) * _nd)

    args = [x, x, x, gpre, mu, w0, w1p, w2p, a0, a1p, a2p, g1p, g2p]
    in_specs = [
        pl.BlockSpec((tm, D), lambda i: (i, 0)),
        pl.BlockSpec((8, D), lambda i: (jnp.maximum(i * (tm // 8) - 1, 0), 0)),
        pl.BlockSpec((8, D), lambda i: (jnp.minimum((i + 1) * (tm // 8), nb8 - 1), 0)),
    ] + [full(a) for a in args[3:]]
    if has_vres:
        v0, v1, v2 = vres
        v1p, v2p = _pad_lora(v1, v2)
        extra = [v0.reshape(1, D), v1p, v2p]
        args += extra
        in_specs += [full(a) for a in extra]
    tok = pl.BlockSpec((tm, D), lambda i: (i, 0))
    tok2 = pl.BlockSpec((2, tm, D), lambda i: (0, i, 0))
    out_specs = [tok, tok, tok, tok2, tok2, tok]
    out_shape = [jax.ShapeDtypeStruct((T, D), BF16)] * 3 + [
        jax.ShapeDtypeStruct((2, T, D), BF16), jax.ShapeDtypeStruct((2, T, D), BF16),
        jax.ShapeDtypeStruct((T, D), BF16)]
    if has_vres:
        out_specs.append(tok)
        out_shape.append(jax.ShapeDtypeStruct((T, D), BF16))
    return pl.pallas_call(
        functools.partial(_rwkv_pre_kernel, seq_starts, seq_ends, has_vres, tm),
        grid=(T // tm,),
        in_specs=in_specs,
        out_specs=out_specs,
        out_shape=out_shape,
        compiler_params=_params(("parallel",)),
    )(*args)


def _wkv_kernel(L, G, nc, resets_f, resets_r, has_vres, *refs):
    r_ref, k_ref, v_ref, ld_ref, a_ref, kk_ref, ka_ref = refs[:7]
    refs = refs[7:]
    if has_vres:
        vf_ref, vg_ref = refs[:2]
        refs = refs[2:]
    y_ref, s_ref = refs

    d = pl.program_id(0)
    c = pl.program_id(2)
    ci = c + d * (nc - 1 - 2 * c)
    reset = ((d == 0) & _in_list(ci, resets_f)) | ((d == 1) & _in_list(ci, resets_r))

    @pl.when(reset)
    def _():
        s_ref[...] = jnp.zeros_like(s_ref)

    sgn = 1 - 2 * d
    tt = lax.broadcasted_iota(jnp.int32, (L, LANES), 0)
    ss = lax.broadcasted_iota(jnp.int32, (L, LANES), 1) % L
    delta = sgn * (tt - ss)
    incl = delta >= 0
    strict = delta > 0
    eye = (tt == ss).astype(F32)
    tri = (sgn * (lax.broadcasted_iota(jnp.int32, (L, L), 0)
                  - lax.broadcasted_iota(jnp.int32, (L, L), 1)) >= 0).astype(BF16)
    first_head = lax.broadcasted_iota(jnp.int32, (L, LANES), 1) < RWKV_HEAD
    bi = lax.broadcasted_iota(jnp.int32, (LANES, LANES), 0) // RWKV_HEAD
    bj = lax.broadcasted_iota(jnp.int32, (LANES, LANES), 1) // RWKV_HEAD
    same_head = bi == bj
    head_ones = same_head.astype(BF16)
    n_double = int(math.log2(L)) - 1

    def bd(x):
        xb = x.astype(BF16)
        zero = jnp.zeros_like(xb)
        return jnp.concatenate([jnp.where(first_head, xb, zero), jnp.where(first_head, zero, xb)], axis=0)

    kkcs = [k_ref[:, p * LANES:(p + 1) * LANES].astype(F32) * kk_ref[:, p * LANES:(p + 1) * LANES]
            for p in range(G)]
    n2_all = jnp.dot(jnp.concatenate([(kkc * kkc).astype(BF16) for kkc in kkcs], axis=0), head_ones,
                     preferred_element_type=F32)
    pairs = []
    for p in range(G):
        sl = slice(p * LANES, (p + 1) * LANES)
        r_ = r_ref[:, sl].astype(F32)
        k_ = k_ref[:, sl].astype(F32)
        v_ = v_ref[:, sl].astype(F32)
        ld_b = ld_ref[:, sl]
        ld_ = ld_b.astype(F32)
        a_ = a_ref[:, sl].astype(F32)
        if has_vres:
            v_ = v_ + (vf_ref[:, sl].astype(F32) - v_) * vg_ref[:, sl].astype(F32)
        kkn = kkcs[p] / jnp.maximum(jnp.sqrt(n2_all[p * L:(p + 1) * L]), 1e-12)
        kd = k_ * (1.0 + (a_ - 1.0) * ka_ref[:, sl])
        cs = jnp.dot(tri, ld_b, preferred_element_type=F32)
        w_inv = jnp.exp(-cs)
        bt = kkn * a_ * w_inv
        kt = kd * w_inv
        pairs.append(dict(
            sl=sl, v=v_, bt=bt, kt=kt, st=s_ref[p],
            lhs=jnp.concatenate([-kkn * jnp.exp(cs - ld_), r_ * jnp.exp(cs)], axis=0).astype(BF16),
            wend=jnp.exp(jnp.sum(ld_, axis=0, keepdims=True))))

    for pr in pairs:
        rhs_t = jnp.concatenate([bd(pr["bt"]), bd(pr["kt"]), pr["st"].astype(BF16)], axis=0)
        prod = _ntdot(pr["lhs"], rhs_t)
        pr["a_ab"] = jnp.where(strict, prod[:L, :LANES], 0.0)
        pr["a_rb"] = jnp.where(incl, prod[L:, :LANES], 0.0)
        a_ak = jnp.where(strict, prod[:L, LANES:2 * LANES], 0.0)
        a_rk = jnp.where(incl, prod[L:, LANES:2 * LANES], 0.0)
        pr["sp"] = prod[:, 2 * LANES:]
        pr["a_k"] = jnp.concatenate([a_ak, a_rk], axis=0)
    for pr in pairs:
        pr["spv"] = pr["sp"] + _bdot(pr["a_k"], bd(pr["v"]))
    for pr in pairs:
        x = _bdot(jnp.concatenate([pr["a_ab"], pr["a_rb"]], axis=0), bd(pr["a_ab"]))
        pr["tinv"] = eye + pr["a_ab"]
        pr["pw"] = x[:L]
        pr["rt"] = pr["a_rb"] + x[L:]
    for lvl in range(1, n_double + 1):
        last = lvl == n_double
        for pr in pairs:
            rows = [pr["tinv"], pr["rt"]] if last else [pr["tinv"], pr["rt"], pr["pw"]]
            x = _bdot(jnp.concatenate(rows, axis=0), bd(pr["pw"]))
            pr["tinv"] = pr["tinv"] + x[:L]
            pr["rt"] = pr["rt"] + x[L:2 * L]
            if not last:
                pr["pw"] = x[2 * L:]
    for pr in pairs:
        z = _bdot(jnp.concatenate([pr["tinv"], pr["rt"]], axis=0), bd(pr["spv"][:L]))
        pr["u"] = z[:L]
        pr["y"] = pr["spv"][L:] + z[L:]
    for p, pr in enumerate(pairs):
        y_ref[:, pr["sl"]] = pr["y"].astype(BF16)
        uv = jnp.concatenate([pr["u"], pr["v"]], axis=0).astype(BF16)
        bk = jnp.concatenate([pr["bt"], pr["kt"]], axis=0).astype(BF16)
        upd = lax.dot_general(uv, bk, _TN, preferred_element_type=F32)
        s_ref[p] = (pr["st"] + jnp.where(same_head, upd, 0.0)) * pr["wend"]


def _wkv(r, k, v, ld, a, k_k, k_a, vres, seq_starts, seq_ends):
    T, D = r.shape
    L = WKV_CHUNK
    npairs = D // LANES
    assert 2 * L == LANES and RWKV_HEAD == L
    G = math.gcd(npairs, WKV_PAIRS_PER_STEP)
    nc = T // L
    W = G * LANES
    resets_f = tuple(s // L for s in seq_starts)
    resets_r = tuple(e // L - 1 for e in seq_ends)
    has_vres = vres is not None

    def cidx(d, c):
        return c + d * (nc - 1 - 2 * c)

    tok = pl.BlockSpec((L, W), lambda d, p, c: (cidx(d, c), p))
    tokd = pl.BlockSpec((None, L, W), lambda d, p, c: (d, cidx(d, c), p))
    par = pl.BlockSpec((1, W), lambda d, p, c: (0, p))
    args = [r, k, v, ld, a, k_k, k_a]
    in_specs = [tok, tok, tok, tokd, tokd, par, par]
    if has_vres:
        args += list(vres)
        in_specs += [tok, tok]
    return pl.pallas_call(
        functools.partial(_wkv_kernel, L, G, nc, resets_f, resets_r, has_vres),
        grid=(2, npairs // G, nc),
        in_specs=in_specs,
        out_specs=tokd,
        out_shape=jax.ShapeDtypeStruct((2, T, D), BF16),
        scratch_shapes=[pltpu.VMEM((G, LANES, LANES), F32)],
        compiler_params=_params(("arbitrary", "arbitrary", "arbitrary")),
    )(*args)


def _rwkv_out_kernel(has_vres, *refs):
    (x_ref, y_ref, r_ref, k_ref, v_ref, a_ref, g_ref, ka_ref, rk_ref, gnw_ref, gnb_ref,
     wo_ref, gpost_ref) = refs[:13]
    refs = refs[13:]
    if has_vres:
        vf_ref, vg_ref = refs[:2]
        refs = refs[2:]
    o_ref, act_ref = refs
    D = x_ref.shape[1]

    bi = lax.broadcasted_iota(jnp.int32, (LANES, LANES), 0) // RWKV_HEAD
    bj = lax.broadcasted_iota(jnp.int32, (LANES, LANES), 1) // RWKV_HEAD
    head_ones = (bi == bj).astype(BF16)
    inv_n = 1.0 / RWKV_HEAD

    slabs = [slice(s * LANES, (s + 1) * LANES) for s in range(D // LANES)]
    ys = [y_ref[0, :, sl].astype(F32) + y_ref[1, :, sl].astype(F32) for sl in slabs]
    means = [_bdot(y, head_ones) * inv_n for y in ys]
    dys = [y - mu for y, mu in zip(ys, means)]
    vars_ = [_bdot(dy * dy, head_ones) * inv_n for dy in dys]
    rks = []
    for sl in slabs:
        a_mean = 0.5 * (a_ref[0, :, sl].astype(F32) + a_ref[1, :, sl].astype(F32))
        k_mean = k_ref[:, sl].astype(F32) * (1.0 + (a_mean - 1.0) * ka_ref[:, sl])
        rks.append(_bdot(r_ref[:, sl].astype(F32) * k_mean * rk_ref[:, sl], head_ones))
    h = None
    slabs_per_dot = MXU_COLS // LANES
    for s, (sl, dy, var, rk) in enumerate(zip(slabs, dys, vars_, rks)):
        yn = dy * lax.rsqrt(var + GN_EPS) * gnw_ref[:, sl] + gnb_ref[:, sl]
        v_ = v_ref[:, sl].astype(F32)
        if has_vres:
            v_ = v_ + (vf_ref[:, sl].astype(F32) - v_) * vg_ref[:, sl].astype(F32)
        act_ref[:, sl] = ((yn + rk * v_) * g_ref[:, sl].astype(F32)).astype(BF16)
        if (s + 1) % slabs_per_dot == 0:
            ks = slice((s + 1 - slabs_per_dot) * LANES, (s + 1) * LANES)
            part = jnp.dot(act_ref[:, ks], wo_ref[ks, :], preferred_element_type=F32)
            h = part if h is None else h + part
    o_ref[...] = x_ref[...] + _rms(h, gpost_ref[...])


def _rwkv_out(x, y, r, k, v, a, g, k_a, r_k, gn_w, gn_b, w_o, gpost, vres):
    T, D = x.shape
    tm = _tile(T, RWKV_OUT_ROWS, BF16_ROWS)
    has_vres = vres is not None
    tok = pl.BlockSpec((tm, D), lambda i: (i, 0))
    tok2 = pl.BlockSpec((2, tm, D), lambda i: (0, i, 0))
    par = pl.BlockSpec((1, D), lambda i: (0, 0))
    args = [x, y, r, k, v, a, g, k_a, r_k, gn_w, gn_b, w_o, gpost]
    in_specs = [tok, tok2, tok, tok, tok, tok2, tok, par, par, par, par,
                pl.BlockSpec((D, D), lambda i: (0, 0)), par]
    if has_vres:
        args += list(vres)
        in_specs += [tok, tok]
    return pl.pallas_call(
        functools.partial(_rwkv_out_kernel, has_vres),
        grid=(T // tm,),
        in_specs=in_specs,
        out_specs=tok,
        out_shape=jax.ShapeDtypeStruct((T, D), F32),
        scratch_shapes=[pltpu.VMEM((tm, D), BF16)],
        compiler_params=_params(("parallel",)),
    )(*args)


def _qkv_kernel(tm, tn, deint, x_ref, gpre_ref, w_ref, cos_ref, sin_ref, o_ref, xn_ref, *acc):
    j = pl.program_id(1)

    @pl.when(j == 0)
    def _():
        def put(rows, y):
            xn_ref[rows, :] = y.astype(BF16)
        _rms_rows(x_ref, gpre_ref[...], put)

    cos = cos_ref[...]
    sin = sin_ref[...]
    xn = xn_ref[...]
    per_dot = MXU_COLS // ATT_HEAD
    for s in range(tn // ATT_HEAD):
        sl = slice(s * ATT_HEAD, (s + 1) * ATT_HEAD)
        if s % per_dot == 0:
            prod = jnp.dot(xn, w_ref[:, s * ATT_HEAD:s * ATT_HEAD + MXU_COLS],
                           preferred_element_type=F32)
        t = prod[:, (s % per_dot) * ATT_HEAD:(s % per_dot + 1) * ATT_HEAD]
        rot = t * cos + pltpu.roll(t, ATT_HEAD // 2, 1) * sin
        if deint == 1:
            o_ref[:, sl] = rot.astype(BF16)
        else:
            acc_ref, = acc
            slot = s % acc_ref.shape[0]
            acc_ref[slot] = rot
            for c in range(deint):
                o_ref[c, :, sl] = acc_ref[slot, pl.ds(c, tm // deint, stride=deint), :].astype(BF16)


def _rope_tables(pos):
    half = ROPE_DIM // 2
    inv = ROPE_THETA ** (-jnp.arange(half, dtype=F32) / half)
    ang = pos.astype(F32)[:, None] * inv[None, :]
    cos, sin = jnp.cos(ang), jnp.sin(ang)
    T = pos.shape[0]
    gap = ATT_HEAD // 2 - half
    ones = jnp.ones((T, gap), F32)
    zeros = jnp.zeros((T, gap), F32)
    cos_t = jnp.concatenate([cos, ones, cos, ones], axis=1)
    sin_t = jnp.concatenate([-sin, zeros, sin, zeros], axis=1)
    return (jnp.stack([cos_t, jnp.ones_like(cos_t)]), jnp.stack([sin_t, jnp.zeros_like(sin_t)]))


def _rope_perm_kernel(w_ref, o_ref):
    kind = pl.program_id(2) % 3

    @pl.when(kind == 2)
    def _():
        o_ref[...] = w_ref[...].astype(BF16)

    @pl.when(kind < 2)
    def _():
        half = ROPE_DIM // 2
        mid = ATT_HEAD // 2
        lane = lax.broadcasted_iota(jnp.int32, (w_ref.shape[0], ATT_HEAD), 1)
        from_hi = (lane >= half) & (lane < 2 * half)
        from_lo = (lane >= mid) & (lane < mid + half)
        for c0 in range(0, w_ref.shape[1], ATT_HEAD):
            w = w_ref[:, c0:c0 + ATT_HEAD]
            swapped = jnp.where(from_hi, pltpu.roll(w, ATT_HEAD - (mid - half), 1),
                                jnp.where(from_lo, pltpu.roll(w, mid - half, 1), w))
            o_ref[:, c0:c0 + ATT_HEAD] = swapped.astype(BF16)


def _rope_perm(w_qkv, D):
    n, K, N = w_qkv.shape
    tr = _tile(K, 512)
    spec = pl.BlockSpec((None, tr, D), lambda l, i, j: (l, i, j))
    return pl.pallas_call(
        _rope_perm_kernel,
        grid=(n, K // tr, N // D),
        in_specs=[spec],
        out_specs=spec,
        out_shape=jax.ShapeDtypeStruct(w_qkv.shape, BF16),
        compiler_params=_params(("parallel", "parallel", "arbitrary")),
    )(w_qkv)


def _qkv(x, gpre, w, layer, tables, gi, dil):
    T, D = x.shape
    tn = _tile(D, QKV_COLS, MXU_COLS)
    nj = 3 * D // tn

    def plane(j):
        return (j * tn) // (2 * D)

    tm = _tile(T, QKV_ROWS, BF16_ROWS * dil)
    tab = pl.BlockSpec((None, tm, ATT_HEAD), lambda i, j: (plane(j), i, 0))
    if dil == 1:
        out = pl.BlockSpec((None, tm, tn), lambda i, j: (0, i, j))
        deint_scratch = []
    else:
        out = pl.BlockSpec((dil, tm // dil, tn), lambda i, j: (0, i, j))
        deint_scratch = [pltpu.VMEM((2, tm, ATT_HEAD), F32)]
    return pl.pallas_call(
        functools.partial(_qkv_kernel, tm, tn, dil),
        grid=(T // tm, nj),
        in_specs=[
            pl.BlockSpec((tm, D), lambda i, j: (i, 0)),
            pl.BlockSpec((1, D), lambda i, j: (0, 0)),
            pl.BlockSpec((None, D, tn), lambda i, j: (layer, 0, gi * nj + j)),
            tab, tab,
        ],
        out_specs=out,
        out_shape=jax.ShapeDtypeStruct((dil, T // dil, 3 * D), BF16),
        scratch_shapes=[pltpu.VMEM((tm, D), BF16)] + deint_scratch,
        compiler_params=_params(("parallel", "arbitrary")),
    )(x, gpre, w, *tables)


def _attn_kernel(bq, n_heads, bounds, q_ref, kp_ref, kc_ref, kn_ref, vp_ref, vc_ref, vn_ref,
                 o_ref, lse_ref):
    row0 = pl.program_id(0) * bq
    lo = jnp.int32(bounds[0])
    hi = jnp.int32(bounds[-1])
    for b in bounds[1:-1]:
        lo = jnp.where(row0 >= b, b, lo)
    for b in reversed(bounds[1:-1]):
        hi = jnp.where(row0 < b, b, hi)
    nk = bq + 2 * ATT_HALF
    qrow = row0 + lax.broadcasted_iota(jnp.int32, (bq, nk), 0)
    krow = row0 - ATT_HALF + lax.broadcasted_iota(jnp.int32, (bq, nk), 1)
    valid = (jnp.abs(krow - qrow) <= ATT_HALF) & (krow >= lo) & (krow < hi)
    scale = ATT_HEAD ** -0.5
    lane = lax.broadcasted_iota(jnp.int32, (bq, LANES), 1)
    lse_all = jnp.zeros((bq, LANES), F32)
    ones = jnp.ones((nk, ATT_HEAD), BF16)
    for h in range(n_heads):
        sl = slice(h * ATT_HEAD, (h + 1) * ATT_HEAD)
        q = q_ref[:, sl]
        kcat = jnp.concatenate([kp_ref[:, sl], kc_ref[:, sl], kn_ref[:, sl]], axis=0)
        vcat = jnp.concatenate([vp_ref[:, sl], vc_ref[:, sl], vn_ref[:, sl]], axis=0)
        s = lax.dot_general(q, kcat, _NT, preferred_element_type=F32) * (scale * LOG2E)
        s = jnp.where(valid, s, -1e30)
        m = jnp.max(s, axis=-1, keepdims=True)
        p = jnp.exp2(s - m).astype(BF16)
        ol = jnp.dot(p, jnp.concatenate([vcat, ones], axis=1), preferred_element_type=F32)
        l = ol[:, ATT_HEAD:]
        o_ref[:, sl] = (ol[:, :ATT_HEAD] / l).astype(BF16)
        lse_all = jnp.where(lane == h, (m + jnp.log2(l)) * (1.0 / LOG2E), lse_all)
    lse_ref[...] = lse_all


def _attn_group(qkv, D, seq_starts, seq_ends):
    dil, rows, _ = qkv.shape
    bounds = tuple(s // dil for s in seq_starts) + (seq_ends[-1] // dil,)
    seg = [b - a for a, b in zip(bounds[:-1], bounds[1:])]
    bq = ATTN_QUERY_ROWS
    for s in seg:
        bq = math.gcd(bq, s)
    assert bq >= ATT_HALF, "sequence too short for the dilated band blocks"
    nb = rows // bq

    per = bq // ATT_HALF
    nh = rows // ATT_HALF

    def cur(kind):
        return pl.BlockSpec((None, bq, D), lambda i, c: (c, i, kind))

    def halo(kind, after):
        def imap(i, c):
            blk = (i + 1) * per if after else i * per - 1
            return (c, jnp.clip(blk, 0, nh - 1), kind)
        return pl.BlockSpec((None, ATT_HALF, D), imap)

    return pl.pallas_call(
        functools.partial(_attn_kernel, bq, D // ATT_HEAD, bounds),
        grid=(nb, dil),
        in_specs=[cur(0), halo(1, False), cur(1), halo(1, True),
                  halo(2, False), cur(2), halo(2, True)],
        out_specs=[pl.BlockSpec((None, bq, D), lambda i, c: (c, i, 0)),
                   pl.BlockSpec((None, bq, LANES), lambda i, c: (c, i, 0))],
        out_shape=[jax.ShapeDtypeStruct((dil, rows, D), BF16),
                   jax.ShapeDtypeStruct((dil, rows, LANES), F32)],
        compiler_params=_params(("parallel", "arbitrary")),
    )(qkv, qkv, qkv, qkv, qkv, qkv, qkv)


def _attn_out_kernel(tm, x_ref, o0_ref, o1_ref, o2_ref, l0_ref, l1_ref, l2_ref, wo_ref, gpost_ref,
                     out_ref, act_ref, *scratch):
    D = x_ref.shape[1]

    def token_major(ref, sl, s):
        dil = ref.shape[0]
        if dil == 1:
            return ref[0, :, sl].astype(F32)
        for c in range(dil):
            s[pl.ds(c, tm // dil, stride=dil), :] = ref[c, :, sl].astype(F32)
        return s[...]

    free = list(scratch)
    l0, l1, l2 = [token_major(r, slice(0, LANES), free.pop() if r.shape[0] > 1 else None)
                  for r in (l0_ref, l1_ref, l2_ref)]
    m = jnp.maximum(jnp.maximum(l0, l1), l2)
    e0, e1, e2 = jnp.exp(l0 - m), jnp.exp(l1 - m), jnp.exp(l2 - m)
    inv = 1.0 / (e0 + e1 + e2)
    w0, w1, w2 = e0 * inv, e1 * inv, e2 * inv
    h = None
    heads_per_dot = MXU_COLS // ATT_HEAD
    for k in range(D // ATT_HEAD):
        sl = slice(k * ATT_HEAD, (k + 1) * ATT_HEAD)
        pool = list(free)
        o0, o1, o2 = [token_major(r, sl, pool.pop() if r.shape[0] > 1 else None)
                      for r in (o0_ref, o1_ref, o2_ref)]
        act_ref[:, sl] = (w0[:, k:k + 1] * o0 + w1[:, k:k + 1] * o1
                          + w2[:, k:k + 1] * o2).astype(BF16)
        if (k + 1) % heads_per_dot == 0:
            ks = slice((k + 1 - heads_per_dot) * ATT_HEAD, (k + 1) * ATT_HEAD)
            part = jnp.dot(act_ref[:, ks], wo_ref[ks, :], preferred_element_type=F32)
            h = part if h is None else h + part
    out_ref[...] = x_ref[...] + _rms(h, gpost_ref[...])


def _attn_out(x, os_, lses, w_o, gpost):
    T, D = x.shape
    tm = _tile(T, ATTN_OUT_ROWS, BF16_ROWS * max(o.shape[0] for o in os_))
    tok = pl.BlockSpec((tm, D), lambda i: (i, 0))

    def planes(a):
        dil = a.shape[0]
        return pl.BlockSpec((dil, tm // dil, a.shape[2]), lambda i: (0, i, 0))

    n_scratch = sum(2 for o in os_ if o.shape[0] > 1)
    return pl.pallas_call(
        functools.partial(_attn_out_kernel, tm),
        grid=(T // tm,),
        in_specs=[tok] + [planes(a) for a in os_] + [planes(a) for a in lses]
        + [pl.BlockSpec((D, D), lambda i: (0, 0)), pl.BlockSpec((1, D), lambda i: (0, 0))],
        out_specs=tok,
        out_shape=jax.ShapeDtypeStruct((T, D), F32),
        scratch_shapes=[pltpu.VMEM((tm, D), BF16)] + [pltpu.VMEM((tm, LANES), F32)] * n_scratch,
        compiler_params=_params(("parallel",)),
    )(x, *os_, *lses, w_o, gpost)


def kernel(x_prompt, x_sample, norm_pre, norm_post, ffn_w_gate, ffn_w_up, ffn_w_down, rwkv_mu, rwkv_w_rkv, rwkv_w0, rwkv_w1, rwkv_w2, rwkv_a0, rwkv_a1, rwkv_a2, rwkv_v0, rwkv_v1, rwkv_v2, rwkv_g1, rwkv_g2, rwkv_k_k, rwkv_k_a, rwkv_r_k, rwkv_gn_w, rwkv_gn_b, rwkv_w_o, attn_w_qkv, attn_w_o):
    D = x_prompt.shape[-1]
    depth = norm_pre.shape[0]
    seqs = [x_prompt.shape[1]] * x_prompt.shape[0] + [x_sample.shape[1]] * x_sample.shape[0]
    seq_starts, seq_ends, t = [], [], 0
    for s in seqs:
        seq_starts.append(t)
        t += s
        seq_ends.append(t)
    seq_starts, seq_ends = tuple(seq_starts), tuple(seq_ends)
    x = jnp.concatenate([x_prompt.reshape(-1, D), x_sample.reshape(-1, D)], axis=0)
    pos = jnp.concatenate([jnp.arange(s, dtype=jnp.int32) for s in seqs])
    tables = _rope_tables(pos)

    def vec(a):
        return a.reshape(1, D)

    wg, wu, wd = ffn_w_gate.astype(BF16), ffn_w_up.astype(BF16), ffn_w_down.astype(BF16)
    w_rkv = rwkv_w_rkv.astype(BF16)
    w_qkv = _rope_perm(attn_w_qkv, D)
    v_first = None
    for layer in range(depth):
        x = _ffn(x, vec(norm_pre[layer, 0]), wg, wu, wd, vec(norm_post[layer, 0]), layer, 0)
        j = layer // 2
        if layer % 2 == 0:
            vres_w = None if j == 0 else (rwkv_v0[j - 1], rwkv_v1[j - 1], rwkv_v2[j - 1])
            pre = _rwkv_pre(x, seq_starts, seq_ends, vec(norm_pre[layer, 1]), rwkv_mu[j],
                            rwkv_w0[j], rwkv_w1[j], rwkv_w2[j], rwkv_a0[j], rwkv_a1[j], rwkv_a2[j],
                            rwkv_g1[j], rwkv_g2[j], vres_w)
            xr, xk, xv, ld, a, g = pre[:6]
            r = _mm(xr, w_rkv, (j, 0))
            k = _mm(xk, w_rkv, (j, 1))
            v = _mm(xv, w_rkv, (j, 2))
            if j == 0:
                v_first, vres = v, None
            else:
                vres = (v_first, pre[6])
            y = _wkv(r, k, v, ld, a, vec(rwkv_k_k[j]), vec(rwkv_k_a[j]), vres, seq_starts, seq_ends)
            x = _rwkv_out(x, y, r, k, v, a, g, vec(rwkv_k_a[j]), vec(rwkv_r_k[j]),
                          vec(rwkv_gn_w[j]), vec(rwkv_gn_b[j]), rwkv_w_o[j].astype(BF16),
                          vec(norm_post[layer, 1]), vres)
        else:
            os_, lses = [], []
            for gi, (_, dil) in enumerate(DILATED_GROUPS):
                qkv = _qkv(x, vec(norm_pre[layer, 1]), w_qkv, j, tables, gi, dil)
                o, lse = _attn_group(qkv, D, seq_starts, seq_ends)
                os_.append(o)
                lses.append(lse)
            x = _attn_out(x, os_, lses, attn_w_o[j].astype(BF16), vec(norm_post[layer, 1]))
        x = _ffn(x, vec(norm_pre[layer, 2]), wg, wu, wd, vec(norm_post[layer, 2]), layer, 1)
    n_p = x_prompt.shape[0] * x_prompt.shape[1]
    return (x[:n_p].reshape(x_prompt.shape), x[n_p:].reshape(x_sample.shape))
```

```python
import functools
import math

import jax
import jax.numpy as jnp
from jax import lax
from jax.experimental import pallas as pl
from jax.experimental.pallas import tpu as pltpu

F32 = jnp.float32
BF16 = jnp.bfloat16

NORM_EPS = 1e-6
GN_EPS = 64e-5
RWKV_HEAD = 64
ATT_HEAD = 128
ROPE_DIM = ATT_HEAD // 4
ROPE_THETA = 500000.0
DILATED_GROUPS = ((128, 1), (512, 4), (2048, 16))
N_GROUPS = len(DILATED_GROUPS)
ATT_HALF = 64
LANES = 128
MXU_COLS = 256
LOG2E = math.log2(math.e)
WKV_CHUNK = 64
WKV_PAIRS_PER_STEP = 16
VMEM_LIMIT = 56 * 1024 * 1024
BF16_ROWS = 16

FFN_ROWS, FFN_COLS = 768, 512
MM_ROWS, MM_COLS = 1024, 1024
QKV_ROWS, QKV_COLS = 1024, 2048
RWKV_PRE_ROWS = 256
RWKV_OUT_ROWS = 256
ATTN_OUT_ROWS = 256
ATTN_QUERY_ROWS = 128

_NT = (((1,), (1,)), ((), ()))
_TN = (((0,), (0,)), ((), ()))


def _tile(n, pref, mult=8):
    if n <= pref:
        return n
    t = (pref // mult) * mult
    while t >= mult:
        if n % t == 0:
            return t
        t -= mult
    return n


def _in_list(v, lst):
    r = v == lst[0]
    for s in lst[1:]:
        r = r | (v == s)
    return r


def _rms(x, g):
    return x * lax.rsqrt(jnp.mean(x * x, axis=-1, keepdims=True) + NORM_EPS) * g


NORM_ROWS = 32


def _rms_rows(src_ref, g, emit):
    for r0 in range(0, src_ref.shape[0], NORM_ROWS):
        rows = slice(r0, min(r0 + NORM_ROWS, src_ref.shape[0]))
        emit(rows, _rms(src_ref[rows, :], g))


def _bdot(a, b):
    return jnp.dot(a.astype(BF16), b.astype(BF16), preferred_element_type=F32)


def _ntdot(a, b):
    return lax.dot_general(a, b, _NT, preferred_element_type=F32)


def _params(sem):
    return pltpu.CompilerParams(dimension_semantics=sem, vmem_limit_bytes=VMEM_LIMIT)


def _ffn_kernel(x_ref, gpre_ref, wg_ref, wu_ref, wd_ref, gpost_ref, o_ref, xn_ref, acc_ref):
    j = pl.program_id(1)

    @pl.when(j == 0)
    def _():
        def put(rows, y):
            xn_ref[rows, :] = y.astype(BF16)
        _rms_rows(x_ref, gpre_ref[...], put)
        acc_ref[...] = jnp.zeros_like(acc_ref)

    xn = xn_ref[...]
    g = jnp.dot(xn, wg_ref[...], preferred_element_type=F32)
    u = jnp.dot(xn, wu_ref[...], preferred_element_type=F32)
    h = (g * jax.nn.sigmoid(g)) * u
    acc_ref[...] += jnp.dot(h.astype(BF16), wd_ref[...], preferred_element_type=F32)

    @pl.when(j == pl.num_programs(1) - 1)
    def _():
        def put(rows, y):
            o_ref[rows, :] = x_ref[rows, :] + y
        _rms_rows(acc_ref, 0.5 * gpost_ref[...], put)


def _ffn(x, gpre, wg, wu, wd, gpost, layer, k):
    T, D = x.shape
    F = wg.shape[-1]
    tm = _tile(T, FFN_ROWS, MXU_COLS)
    tf = _tile(F, FFN_COLS, LANES)
    return pl.pallas_call(
        _ffn_kernel,
        grid=(T // tm, F // tf),
        in_specs=[
            pl.BlockSpec((tm, D), lambda i, j: (i, 0)),
            pl.BlockSpec((1, D), lambda i, j: (0, 0)),
            pl.BlockSpec((None, None, D, tf), lambda i, j: (layer, k, 0, j)),
            pl.BlockSpec((None, None, D, tf), lambda i, j: (layer, k, 0, j)),
            pl.BlockSpec((None, None, tf, D), lambda i, j: (layer, k, j, 0)),
            pl.BlockSpec((1, D), lambda i, j: (0, 0)),
        ],
        out_specs=pl.BlockSpec((tm, D), lambda i, j: (i, 0)),
        out_shape=jax.ShapeDtypeStruct((T, D), F32),
        scratch_shapes=[pltpu.VMEM((tm, D), BF16), pltpu.VMEM((tm, D), F32)],
        compiler_params=_params(("parallel", "arbitrary")),
    )(x, gpre, wg, wu, wd, gpost)


def _mm_kernel(a_ref, w_ref, o_ref):
    a = a_ref[...]
    for c in range(0, o_ref.shape[1], MXU_COLS):
        o_ref[:, c:c + MXU_COLS] = jnp.dot(
            a, w_ref[:, c:c + MXU_COLS], preferred_element_type=F32).astype(o_ref.dtype)


def _mm(a, w, lead):
    M, K = a.shape
    N = w.shape[-1]
    tm = _tile(M, MM_ROWS, BF16_ROWS)
    tn = _tile(N, MM_COLS, MXU_COLS)
    return pl.pallas_call(
        _mm_kernel,
        grid=(M // tm, N // tn),
        in_specs=[
            pl.BlockSpec((tm, K), lambda i, j: (i, 0)),
            pl.BlockSpec((None,) * len(lead) + (K, tn), lambda i, j: tuple(lead) + (0, j)),
        ],
        out_specs=pl.BlockSpec((tm, tn), lambda i, j: (i, j)),
        out_shape=jax.ShapeDtypeStruct((M, N), BF16),
        compiler_params=_params(("parallel", "arbitrary")),
    )(a, w)


def _rwkv_pre_kernel(seq_starts, seq_ends, has_vres, tm, *refs):
    (x_ref, xp_ref, xn_ref, gpre_ref, mu_ref, w0_ref, w1_ref, w2_ref,
     a0_ref, a1_ref, a2_ref, g1_ref, g2_ref) = refs[:13]
    refs = refs[13:]
    if has_vres:
        v0_ref, v1_ref, v2_ref = refs[:3]
        refs = refs[3:]
    xr_ref, xk_ref, xv_ref, ld_ref, a_ref, g_ref = refs[:6]

    t0 = pl.program_id(0) * tm
    gpre = gpre_ref[...]
    h = _rms(x_ref[...], gpre)
    keep_p = 1.0 - _in_list(t0, seq_starts).astype(F32)
    keep_n = 1.0 - _in_list(t0 + tm, seq_ends).astype(F32)
    hp = _rms(xp_ref[7:8, :], gpre) * keep_p
    hn = _rms(xn_ref[0:1, :], gpre) * keep_n
    row = lax.broadcasted_iota(jnp.int32, h.shape, 0)
    prev = jnp.where(row == 0, hp, pltpu.roll(h, 1, 0))
    nxt = jnp.where(row == tm - 1, hn, pltpu.roll(h, tm - 1, 0))
    xx = 0.5 * (prev + nxt) - h

    def mix(i):
        return (h + xx * mu_ref[i:i + 1, :]).astype(BF16)

    xr_ref[...] = mix(0)
    xw = mix(1)
    xk_ref[...] = mix(2)
    xv = mix(3)
    xv_ref[...] = xv
    xa = mix(4)
    xg = mix(5)
    for d in range(2):
        tw = jnp.tanh(jnp.dot(xw, w1_ref[d], preferred_element_type=F32))
        z = w0_ref[d:d + 1, :] + _bdot(tw, w2_ref[d])
        ld_ref[d] = (-math.exp(-0.5) * jax.nn.sigmoid(z)).astype(BF16)
        ta = jnp.dot(xa, a1_ref[d], preferred_element_type=F32)
        a_ref[d] = jax.nn.sigmoid(a0_ref[d:d + 1, :] + _bdot(ta, a2_ref[d])).astype(BF16)
    tg = jax.nn.sigmoid(jnp.dot(xg, g1_ref[...], preferred_element_type=F32))
    g_ref[...] = _bdot(tg, g2_ref[...]).astype(BF16)
    if has_vres:
        vg_ref = refs[6]
        tv = jnp.dot(xv, v1_ref[...], preferred_element_type=F32)
        vg_ref[...] = jax.nn.sigmoid(v0_ref[...] + _bdot(tv, v2_ref[...])).astype(BF16)


def _pad_lora(w_in, w_out):
    r = w_in.shape[-1]
    rp = -(-r // LANES) * LANES
    pin = [(0, 0)] * (w_in.ndim - 1) + [(0, rp - r)]
    pout = [(0, 0)] * (w_out.ndim - 2) + [(0, rp - r), (0, 0)]
    return jnp.pad(w_in, pin).astype(BF16), jnp.pad(w_out, pout).astype(BF16)


def _rwkv_pre(x, seq_starts, seq_ends, gpre, mu, w0, w1, w2, a0, a1, a2, g1, g2, vres):
    T, D = x.shape
    tm = _tile(T, RWKV_PRE_ROWS, BF16_ROWS)
    nb8 = T // 8
    has_vres = vres is not None
    w1p, w2p = _pad_lora(w1, w2)
    a1p, a2p = _pad_lora(a1, a2)
    g1p, g2p = _pad_lora(g1, g2)

    def full(a):
        nd = a.ndim
        return pl.BlockSpec(a.shape, lambda i, _nd=nd: (0,) * _nd)

    args = [x, x, x, gpre, mu, w0, w1p, w2p, a0, a1p, a2p, g1p, g2p]
    in_specs = [
        pl.BlockSpec((tm, D), lambda i: (i, 0)),
        pl.BlockSpec((8, D), lambda i: (jnp.maximum(i * (tm // 8) - 1, 0), 0)),
        pl.BlockSpec((8, D), lambda i: (jnp.minimum((i + 1) * (tm // 8), nb8 - 1), 0)),
    ] + [full(a) for a in args[3:]]
    if has_vres:
        v0, v1, v2 = vres
        v1p, v2p = _pad_lora(v1, v2)
        extra = [v0.reshape(1, D), v1p, v2p]
        args += extra
        in_specs += [full(a) for a in extra]
    tok = pl.BlockSpec((tm, D), lambda i: (i, 0))
    tok2 = pl.BlockSpec((2, tm, D), lambda i: (0, i, 0))
    out_specs = [tok, tok, tok, tok2, tok2, tok]
    out_shape = [jax.ShapeDtypeStruct((T, D), BF16)] * 3 + [
        jax.ShapeDtypeStruct((2, T, D), BF16), jax.ShapeDtypeStruct((2, T, D), BF16),
        jax.ShapeDtypeStruct((T, D), BF16)]
    if has_vres:
        out_specs.append(tok)
        out_shape.append(jax.ShapeDtypeStruct((T, D), BF16))
    return pl.pallas_call(
        functools.partial(_rwkv_pre_kernel, seq_starts, seq_ends, has_vres, tm),
        grid=(T // tm,),
        in_specs=in_specs,
        out_specs=out_specs,
        out_shape=out_shape,
        compiler_params=_params(("parallel",)),
    )(*args)


def _wkv_kernel(L, G, nc, resets_f, resets_r, has_vres, *refs):
    r_ref, k_ref, v_ref, ld_ref, a_ref, kk_ref, ka_ref = refs[:7]
    refs = refs[7:]
    if has_vres:
        vf_ref, vg_ref = refs[:2]
        refs = refs[2:]
    y_ref, s_ref = refs

    d = pl.program_id(0)
    c = pl.program_id(2)
    ci = c + d * (nc - 1 - 2 * c)
    reset = ((d == 0) & _in_list(ci, resets_f)) | ((d == 1) & _in_list(ci, resets_r))

    @pl.when(reset)
    def _():
        s_ref[...] = jnp.zeros_like(s_ref)

    sgn = 1 - 2 * d
    tt = lax.broadcasted_iota(jnp.int32, (L, LANES), 0)
    ss = lax.broadcasted_iota(jnp.int32, (L, LANES), 1) % L
    delta = sgn * (tt - ss)
    incl = delta >= 0
    strict = delta > 0
    eye = (tt == ss).astype(F32)
    tri = (sgn * (lax.broadcasted_iota(jnp.int32, (L, L), 0)
                  - lax.broadcasted_iota(jnp.int32, (L, L), 1)) >= 0).astype(BF16)
    first_head = lax.broadcasted_iota(jnp.int32, (L, LANES), 1) < RWKV_HEAD
    bi = lax.broadcasted_iota(jnp.int32, (LANES, LANES), 0) // RWKV_HEAD
    bj = lax.broadcasted_iota(jnp.int32, (LANES, LANES), 1) // RWKV_HEAD
    same_head = bi == bj
    head_ones = same_head.astype(BF16)
    n_double = int(math.log2(L)) - 1

    def bd(x):
        xb = x.astype(BF16)
        zero = jnp.zeros_like(xb)
        return jnp.concatenate([jnp.where(first_head, xb, zero), jnp.where(first_head, zero, xb)], axis=0)

    kkcs = [k_ref[:, p * LANES:(p + 1) * LANES].astype(F32) * kk_ref[:, p * LANES:(p + 1) * LANES]
            for p in range(G)]
    n2_all = jnp.dot(jnp.concatenate([(kkc * kkc).astype(BF16) for kkc in kkcs], axis=0), head_ones,
                     preferred_element_type=F32)
    pairs = []
    for p in range(G):
        sl = slice(p * LANES, (p + 1) * LANES)
        r_ = r_ref[:, sl].astype(F32)
        k_ = k_ref[:, sl].astype(F32)
        v_ = v_ref[:, sl].astype(F32)
        ld_b = ld_ref[:, sl]
        ld_ = ld_b.astype(F32)
        a_ = a_ref[:, sl].astype(F32)
        if has_vres:
            v_ = v_ + (vf_ref[:, sl].astype(F32) - v_) * vg_ref[:, sl].astype(F32)
        kkn = kkcs[p] / jnp.maximum(jnp.sqrt(n2_all[p * L:(p + 1) * L]), 1e-12)
        kd = k_ * (1.0 + (a_ - 1.0) * ka_ref[:, sl])
        cs = jnp.dot(tri, ld_b, preferred_element_type=F32)
        w_inv = jnp.exp(-cs)
        bt = kkn * a_ * w_inv
        kt = kd * w_inv
        pairs.append(dict(
            sl=sl, v=v_, bt=bt, kt=kt, st=s_ref[p],
            lhs=jnp.concatenate([-kkn * jnp.exp(cs - ld_), r_ * jnp.exp(cs)], axis=0).astype(BF16),
            wend=jnp.exp(jnp.sum(ld_, axis=0, keepdims=True))))

    for pr in pairs:
        rhs_t = jnp.concatenate([bd(pr["bt"]), bd(pr["kt"]), pr["st"].astype(BF16)], axis=0)
        prod = _ntdot(pr["lhs"], rhs_t)
        pr["a_ab"] = jnp.where(strict, prod[:L, :LANES], 0.0)
        pr["a_rb"] = jnp.where(incl, prod[L:, :LANES], 0.0)
        a_ak = jnp.where(strict, prod[:L, LANES:2 * LANES], 0.0)
        a_rk = jnp.where(incl, prod[L:, LANES:2 * LANES], 0.0)
        pr["sp"] = prod[:, 2 * LANES:]
        pr["a_k"] = jnp.concatenate([a_ak, a_rk], axis=0)
    for pr in pairs:
        pr["spv"] = pr["sp"] + _bdot(pr["a_k"], bd(pr["v"]))
    for pr in pairs:
        x = _bdot(jnp.concatenate([pr["a_ab"], pr["a_rb"]], axis=0), bd(pr["a_ab"]))
        pr["tinv"] = eye + pr["a_ab"]
        pr["pw"] = x[:L]
        pr["rt"] = pr["a_rb"] + x[L:]
    for lvl in range(1, n_double + 1):
        last = lvl == n_double
        for pr in pairs:
            rows = [pr["tinv"], pr["rt"]] if last else [pr["tinv"], pr["rt"], pr["pw"]]
            x = _bdot(jnp.concatenate(rows, axis=0), bd(pr["pw"]))
            pr["tinv"] = pr["tinv"] + x[:L]
            pr["rt"] = pr["rt"] + x[L:2 * L]
            if not last:
                pr["pw"] = x[2 * L:]
    for pr in pairs:
        z = _bdot(jnp.concatenate([pr["tinv"], pr["rt"]], axis=0), bd(pr["spv"][:L]))
        pr["u"] = z[:L]
        pr["y"] = pr["spv"][L:] + z[L:]
    for p, pr in enumerate(pairs):
        y_ref[:, pr["sl"]] = pr["y"].astype(BF16)
        uv = jnp.concatenate([pr["u"], pr["v"]], axis=0).astype(BF16)
        bk = jnp.concatenate([pr["bt"], pr["kt"]], axis=0).astype(BF16)
        upd = lax.dot_general(uv, bk, _TN, preferred_element_type=F32)
        s_ref[p] = (pr["st"] + jnp.where(same_head, upd, 0.0)) * pr["wend"]


def _wkv(r, k, v, ld, a, k_k, k_a, vres, seq_starts, seq_ends):
    T, D = r.shape
    L = WKV_CHUNK
    npairs = D // LANES
    assert 2 * L == LANES and RWKV_HEAD == L
    G = math.gcd(npairs, WKV_PAIRS_PER_STEP)
    nc = T // L
    W = G * LANES
    resets_f = tuple(s // L for s in seq_starts)
    resets_r = tuple(e // L - 1 for e in seq_ends)
    has_vres = vres is not None

    def cidx(d, c):
        return c + d * (nc - 1 - 2 * c)

    tok = pl.BlockSpec((L, W), lambda d, p, c: (cidx(d, c), p))
    tokd = pl.BlockSpec((None, L, W), lambda d, p, c: (d, cidx(d, c), p))
    par = pl.BlockSpec((1, W), lambda d, p, c: (0, p))
    args = [r, k, v, ld, a, k_k, k_a]
    in_specs = [tok, tok, tok, tokd, tokd, par, par]
    if has_vres:
        args += list(vres)
        in_specs += [tok, tok]
    return pl.pallas_call(
        functools.partial(_wkv_kernel, L, G, nc, resets_f, resets_r, has_vres),
        grid=(2, npairs // G, nc),
        in_specs=in_specs,
        out_specs=tokd,
        out_shape=jax.ShapeDtypeStruct((2, T, D), BF16),
        scratch_shapes=[pltpu.VMEM((G, LANES, LANES), F32)],
        compiler_params=_params(("arbitrary", "arbitrary", "arbitrary")),
    )(*args)


def _rwkv_out_kernel(has_vres, *refs):
    (x_ref, y_ref, r_ref, k_ref, v_ref, a_ref, g_ref, ka_ref, rk_ref, gnw_ref, gnb_ref,
     wo_ref, gpost_ref) = refs[:13]
    refs = refs[13:]
    if has_vres:
        vf_ref, vg_ref = refs[:2]
        refs = refs[2:]
    o_ref, act_ref = refs
    D = x_ref.shape[1]

    bi = lax.broadcasted_iota(jnp.int32, (LANES, LANES), 0) // RWKV_HEAD
    bj = lax.broadcasted_iota(jnp.int32, (LANES, LANES), 1) // RWKV_HEAD
    head_ones = (bi == bj).astype(BF16)
    inv_n = 1.0 / RWKV_HEAD

    slabs = [slice(s * LANES, (s + 1) * LANES) for s in range(D // LANES)]
    ys = [y_ref[0, :, sl].astype(F32) + y_ref[1, :, sl].astype(F32) for sl in slabs]
    means = [_bdot(y, head_ones) * inv_n for y in ys]
    dys = [y - mu for y, mu in zip(ys, means)]
    vars_ = [_bdot(dy * dy, head_ones) * inv_n for dy in dys]
    rks = []
    for sl in slabs:
        a_mean = 0.5 * (a_ref[0, :, sl].astype(F32) + a_ref[1, :, sl].astype(F32))
        k_mean = k_ref[:, sl].astype(F32) * (1.0 + (a_mean - 1.0) * ka_ref[:, sl])
        rks.append(_bdot(r_ref[:, sl].astype(F32) * k_mean * rk_ref[:, sl], head_ones))
    h = None
    slabs_per_dot = MXU_COLS // LANES
    for s, (sl, dy, var, rk) in enumerate(zip(slabs, dys, vars_, rks)):
        yn = dy * lax.rsqrt(var + GN_EPS) * gnw_ref[:, sl] + gnb_ref[:, sl]
        v_ = v_ref[:, sl].astype(F32)
        if has_vres:
            v_ = v_ + (vf_ref[:, sl].astype(F32) - v_) * vg_ref[:, sl].astype(F32)
        act_ref[:, sl] = ((yn + rk * v_) * g_ref[:, sl].astype(F32)).astype(BF16)
        if (s + 1) % slabs_per_dot == 0:
            ks = slice((s + 1 - slabs_per_dot) * LANES, (s + 1) * LANES)
            part = jnp.dot(act_ref[:, ks], wo_ref[ks, :], preferred_element_type=F32)
            h = part if h is None else h + part
    o_ref[...] = x_ref[...] + _rms(h, gpost_ref[...])


def _rwkv_out(x, y, r, k, v, a, g, k_a, r_k, gn_w, gn_b, w_o, gpost, vres):
    T, D = x.shape
    tm = _tile(T, RWKV_OUT_ROWS, BF16_ROWS)
    has_vres = vres is not None
    tok = pl.BlockSpec((tm, D), lambda i: (i, 0))
    tok2 = pl.BlockSpec((2, tm, D), lambda i: (0, i, 0))
    par = pl.BlockSpec((1, D), lambda i: (0, 0))
    args = [x, y, r, k, v, a, g, k_a, r_k, gn_w, gn_b, w_o, gpost]
    in_specs = [tok, tok2, tok, tok, tok, tok2, tok, par, par, par, par,
                pl.BlockSpec((D, D), lambda i: (0, 0)), par]
    if has_vres:
        args += list(vres)
        in_specs += [tok, tok]
    return pl.pallas_call(
        functools.partial(_rwkv_out_kernel, has_vres),
        grid=(T // tm,),
        in_specs=in_specs,
        out_specs=tok,
        out_shape=jax.ShapeDtypeStruct((T, D), F32),
        scratch_shapes=[pltpu.VMEM((tm, D), BF16)],
        compiler_params=_params(("parallel",)),
    )(*args)


def _qkv_kernel(tm, tn, deint, x_ref, gpre_ref, w_ref, cos_ref, sin_ref, o_ref, xn_ref, *acc):
    j = pl.program_id(1)

    @pl.when(j == 0)
    def _():
        def put(rows, y):
            xn_ref[rows, :] = y.astype(BF16)
        _rms_rows(x_ref, gpre_ref[...], put)

    cos = cos_ref[...]
    sin = sin_ref[...]
    xn = xn_ref[...]
    per_dot = MXU_COLS // ATT_HEAD
    for s in range(tn // ATT_HEAD):
        sl = slice(s * ATT_HEAD, (s + 1) * ATT_HEAD)
        if s % per_dot == 0:
            prod = jnp.dot(xn, w_ref[:, s * ATT_HEAD:s * ATT_HEAD + MXU_COLS],
                           preferred_element_type=F32)
        t = prod[:, (s % per_dot) * ATT_HEAD:(s % per_dot + 1) * ATT_HEAD]
        rot = t * cos + pltpu.roll(t, ATT_HEAD // 2, 1) * sin
        if deint == 1:
            o_ref[:, sl] = rot.astype(BF16)
        else:
            acc_ref, = acc
            slot = s % acc_ref.shape[0]
            acc_ref[slot] = rot
            for c in range(deint):
                o_ref[c, :, sl] = acc_ref[slot, pl.ds(c, tm // deint, stride=deint), :].astype(BF16)


def _rope_tables(pos):
    half = ROPE_DIM // 2
    inv = ROPE_THETA ** (-jnp.arange(half, dtype=F32) / half)
    ang = pos.astype(F32)[:, None] * inv[None, :]
    cos, sin = jnp.cos(ang), jnp.sin(ang)
    T = pos.shape[0]
    gap = ATT_HEAD // 2 - half
    ones = jnp.ones((T, gap), F32)
    zeros = jnp.zeros((T, gap), F32)
    cos_t = jnp.concatenate([cos, ones, cos, ones], axis=1)
    sin_t = jnp.concatenate([-sin, zeros, sin, zeros], axis=1)
    return (jnp.stack([cos_t, jnp.ones_like(cos_t)]), jnp.stack([sin_t, jnp.zeros_like(sin_t)]))


def _rope_perm_kernel(w_ref, o_ref):
    kind = pl.program_id(2) % 3

    @pl.when(kind == 2)
    def _():
        o_ref[...] = w_ref[...].astype(BF16)

    @pl.when(kind < 2)
    def _():
        half = ROPE_DIM // 2
        mid = ATT_HEAD // 2
        lane = lax.broadcasted_iota(jnp.int32, (w_ref.shape[0], ATT_HEAD), 1)
        from_hi = (lane >= half) & (lane < 2 * half)
        from_lo = (lane >= mid) & (lane < mid + half)
        for c0 in range(0, w_ref.shape[1], ATT_HEAD):
            w = w_ref[:, c0:c0 + ATT_HEAD]
            swapped = jnp.where(from_hi, pltpu.roll(w, ATT_HEAD - (mid - half), 1),
                                jnp.where(from_lo, pltpu.roll(w, mid - half, 1), w))
            o_ref[:, c0:c0 + ATT_HEAD] = swapped.astype(BF16)


def _rope_perm(w_qkv, D):
    n, K, N = w_qkv.shape
    tr = _tile(K, 512)
    spec = pl.BlockSpec((None, tr, D), lambda l, i, j: (l, i, j))
    return pl.pallas_call(
        _rope_perm_kernel,
        grid=(n, K // tr, N // D),
        in_specs=[spec],
        out_specs=spec,
        out_shape=jax.ShapeDtypeStruct(w_qkv.shape, BF16),
        compiler_params=_params(("parallel", "parallel", "arbitrary")),
    )(w_qkv)


def _qkv(x, gpre, w, layer, tables, gi, dil):
    T, D = x.shape
    tn = _tile(D, QKV_COLS, MXU_COLS)
    nj = 3 * D // tn

    def plane(j):
        return (j * tn) // (2 * D)

    tm = _tile(T, QKV_ROWS, BF16_ROWS * dil)
    tab = pl.BlockSpec((None, tm, ATT_HEAD), lambda i, j: (plane(j), i, 0))
    if dil == 1:
        out = pl.BlockSpec((None, tm, tn), lambda i, j: (0, i, j))
        deint_scratch = []
    else:
        out = pl.BlockSpec((dil, tm // dil, tn), lambda i, j: (0, i, j))
        deint_scratch = [pltpu.VMEM((2, tm, ATT_HEAD), F32)]
    return pl.pallas_call(
        functools.partial(_qkv_kernel, tm, tn, dil),
        grid=(T // tm, nj),
        in_specs=[
            pl.BlockSpec((tm, D), lambda i, j: (i, 0)),
            pl.BlockSpec((1, D), lambda i, j: (0, 0)),
            pl.BlockSpec((None, D, tn), lambda i, j: (layer, 0, gi * nj + j)),
            tab, tab,
        ],
        out_specs=out,
        out_shape=jax.ShapeDtypeStruct((dil, T // dil, 3 * D), BF16),
        scratch_shapes=[pltpu.VMEM((tm, D), BF16)] + deint_scratch,
        compiler_params=_params(("parallel", "arbitrary")),
    )(x, gpre, w, *tables)


def _attn_kernel(bq, n_heads, bounds, cur_ref, prev_ref, next_ref, o_ref, lse_ref):
    D = n_heads * ATT_HEAD
    row0 = pl.program_id(0) * bq
    lo = jnp.int32(bounds[0])
    hi = jnp.int32(bounds[-1])
    for b in bounds[1:-1]:
        lo = jnp.where(row0 >= b, b, lo)
    for b in reversed(bounds[1:-1]):
        hi = jnp.where(row0 < b, b, hi)
    nk = bq + 2 * ATT_HALF
    qrow = row0 + lax.broadcasted_iota(jnp.int32, (bq, nk), 0)
    krow = row0 - ATT_HALF + lax.broadcasted_iota(jnp.int32, (bq, nk), 1)
    valid = (jnp.abs(krow - qrow) <= ATT_HALF) & (krow >= lo) & (krow < hi)
    scale = ATT_HEAD ** -0.5
    lane = lax.broadcasted_iota(jnp.int32, (bq, LANES), 1)
    lse_all = jnp.zeros((bq, LANES), F32)
    ones = jnp.ones((nk, ATT_HEAD), BF16)
    for h in range(n_heads):
        sl = slice(h * ATT_HEAD, (h + 1) * ATT_HEAD)
        ksl = slice(D + h * ATT_HEAD, D + (h + 1) * ATT_HEAD)
        vsl = slice(2 * D + h * ATT_HEAD, 2 * D + (h + 1) * ATT_HEAD)
        q = cur_ref[:, sl]
        kcat = jnp.concatenate([prev_ref[:, ksl], cur_ref[:, ksl], next_ref[:, ksl]], axis=0)
        vcat = jnp.concatenate([prev_ref[:, vsl], cur_ref[:, vsl], next_ref[:, vsl]], axis=0)
        s = lax.dot_general(q, kcat, _NT, preferred_element_type=F32) * (scale * LOG2E)
        s = jnp.where(valid, s, -1e30)
        m = jnp.max(s, axis=-1, keepdims=True)
        p = jnp.exp2(s - m).astype(BF16)
        ol = jnp.dot(p, jnp.concatenate([vcat, ones], axis=1), preferred_element_type=F32)
        l = ol[:, ATT_HEAD:]
        o_ref[:, sl] = (ol[:, :ATT_HEAD] / l).astype(BF16)
        lse_all = jnp.where(lane == h, (m + jnp.log2(l)) * (1.0 / LOG2E), lse_all)
    lse_ref[...] = lse_all


def _attn_group(qkv, D, seq_starts, seq_ends):
    dil, rows, _ = qkv.shape
    bounds = tuple(s // dil for s in seq_starts) + (seq_ends[-1] // dil,)
    seg = [b - a for a, b in zip(bounds[:-1], bounds[1:])]
    bq = ATTN_QUERY_ROWS
    for s in seg:
        bq = math.gcd(bq, s)
    assert bq >= ATT_HALF, "sequence too short for the dilated band blocks"
    nb = rows // bq

    per = bq // ATT_HALF
    nh = rows // ATT_HALF

    def halo(after):
        def imap(i, c):
            blk = (i + 1) * per if after else i * per - 1
            return (c, jnp.clip(blk, 0, nh - 1), 0)
        return pl.BlockSpec((None, ATT_HALF, 3 * D), imap)

    return pl.pallas_call(
        functools.partial(_attn_kernel, bq, D // ATT_HEAD, bounds),
        grid=(nb, dil),
        in_specs=[pl.BlockSpec((None, bq, 3 * D), lambda i, c: (c, i, 0)), halo(False), halo(True)],
        out_specs=[pl.BlockSpec((None, bq, D), lambda i, c: (c, i, 0)),
                   pl.BlockSpec((None, bq, LANES), lambda i, c: (c, i, 0))],
        out_shape=[jax.ShapeDtypeStruct((dil, rows, D), BF16),
                   jax.ShapeDtypeStruct((dil, rows, LANES), F32)],
        compiler_params=_params(("parallel", "arbitrary")),
    )(qkv, qkv, qkv)


def _attn_out_kernel(tm, x_ref, o0_ref, o1_ref, o2_ref, l0_ref, l1_ref, l2_ref, wo_ref, gpost_ref,
                     out_ref, act_ref, *scratch):
    D = x_ref.shape[1]

    def token_major(ref, sl, s):
        dil = ref.shape[0]
        if dil == 1:
            return ref[0, :, sl].astype(F32)
        for c in range(dil):
            s[pl.ds(c, tm // dil, stride=dil), :] = ref[c, :, sl].astype(F32)
        return s[...]

    free = list(scratch)
    l0, l1, l2 = [token_major(r, slice(0, LANES), free.pop() if r.shape[0] > 1 else None)
                  for r in (l0_ref, l1_ref, l2_ref)]
    m = jnp.maximum(jnp.maximum(l0, l1), l2)
    e0, e1, e2 = jnp.exp(l0 - m), jnp.exp(l1 - m), jnp.exp(l2 - m)
    inv = 1.0 / (e0 + e1 + e2)
    w0, w1, w2 = e0 * inv, e1 * inv, e2 * inv
    h = None
    heads_per_dot = MXU_COLS // ATT_HEAD
    for k in range(D // ATT_HEAD):
        sl = slice(k * ATT_HEAD, (k + 1) * ATT_HEAD)
        pool = list(free)
        o0, o1, o2 = [token_major(r, sl, pool.pop() if r.shape[0] > 1 else None)
                      for r in (o0_ref, o1_ref, o2_ref)]
        act_ref[:, sl] = (w0[:, k:k + 1] * o0 + w1[:, k:k + 1] * o1
                          + w2[:, k:k + 1] * o2).astype(BF16)
        if (k + 1) % heads_per_dot == 0:
            ks = slice((k + 1 - heads_per_dot) * ATT_HEAD, (k + 1) * ATT_HEAD)
            part = jnp.dot(act_ref[:, ks], wo_ref[ks, :], preferred_element_type=F32)
            h = part if h is None else h + part
    out_ref[...] = x_ref[...] + _rms(h, gpost_ref[...])


def _attn_out(x, os_, lses, w_o, gpost):
    T, D = x.shape
    tm = _tile(T, ATTN_OUT_ROWS, BF16_ROWS * max(o.shape[0] for o in os_))
    tok = pl.BlockSpec((tm, D), lambda i: (i, 0))

    def planes(a):
        dil = a.shape[0]
        return pl.BlockSpec((dil, tm // dil, a.shape[2]), lambda i: (0, i, 0))

    n_scratch = sum(2 for o in os_ if o.shape[0] > 1)
    return pl.pallas_call(
        functools.partial(_attn_out_kernel, tm),
        grid=(T // tm,),
        in_specs=[tok] + [planes(a) for a in os_] + [planes(a) for a in lses]
        + [pl.BlockSpec((D, D), lambda i: (0, 0)), pl.BlockSpec((1, D), lambda i: (0, 0))],
        out_specs=tok,
        out_shape=jax.ShapeDtypeStruct((T, D), F32),
        scratch_shapes=[pltpu.VMEM((tm, D), BF16)] + [pltpu.VMEM((tm, LANES), F32)] * n_scratch,
        compiler_params=_params(("parallel",)),
    )(x, *os_, *lses, w_o, gpost)


def kernel(x_prompt, x_sample, norm_pre, norm_post, ffn_w_gate, ffn_w_up, ffn_w_down, rwkv_mu, rwkv_w_rkv, rwkv_w0, rwkv_w1, rwkv_w2, rwkv_a0, rwkv_a1, rwkv_a2, rwkv_v0, rwkv_v1, rwkv_v2, rwkv_g1, rwkv_g2, rwkv_k_k, rwkv_k_a, rwkv_r_k, rwkv_gn_w, rwkv_gn_b, rwkv_w_o, attn_w_qkv, attn_w_o):
    D = x_prompt.shape[-1]
    depth = norm_pre.shape[0]
    seqs = [x_prompt.shape[1]] * x_prompt.shape[0] + [x_sample.shape[1]] * x_sample.shape[0]
    seq_starts, seq_ends, t = [], [], 0
    for s in seqs:
        seq_starts.append(t)
        t += s
        seq_ends.append(t)
    seq_starts, seq_ends = tuple(seq_starts), tuple(seq_ends)
    x = jnp.concatenate([x_prompt.reshape(-1, D), x_sample.reshape(-1, D)], axis=0)
    pos = jnp.concatenate([jnp.arange(s, dtype=jnp.int32) for s in seqs])
    tables = _rope_tables(pos)

    def vec(a):
        return a.reshape(1, D)

    wg, wu, wd = ffn_w_gate.astype(BF16), ffn_w_up.astype(BF16), ffn_w_down.astype(BF16)
    w_rkv = rwkv_w_rkv.astype(BF16)
    w_qkv = _rope_perm(attn_w_qkv, D)
    v_first = None
    for layer in range(depth):
        x = _ffn(x, vec(norm_pre[layer, 0]), wg, wu, wd, vec(norm_post[layer, 0]), layer, 0)
        j = layer // 2
        if layer % 2 == 0:
            vres_w = None if j == 0 else (rwkv_v0[j - 1], rwkv_v1[j - 1], rwkv_v2[j - 1])
            pre = _rwkv_pre(x, seq_starts, seq_ends, vec(norm_pre[layer, 1]), rwkv_mu[j],
                            rwkv_w0[j], rwkv_w1[j], rwkv_w2[j], rwkv_a0[j], rwkv_a1[j], rwkv_a2[j],
                            rwkv_g1[j], rwkv_g2[j], vres_w)
            xr, xk, xv, ld, a, g = pre[:6]
            r = _mm(xr, w_rkv, (j, 0))
            k = _mm(xk, w_rkv, (j, 1))
            v = _mm(xv, w_rkv, (j, 2))
            if j == 0:
                v_first, vres = v, None
            else:
                vres = (v_first, pre[6])
            y = _wkv(r, k, v, ld, a, vec(rwkv_k_k[j]), vec(rwkv_k_a[j]), vres, seq_starts, seq_ends)
            x = _rwkv_out(x, y, r, k, v, a, g, vec(rwkv_k_a[j]), vec(rwkv_r_k[j]),
                          vec(rwkv_gn_w[j]), vec(rwkv_gn_b[j]), rwkv_w_o[j].astype(BF16),
                          vec(norm_post[layer, 1]), vres)
        else:
            os_, lses = [], []
            for gi, (_, dil) in enumerate(DILATED_GROUPS):
                qkv = _qkv(x, vec(norm_pre[layer, 1]), w_qkv, j, tables, gi, dil)
                o, lse = _attn_group(qkv, D, seq_starts, seq_ends)
                os_.append(o)
                lses.append(lse)
            x = _attn_out(x, os_, lses, attn_w_o[j].astype(BF16), vec(norm_post[layer, 1]))
        x = _ffn(x, vec(norm_pre[layer, 2]), wg, wu, wd, vec(norm_post[layer, 2]), layer, 1)
    n_p = x_prompt.shape[0] * x_prompt.shape[1]
    return (x[:n_p].reshape(x_prompt.shape), x[n_p:].reshape(x_sample.shape))
```

```python
import functools
import math

import jax
import jax.numpy as jnp
from jax import lax
from jax.experimental import pallas as pl
from jax.experimental.pallas import tpu as pltpu

F32 = jnp.float32
BF16 = jnp.bfloat16

NORM_EPS = 1e-6
GN_EPS = 64e-5
RWKV_HEAD = 64
ATT_HEAD = 128
ROPE_DIM = ATT_HEAD // 4
ROPE_THETA = 500000.0
DILATED_GROUPS = ((128, 1), (512, 4), (2048, 16))
N_GROUPS = len(DILATED_GROUPS)
ATT_HALF = 64
LANES = 128
MXU_COLS = 256
LOG2E = math.log2(math.e)
WKV_CHUNK = 64
WKV_PAIRS_PER_STEP = 16
VMEM_LIMIT = 56 * 1024 * 1024
BF16_ROWS = 16

FFN_ROWS, FFN_COLS = 768, 512
MM_ROWS, MM_COLS = 1024, 1024
QKV_ROWS, QKV_COLS = 1024, 2048
RWKV_PRE_ROWS = 256
RWKV_OUT_ROWS = 256
ATTN_OUT_ROWS = 256
ATTN_QUERY_ROWS = 128

_NT = (((1,), (1,)), ((), ()))
_TN = (((0,), (0,)), ((), ()))


def _tile(n, pref, mult=8):
    if n <= pref:
        return n
    t = (pref // mult) * mult
    while t >= mult:
        if n % t == 0:
            return t
        t -= mult
    return n


def _in_list(v, lst):
    r = v == lst[0]
    for s in lst[1:]:
        r = r | (v == s)
    return r


def _rms(x, g):
    return x * lax.rsqrt(jnp.mean(x * x, axis=-1, keepdims=True) + NORM_EPS) * g


NORM_ROWS = 32


def _rms_rows(src_ref, g, emit):
    for r0 in range(0, src_ref.shape[0], NORM_ROWS):
        rows = slice(r0, min(r0 + NORM_ROWS, src_ref.shape[0]))
        emit(rows, _rms(src_ref[rows, :], g))


def _bdot(a, b):
    return jnp.dot(a.astype(BF16), b.astype(BF16), preferred_element_type=F32)


def _ntdot(a, b):
    return lax.dot_general(a, b, _NT, preferred_element_type=F32)


def _params(sem):
    return pltpu.CompilerParams(dimension_semantics=sem, vmem_limit_bytes=VMEM_LIMIT)


def _ffn_kernel(x_ref, gpre_ref, wg_ref, wu_ref, wd_ref, gpost_ref, o_ref, xn_ref, acc_ref):
    j = pl.program_id(1)

    @pl.when(j == 0)
    def _():
        def put(rows, y):
            xn_ref[rows, :] = y.astype(BF16)
        _rms_rows(x_ref, gpre_ref[...], put)
        acc_ref[...] = jnp.zeros_like(acc_ref)

    xn = xn_ref[...]
    g = jnp.dot(xn, wg_ref[...], preferred_element_type=F32)
    u = jnp.dot(xn, wu_ref[...], preferred_element_type=F32)
    h = (g * jax.nn.sigmoid(g)) * u
    acc_ref[...] += jnp.dot(h.astype(BF16), wd_ref[...], preferred_element_type=F32)

    @pl.when(j == pl.num_programs(1) - 1)
    def _():
        def put(rows, y):
            o_ref[rows, :] = x_ref[rows, :] + y
        _rms_rows(acc_ref, 0.5 * gpost_ref[...], put)


def _ffn(x, gpre, wg, wu, wd, gpost, layer, k):
    T, D = x.shape
    F = wg.shape[-1]
    tm = _tile(T, FFN_ROWS, MXU_COLS)
    tf = _tile(F, FFN_COLS, LANES)
    return pl.pallas_call(
        _ffn_kernel,
        grid=(T // tm, F // tf),
        in_specs=[
            pl.BlockSpec((tm, D), lambda i, j: (i, 0)),
            pl.BlockSpec((1, D), lambda i, j: (0, 0)),
            pl.BlockSpec((None, None, D, tf), lambda i, j: (layer, k, 0, j)),
            pl.BlockSpec((None, None, D, tf), lambda i, j: (layer, k, 0, j)),
            pl.BlockSpec((None, None, tf, D), lambda i, j: (layer, k, j, 0)),
            pl.BlockSpec((1, D), lambda i, j: (0, 0)),
        ],
        out_specs=pl.BlockSpec((tm, D), lambda i, j: (i, 0)),
        out_shape=jax.ShapeDtypeStruct((T, D), F32),
        scratch_shapes=[pltpu.VMEM((tm, D), BF16), pltpu.VMEM((tm, D), F32)],
        compiler_params=_params(("parallel", "arbitrary")),
    )(x, gpre, wg, wu, wd, gpost)


def _mm_kernel(a_ref, w_ref, o_ref):
    a = a_ref[...]
    for c in range(0, o_ref.shape[1], MXU_COLS):
        o_ref[:, c:c + MXU_COLS] = jnp.dot(
            a, w_ref[:, c:c + MXU_COLS], preferred_element_type=F32).astype(o_ref.dtype)


def _mm(a, w, lead):
    M, K = a.shape
    N = w.shape[-1]
    tm = _tile(M, MM_ROWS, BF16_ROWS)
    tn = _tile(N, MM_COLS, MXU_COLS)
    return pl.pallas_call(
        _mm_kernel,
        grid=(M // tm, N // tn),
        in_specs=[
            pl.BlockSpec((tm, K), lambda i, j: (i, 0)),
            pl.BlockSpec((None,) * len(lead) + (K, tn), lambda i, j: tuple(lead) + (0, j)),
        ],
        out_specs=pl.BlockSpec((tm, tn), lambda i, j: (i, j)),
        out_shape=jax.ShapeDtypeStruct((M, N), BF16),
        compiler_params=_params(("parallel", "arbitrary")),
    )(a, w)


def _rwkv_pre_kernel(seq_starts, seq_ends, has_vres, tm, *refs):
    (x_ref, xp_ref, xn_ref, gpre_ref, mu_ref, w0_ref, w1_ref, w2_ref,
     a0_ref, a1_ref, a2_ref, g1_ref, g2_ref) = refs[:13]
    refs = refs[13:]
    if has_vres:
        v0_ref, v1_ref, v2_ref = refs[:3]
        refs = refs[3:]
    xr_ref, xk_ref, xv_ref, ld_ref, a_ref, g_ref = refs[:6]

    t0 = pl.program_id(0) * tm
    gpre = gpre_ref[...]
    h = _rms(x_ref[...], gpre)
    keep_p = 1.0 - _in_list(t0, seq_starts).astype(F32)
    keep_n = 1.0 - _in_list(t0 + tm, seq_ends).astype(F32)
    hp = _rms(xp_ref[7:8, :], gpre) * keep_p
    hn = _rms(xn_ref[0:1, :], gpre) * keep_n
    row = lax.broadcasted_iota(jnp.int32, h.shape, 0)
    prev = jnp.where(row == 0, hp, pltpu.roll(h, 1, 0))
    nxt = jnp.where(row == tm - 1, hn, pltpu.roll(h, tm - 1, 0))
    xx = 0.5 * (prev + nxt) - h

    def mix(i):
        return (h + xx * mu_ref[i:i + 1, :]).astype(BF16)

    xr_ref[...] = mix(0)
    xw = mix(1)
    xk_ref[...] = mix(2)
    xv = mix(3)
    xv_ref[...] = xv
    xa = mix(4)
    xg = mix(5)
    for d in range(2):
        tw = jnp.tanh(jnp.dot(xw, w1_ref[d], preferred_element_type=F32))
        z = w0_ref[d:d + 1, :] + _bdot(tw, w2_ref[d])
        ld_ref[d] = (-math.exp(-0.5) * jax.nn.sigmoid(z)).astype(BF16)
        ta = jnp.dot(xa, a1_ref[d], preferred_element_type=F32)
        a_ref[d] = jax.nn.sigmoid(a0_ref[d:d + 1, :] + _bdot(ta, a2_ref[d])).astype(BF16)
    tg = jax.nn.sigmoid(jnp.dot(xg, g1_ref[...], preferred_element_type=F32))
    g_ref[...] = _bdot(tg, g2_ref[...]).astype(BF16)
    if has_vres:
        vg_ref = refs[6]
        tv = jnp.dot(xv, v1_ref[...], preferred_element_type=F32)
        vg_ref[...] = jax.nn.sigmoid(v0_ref[...] + _bdot(tv, v2_ref[...])).astype(BF16)


def _pad_lora(w_in, w_out):
    r = w_in.shape[-1]
    rp = -(-r // LANES) * LANES
    pin = [(0, 0)] * (w_in.ndim - 1) + [(0, rp - r)]
    pout = [(0, 0)] * (w_out.ndim - 2) + [(0, rp - r), (0, 0)]
    return jnp.pad(w_in, pin).astype(BF16), jnp.pad(w_out, pout).astype(BF16)


def _rwkv_pre(x, seq_starts, seq_ends, gpre, mu, w0, w1, w2, a0, a1, a2, g1, g2, vres):
    T, D = x.shape
    tm = _tile(T, RWKV_PRE_ROWS, BF16_ROWS)
    nb8 = T // 8
    has_vres = vres is not None
    w1p, w2p = _pad_lora(w1, w2)
    a1p, a2p = _pad_lora(a1, a2)
    g1p, g2p = _pad_lora(g1, g2)

    def full(a):
        nd = a.ndim
        return pl.BlockSpec(a.shape, lambda i, _nd=nd: (0,) * _nd)

    args = [x, x, x, gpre, mu, w0, w1p, w2p, a0, a1p, a2p, g1p, g2p]
    in_specs = [
        pl.BlockSpec((tm, D), lambda i: (i, 0)),
        pl.BlockSpec((8, D), lambda i: (jnp.maximum(i * (tm // 8) - 1, 0), 0)),
        pl.BlockSpec((8, D), lambda i: (jnp.minimum((i + 1) * (tm // 8), nb8 - 1), 0)),
    ] + [full(a) for a in args[3:]]
    if has_vres:
        v0, v1, v2 = vres
        v1p, v2p = _pad_lora(v1, v2)
        extra = [v0.reshape(1, D), v1p, v2p]
        args += extra
        in_specs += [full(a) for a in extra]
    tok = pl.BlockSpec((tm, D), lambda i: (i, 0))
    tok2 = pl.BlockSpec((2, tm, D), lambda i: (0, i, 0))
    out_specs = [tok, tok, tok, tok2, tok2, tok]
    out_shape = [jax.ShapeDtypeStruct((T, D), BF16)] * 3 + [
        jax.ShapeDtypeStruct((2, T, D), BF16), jax.ShapeDtypeStruct((2, T, D), BF16),
        jax.ShapeDtypeStruct((T, D), BF16)]
    if has_vres:
        out_specs.append(tok)
        out_shape.append(jax.ShapeDtypeStruct((T, D), BF16))
    return pl.pallas_call(
        functools.partial(_rwkv_pre_kernel, seq_starts, seq_ends, has_vres, tm),
        grid=(T // tm,),
        in_specs=in_specs,
        out_specs=out_specs,
        out_shape=out_shape,
        compiler_params=_params(("parallel",)),
    )(*args)


def _wkv_kernel(L, G, nc, resets_f, resets_r, has_vres, *refs):
    r_ref, k_ref, v_ref, ld_ref, a_ref, kk_ref, ka_ref = refs[:7]
    refs = refs[7:]
    if has_vres:
        vf_ref, vg_ref = refs[:2]
        refs = refs[2:]
    y_ref, s_ref = refs

    d = pl.program_id(0)
    c = pl.program_id(2)
    ci = c + d * (nc - 1 - 2 * c)
    reset = ((d == 0) & _in_list(ci, resets_f)) | ((d == 1) & _in_list(ci, resets_r))

    @pl.when(reset)
    def _():
        s_ref[...] = jnp.zeros_like(s_ref)

    sgn = 1 - 2 * d
    tt = lax.broadcasted_iota(jnp.int32, (L, LANES), 0)
    ss = lax.broadcasted_iota(jnp.int32, (L, LANES), 1) % L
    delta = sgn * (tt - ss)
    incl = delta >= 0
    strict = delta > 0
    eye = (tt == ss).astype(F32)
    tri = (sgn * (lax.broadcasted_iota(jnp.int32, (L, L), 0)
                  - lax.broadcasted_iota(jnp.int32, (L, L), 1)) >= 0).astype(BF16)
    first_head = lax.broadcasted_iota(jnp.int32, (L, LANES), 1) < RWKV_HEAD
    bi = lax.broadcasted_iota(jnp.int32, (LANES, LANES), 0) // RWKV_HEAD
    bj = lax.broadcasted_iota(jnp.int32, (LANES, LANES), 1) // RWKV_HEAD
    same_head = bi == bj
    head_ones = same_head.astype(BF16)
    n_double = int(math.log2(L)) - 1

    def bd(x):
        xb = x.astype(BF16)
        zero = jnp.zeros_like(xb)
        return jnp.concatenate([jnp.where(first_head, xb, zero), jnp.where(first_head, zero, xb)], axis=0)

    kkcs = [k_ref[:, p * LANES:(p + 1) * LANES].astype(F32) * kk_ref[:, p * LANES:(p + 1) * LANES]
            for p in range(G)]
    n2_all = jnp.dot(jnp.concatenate([(kkc * kkc).astype(BF16) for kkc in kkcs], axis=0), head_ones,
                     preferred_element_type=F32)
    pairs = []
    for p in range(G):
        sl = slice(p * LANES, (p + 1) * LANES)
        r_ = r_ref[:, sl].astype(F32)
        k_ = k_ref[:, sl].astype(F32)
        v_ = v_ref[:, sl].astype(F32)
        ld_b = ld_ref[:, sl]
        ld_ = ld_b.astype(F32)
        a_ = a_ref[:, sl].astype(F32)
        if has_vres:
            v_ = v_ + (vf_ref[:, sl].astype(F32) - v_) * vg_ref[:, sl].astype(F32)
        kkn = kkcs[p] / jnp.maximum(jnp.sqrt(n2_all[p * L:(p + 1) * L]), 1e-12)
        kd = k_ * (1.0 + (a_ - 1.0) * ka_ref[:, sl])
        cs = jnp.dot(tri, ld_b, preferred_element_type=F32)
        w_inv = jnp.exp(-cs)
        bt = kkn * a_ * w_inv
        kt = kd * w_inv
        pairs.append(dict(
            sl=sl, v=v_, bt=bt, kt=kt, st=s_ref[p],
            lhs=jnp.concatenate([-kkn * jnp.exp(cs - ld_), r_ * jnp.exp(cs)], axis=0).astype(BF16),
            wend=jnp.exp(jnp.sum(ld_, axis=0, keepdims=True))))

    for pr in pairs:
        rhs_t = jnp.concatenate([bd(pr["bt"]), bd(pr["kt"]), pr["st"].astype(BF16)], axis=0)
        prod = _ntdot(pr["lhs"], rhs_t)
        pr["a_ab"] = jnp.where(strict, prod[:L, :LANES], 0.0)
        pr["a_rb"] = jnp.where(incl, prod[L:, :LANES], 0.0)
        a_ak = jnp.where(strict, prod[:L, LANES:2 * LANES], 0.0)
        a_rk = jnp.where(incl, prod[L:, LANES:2 * LANES], 0.0)
        pr["sp"] = prod[:, 2 * LANES:]
        pr["a_k"] = jnp.concatenate([a_ak, a_rk], axis=0)
    for pr in pairs:
        pr["spv"] = pr["sp"] + _bdot(pr["a_k"], bd(pr["v"]))
    for pr in pairs:
        x = _bdot(jnp.concatenate([pr["a_ab"], pr["a_rb"]], axis=0), bd(pr["a_ab"]))
        pr["tinv"] = eye + pr["a_ab"]
        pr["pw"] = x[:L]
        pr["rt"] = pr["a_rb"] + x[L:]
    for lvl in range(1, n_double + 1):
        last = lvl == n_double
        for pr in pairs:
            rows = [pr["tinv"], pr["rt"]] if last else [pr["tinv"], pr["rt"], pr["pw"]]
            x = _bdot(jnp.concatenate(rows, axis=0), bd(pr["pw"]))
            pr["tinv"] = pr["tinv"] + x[:L]
            pr["rt"] = pr["rt"] + x[L:2 * L]
            if not last:
                pr["pw"] = x[2 * L:]
    for pr in pairs:
        z = _bdot(jnp.concatenate([pr["tinv"], pr["rt"]], axis=0), bd(pr["spv"][:L]))
        pr["u"] = z[:L]
        pr["y"] = pr["spv"][L:] + z[L:]
    for p, pr in enumerate(pairs):
        y_ref[:, pr["sl"]] = pr["y"].astype(BF16)
        uv = jnp.concatenate([pr["u"], pr["v"]], axis=0).astype(BF16)
        bk = jnp.concatenate([pr["bt"], pr["kt"]], axis=0).astype(BF16)
        upd = lax.dot_general(uv, bk, _TN, preferred_element_type=F32)
        s_ref[p] = (pr["st"] + jnp.where(same_head, upd, 0.0)) * pr["wend"]


def _wkv(r, k, v, ld, a, k_k, k_a, vres, seq_starts, seq_ends):
    T, D = r.shape
    L = WKV_CHUNK
    npairs = D // LANES
    assert 2 * L == LANES and RWKV_HEAD == L
    G = math.gcd(npairs, WKV_PAIRS_PER_STEP)
    nc = T // L
    W = G * LANES
    resets_f = tuple(s // L for s in seq_starts)
    resets_r = tuple(e // L - 1 for e in seq_ends)
    has_vres = vres is not None

    def cidx(d, c):
        return c + d * (nc - 1 - 2 * c)

    tok = pl.BlockSpec((L, W), lambda d, p, c: (cidx(d, c), p))
    tokd = pl.BlockSpec((None, L, W), lambda d, p, c: (d, cidx(d, c), p))
    par = pl.BlockSpec((1, W), lambda d, p, c: (0, p))
    args = [r, k, v, ld, a, k_k, k_a]
    in_specs = [tok, tok, tok, tokd, tokd, par, par]
    if has_vres:
        args += list(vres)
        in_specs += [tok, tok]
    return pl.pallas_call(
        functools.partial(_wkv_kernel, L, G, nc, resets_f, resets_r, has_vres),
        grid=(2, npairs // G, nc),
        in_specs=in_specs,
        out_specs=tokd,
        out_shape=jax.ShapeDtypeStruct((2, T, D), BF16),
        scratch_shapes=[pltpu.VMEM((G, LANES, LANES), F32)],
        compiler_params=_params(("arbitrary", "arbitrary", "arbitrary")),
    )(*args)


def _rwkv_out_kernel(has_vres, *refs):
    (x_ref, y_ref, r_ref, k_ref, v_ref, a_ref, g_ref, ka_ref, rk_ref, gnw_ref, gnb_ref,
     wo_ref, gpost_ref) = refs[:13]
    refs = refs[13:]
    if has_vres:
        vf_ref, vg_ref = refs[:2]
        refs = refs[2:]
    o_ref, act_ref = refs
    D = x_ref.shape[1]

    bi = lax.broadcasted_iota(jnp.int32, (LANES, LANES), 0) // RWKV_HEAD
    bj = lax.broadcasted_iota(jnp.int32, (LANES, LANES), 1) // RWKV_HEAD
    head_ones = (bi == bj).astype(BF16)
    inv_n = 1.0 / RWKV_HEAD

    slabs = [slice(s * LANES, (s + 1) * LANES) for s in range(D // LANES)]
    ys = [y_ref[0, :, sl].astype(F32) + y_ref[1, :, sl].astype(F32) for sl in slabs]
    means = [_bdot(y, head_ones) * inv_n for y in ys]
    dys = [y - mu for y, mu in zip(ys, means)]
    vars_ = [_bdot(dy * dy, head_ones) * inv_n for dy in dys]
    rks = []
    for sl in slabs:
        a_mean = 0.5 * (a_ref[0, :, sl].astype(F32) + a_ref[1, :, sl].astype(F32))
        k_mean = k_ref[:, sl].astype(F32) * (1.0 + (a_mean - 1.0) * ka_ref[:, sl])
        rks.append(_bdot(r_ref[:, sl].astype(F32) * k_mean * rk_ref[:, sl], head_ones))
    h = None
    slabs_per_dot = MXU_COLS // LANES
    for s, (sl, dy, var, rk) in enumerate(zip(slabs, dys, vars_, rks)):
        yn = dy * lax.rsqrt(var + GN_EPS) * gnw_ref[:, sl] + gnb_ref[:, sl]
        v_ = v_ref[:, sl].astype(F32)
        if has_vres:
            v_ = v_ + (vf_ref[:, sl].astype(F32) - v_) * vg_ref[:, sl].astype(F32)
        act_ref[:, sl] = ((yn + rk * v_) * g_ref[:, sl].astype(F32)).astype(BF16)
        if (s + 1) % slabs_per_dot == 0:
            ks = slice((s + 1 - slabs_per_dot) * LANES, (s + 1) * LANES)
            part = jnp.dot(act_ref[:, ks], wo_ref[ks, :], preferred_element_type=F32)
            h = part if h is None else h + part
    o_ref[...] = x_ref[...] + _rms(h, gpost_ref[...])


def _rwkv_out(x, y, r, k, v, a, g, k_a, r_k, gn_w, gn_b, w_o, gpost, vres):
    T, D = x.shape
    tm = _tile(T, RWKV_OUT_ROWS, BF16_ROWS)
    has_vres = vres is not None
    tok = pl.BlockSpec((tm, D), lambda i: (i, 0))
    tok2 = pl.BlockSpec((2, tm, D), lambda i: (0, i, 0))
    par = pl.BlockSpec((1, D), lambda i: (0, 0))
    args = [x, y, r, k, v, a, g, k_a, r_k, gn_w, gn_b, w_o, gpost]
    in_specs = [tok, tok2, tok, tok, tok, tok2, tok, par, par, par, par,
                pl.BlockSpec((D, D), lambda i: (0, 0)), par]
    if has_vres:
        args += list(vres)
        in_specs += [tok, tok]
    return pl.pallas_call(
        functools.partial(_rwkv_out_kernel, has_vres),
        grid=(T // tm,),
        in_specs=in_specs,
        out_specs=tok,
        out_shape=jax.ShapeDtypeStruct((T, D), F32),
        scratch_shapes=[pltpu.VMEM((tm, D), BF16)],
        compiler_params=_params(("parallel",)),
    )(*args)


def _qkv_kernel(tm, tn, deint, x_ref, gpre_ref, w_ref, cos_ref, sin_ref, o_ref, xn_ref, *acc):
    j = pl.program_id(1)

    @pl.when(j == 0)
    def _():
        def put(rows, y):
            xn_ref[rows, :] = y.astype(BF16)
        _rms_rows(x_ref, gpre_ref[...], put)

    cos = cos_ref[...]
    sin = sin_ref[...]
    xn = xn_ref[...]
    per_dot = MXU_COLS // ATT_HEAD
    for s in range(tn // ATT_HEAD):
        sl = slice(s * ATT_HEAD, (s + 1) * ATT_HEAD)
        if s % per_dot == 0:
            prod = jnp.dot(xn, w_ref[:, s * ATT_HEAD:s * ATT_HEAD + MXU_COLS],
                           preferred_element_type=F32)
        t = prod[:, (s % per_dot) * ATT_HEAD:(s % per_dot + 1) * ATT_HEAD]
        rot = t * cos + pltpu.roll(t, ATT_HEAD // 2, 1) * sin
        if deint == 1:
            o_ref[:, sl] = rot.astype(BF16)
        else:
            acc_ref, = acc
            slot = s % acc_ref.shape[0]
            acc_ref[slot] = rot
            for c in range(deint):
                o_ref[c, :, sl] = acc_ref[slot, pl.ds(c, tm // deint, stride=deint), :].astype(BF16)


def _rope_tables(pos):
    half = ROPE_DIM // 2
    inv = ROPE_THETA ** (-jnp.arange(half, dtype=F32) / half)
    ang = pos.astype(F32)[:, None] * inv[None, :]
    cos, sin = jnp.cos(ang), jnp.sin(ang)
    T = pos.shape[0]
    gap = ATT_HEAD // 2 - half
    ones = jnp.ones((T, gap), F32)
    zeros = jnp.zeros((T, gap), F32)
    cos_t = jnp.concatenate([cos, ones, cos, ones], axis=1)
    sin_t = jnp.concatenate([-sin, zeros, sin, zeros], axis=1)
    return (jnp.stack([cos_t, jnp.ones_like(cos_t)]), jnp.stack([sin_t, jnp.zeros_like(sin_t)]))


def _rope_perm_kernel(w_ref, o_ref):
    kind = pl.program_id(2) % 3

    @pl.when(kind == 2)
    def _():
        o_ref[...] = w_ref[...].astype(BF16)

    @pl.when(kind < 2)
    def _():
        half = ROPE_DIM // 2
        mid = ATT_HEAD // 2
        lane = lax.broadcasted_iota(jnp.int32, (w_ref.shape[0], ATT_HEAD), 1)
        from_hi = (lane >= half) & (lane < 2 * half)
        from_lo = (lane >= mid) & (lane < mid + half)
        for c0 in range(0, w_ref.shape[1], ATT_HEAD):
            w = w_ref[:, c0:c0 + ATT_HEAD]
            swapped = jnp.where(from_hi, pltpu.roll(w, ATT_HEAD - (mid - half), 1),
                                jnp.where(from_lo, pltpu.roll(w, mid - half, 1), w))
            o_ref[:, c0:c0 + ATT_HEAD] = swapped.astype(BF16)


def _rope_perm(w_qkv, D):
    n, K, N = w_qkv.shape
    tr = _tile(K, 512)
    spec = pl.BlockSpec((None, tr, D), lambda l, i, j: (l, i, j))
    return pl.pallas_call(
        _rope_perm_kernel,
        grid=(n, K // tr, N // D),
        in_specs=[spec],
        out_specs=spec,
        out_shape=jax.ShapeDtypeStruct(w_qkv.shape, BF16),
        compiler_params=_params(("parallel", "parallel", "arbitrary")),
    )(w_qkv)


def _qkv(x, gpre, w, layer, tables, gi, dil):
    T, D = x.shape
    tn = _tile(D, QKV_COLS, MXU_COLS)
    nj = 3 * D // tn

    def plane(j):
        return (j * tn) // (2 * D)

    tm = _tile(T, QKV_ROWS, BF16_ROWS * dil)
    tab = pl.BlockSpec((None, tm, ATT_HEAD), lambda i, j: (plane(j), i, 0))
    if dil == 1:
        out = pl.BlockSpec((None, tm, tn), lambda i, j: (0, i, j))
        deint_scratch = []
    else:
        out = pl.BlockSpec((dil, tm // dil, tn), lambda i, j: (0, i, j))
        deint_scratch = [pltpu.VMEM((2, tm, ATT_HEAD), F32)]
    return pl.pallas_call(
        functools.partial(_qkv_kernel, tm, tn, dil),
        grid=(T // tm, nj),
        in_specs=[
            pl.BlockSpec((tm, D), lambda i, j: (i, 0)),
            pl.BlockSpec((1, D), lambda i, j: (0, 0)),
            pl.BlockSpec((None, D, tn), lambda i, j: (layer, 0, gi * nj + j)),
            tab, tab,
        ],
        out_specs=out,
        out_shape=jax.ShapeDtypeStruct((dil, T // dil, 3 * D), BF16),
        scratch_shapes=[pltpu.VMEM((tm, D), BF16)] + deint_scratch,
        compiler_params=_params(("parallel", "arbitrary")),
    )(x, gpre, w, *tables)


def _attn_kernel(bq, n_heads, bounds, cur_ref, kp_ref, kn_ref, vp_ref, vn_ref, o_ref, lse_ref):
    D = n_heads * ATT_HEAD
    row0 = pl.program_id(0) * bq
    lo = jnp.int32(bounds[0])
    hi = jnp.int32(bounds[-1])
    for b in bounds[1:-1]:
        lo = jnp.where(row0 >= b, b, lo)
    for b in reversed(bounds[1:-1]):
        hi = jnp.where(row0 < b, b, hi)
    nk = bq + 2 * ATT_HALF
    qrow = row0 + lax.broadcasted_iota(jnp.int32, (bq, nk), 0)
    krow = row0 - ATT_HALF + lax.broadcasted_iota(jnp.int32, (bq, nk), 1)
    valid = (jnp.abs(krow - qrow) <= ATT_HALF) & (krow >= lo) & (krow < hi)
    scale = ATT_HEAD ** -0.5
    lane = lax.broadcasted_iota(jnp.int32, (bq, LANES), 1)
    lse_all = jnp.zeros((bq, LANES), F32)
    ones = jnp.ones((nk, ATT_HEAD), BF16)
    for h in range(n_heads):
        sl = slice(h * ATT_HEAD, (h + 1) * ATT_HEAD)
        ksl = slice(D + h * ATT_HEAD, D + (h + 1) * ATT_HEAD)
        vsl = slice(2 * D + h * ATT_HEAD, 2 * D + (h + 1) * ATT_HEAD)
        q = cur_ref[:, sl]
        kcat = jnp.concatenate([kp_ref[:, sl], cur_ref[:, ksl], kn_ref[:, sl]], axis=0)
        vcat = jnp.concatenate([vp_ref[:, sl], cur_ref[:, vsl], vn_ref[:, sl]], axis=0)
        s = lax.dot_general(q, kcat, _NT, preferred_element_type=F32) * (scale * LOG2E)
        s = jnp.where(valid, s, -1e30)
        m = jnp.max(s, axis=-1, keepdims=True)
        p = jnp.exp2(s - m).astype(BF16)
        ol = jnp.dot(p, jnp.concatenate([vcat, ones], axis=1), preferred_element_type=F32)
        l = ol[:, ATT_HEAD:]
        o_ref[:, sl] = (ol[:, :ATT_HEAD] / l).astype(BF16)
        lse_all = jnp.where(lane == h, (m + jnp.log2(l)) * (1.0 / LOG2E), lse_all)
    lse_ref[...] = lse_all


def _attn_group(qkv, D, seq_starts, seq_ends):
    dil, rows, _ = qkv.shape
    bounds = tuple(s // dil for s in seq_starts) + (seq_ends[-1] // dil,)
    seg = [b - a for a, b in zip(bounds[:-1], bounds[1:])]
    bq = ATTN_QUERY_ROWS
    for s in seg:
        bq = math.gcd(bq, s)
    assert bq >= ATT_HALF, "sequence too short for the dilated band blocks"
    nb = rows // bq

    per = bq // ATT_HALF
    nh = rows // ATT_HALF

    def halo(kind, after):
        def imap(i, c):
            blk = (i + 1) * per if after else i * per - 1
            return (c, jnp.clip(blk, 0, nh - 1), kind)
        return pl.BlockSpec((None, ATT_HALF, D), imap)

    return pl.pallas_call(
        functools.partial(_attn_kernel, bq, D // ATT_HEAD, bounds),
        grid=(nb, dil),
        in_specs=[pl.BlockSpec((None, bq, 3 * D), lambda i, c: (c, i, 0)),
                  halo(1, False), halo(1, True), halo(2, False), halo(2, True)],
        out_specs=[pl.BlockSpec((None, bq, D), lambda i, c: (c, i, 0)),
                   pl.BlockSpec((None, bq, LANES), lambda i, c: (c, i, 0))],
        out_shape=[jax.ShapeDtypeStruct((dil, rows, D), BF16),
                   jax.ShapeDtypeStruct((dil, rows, LANES), F32)],
        compiler_params=_params(("parallel", "arbitrary")),
    )(qkv, qkv, qkv, qkv, qkv)


def _attn_out_kernel(tm, x_ref, o0_ref, o1_ref, o2_ref, l0_ref, l1_ref, l2_ref, wo_ref, gpost_ref,
                     out_ref, act_ref, *scratch):
    D = x_ref.shape[1]

    def token_major(ref, sl, s):
        dil = ref.shape[0]
        if dil == 1:
            return ref[0, :, sl].astype(F32)
        for c in range(dil):
            s[pl.ds(c, tm // dil, stride=dil), :] = ref[c, :, sl].astype(F32)
        return s[...]

    free = list(scratch)
    l0, l1, l2 = [token_major(r, slice(0, LANES), free.pop() if r.shape[0] > 1 else None)
                  for r in (l0_ref, l1_ref, l2_ref)]
    m = jnp.maximum(jnp.maximum(l0, l1), l2)
    e0, e1, e2 = jnp.exp(l0 - m), jnp.exp(l1 - m), jnp.exp(l2 - m)
    inv = 1.0 / (e0 + e1 + e2)
    w0, w1, w2 = e0 * inv, e1 * inv, e2 * inv
    h = None
    heads_per_dot = MXU_COLS // ATT_HEAD
    for k in range(D // ATT_HEAD):
        sl = slice(k * ATT_HEAD, (k + 1) * ATT_HEAD)
        pool = list(free)
        o0, o1, o2 = [token_major(r, sl, pool.pop() if r.shape[0] > 1 else None)
                      for r in (o0_ref, o1_ref, o2_ref)]
        act_ref[:, sl] = (w0[:, k:k + 1] * o0 + w1[:, k:k + 1] * o1
                          + w2[:, k:k + 1] * o2).astype(BF16)
        if (k + 1) % heads_per_dot == 0:
            ks = slice((k + 1 - heads_per_dot) * ATT_HEAD, (k + 1) * ATT_HEAD)
            part = jnp.dot(act_ref[:, ks], wo_ref[ks, :], preferred_element_type=F32)
            h = part if h is None else h + part
    out_ref[...] = x_ref[...] + _rms(h, gpost_ref[...])


def _attn_out(x, os_, lses, w_o, gpost):
    T, D = x.shape
    tm = _tile(T, ATTN_OUT_ROWS, BF16_ROWS * max(o.shape[0] for o in os_))
    tok = pl.BlockSpec((tm, D), lambda i: (i, 0))

    def planes(a):
        dil = a.shape[0]
        return pl.BlockSpec((dil, tm // dil, a.shape[2]), lambda i: (0, i, 0))

    n_scratch = sum(2 for o in os_ if o.shape[0] > 1)
    return pl.pallas_call(
        functools.partial(_attn_out_kernel, tm),
        grid=(T // tm,),
        in_specs=[tok] + [planes(a) for a in os_] + [planes(a) for a in lses]
        + [pl.BlockSpec((D, D), lambda i: (0, 0)), pl.BlockSpec((1, D), lambda i: (0, 0))],
        out_specs=tok,
        out_shape=jax.ShapeDtypeStruct((T, D), F32),
        scratch_shapes=[pltpu.VMEM((tm, D), BF16)] + [pltpu.VMEM((tm, LANES), F32)] * n_scratch,
        compiler_params=_params(("parallel",)),
    )(x, *os_, *lses, w_o, gpost)


def kernel(x_prompt, x_sample, norm_pre, norm_post, ffn_w_gate, ffn_w_up, ffn_w_down, rwkv_mu, rwkv_w_rkv, rwkv_w0, rwkv_w1, rwkv_w2, rwkv_a0, rwkv_a1, rwkv_a2, rwkv_v0, rwkv_v1, rwkv_v2, rwkv_g1, rwkv_g2, rwkv_k_k, rwkv_k_a, rwkv_r_k, rwkv_gn_w, rwkv_gn_b, rwkv_w_o, attn_w_qkv, attn_w_o):
    D = x_prompt.shape[-1]
    depth = norm_pre.shape[0]
    seqs = [x_prompt.shape[1]] * x_prompt.shape[0] + [x_sample.shape[1]] * x_sample.shape[0]
    seq_starts, seq_ends, t = [], [], 0
    for s in seqs:
        seq_starts.append(t)
        t += s
        seq_ends.append(t)
    seq_starts, seq_ends = tuple(seq_starts), tuple(seq_ends)
    x = jnp.concatenate([x_prompt.reshape(-1, D), x_sample.reshape(-1, D)], axis=0)
    pos = jnp.concatenate([jnp.arange(s, dtype=jnp.int32) for s in seqs])
    tables = _rope_tables(pos)

    def vec(a):
        return a.reshape(1, D)

    wg, wu, wd = ffn_w_gate.astype(BF16), ffn_w_up.astype(BF16), ffn_w_down.astype(BF16)
    w_rkv = rwkv_w_rkv.astype(BF16)
    w_qkv = _rope_perm(attn_w_qkv, D)
    v_first = None
    for layer in range(depth):
        x = _ffn(x, vec(norm_pre[layer, 0]), wg, wu, wd, vec(norm_post[layer, 0]), layer, 0)
        j = layer // 2
        if layer % 2 == 0:
            vres_w = None if j == 0 else (rwkv_v0[j - 1], rwkv_v1[j - 1], rwkv_v2[j - 1])
            pre = _rwkv_pre(x, seq_starts, seq_ends, vec(norm_pre[layer, 1]), rwkv_mu[j],
                            rwkv_w0[j], rwkv_w1[j], rwkv_w2[j], rwkv_a0[j], rwkv_a1[j], rwkv_a2[j],
                            rwkv_g1[j], rwkv_g2[j], vres_w)
            xr, xk, xv, ld, a, g = pre[:6]
            r = _mm(xr, w_rkv, (j, 0))
            k = _mm(xk, w_rkv, (j, 1))
            v = _mm(xv, w_rkv, (j, 2))
            if j == 0:
                v_first, vres = v, None
            else:
                vres = (v_first, pre[6])
            y = _wkv(r, k, v, ld, a, vec(rwkv_k_k[j]), vec(rwkv_k_a[j]), vres, seq_starts, seq_ends)
            x = _rwkv_out(x, y, r, k, v, a, g, vec(rwkv_k_a[j]), vec(rwkv_r_k[j]),
                          vec(rwkv_gn_w[j]), vec(rwkv_gn_b[j]), rwkv_w_o[j].astype(BF16),
                          vec(norm_post[layer, 1]), vres)
        else:
            os_, lses = [], []
            for gi, (_, dil) in enumerate(DILATED_GROUPS):
                qkv = _qkv(x, vec(norm_pre[layer, 1]), w_qkv, j, tables, gi, dil)
                o, lse = _attn_group(qkv, D, seq_starts, seq_ends)
                os_.append(o)
                lses.append(lse)
            x = _attn_out(x, os_, lses, attn_w_o[j].astype(BF16), vec(norm_post[layer, 1]))
        x = _ffn(x, vec(norm_pre[layer, 2]), wg, wu, wd, vec(norm_post[layer, 2]), layer, 1)
    n_p = x_prompt.shape[0] * x_prompt.shape[1]
    return (x[:n_p].reshape(x_prompt.shape), x[n_p:].reshape(x_sample.shape))
```

```python
import functools
import math

import jax
import jax.numpy as jnp
from jax import lax
from jax.experimental import pallas as pl
from jax.experimental.pallas import tpu as pltpu

F32 = jnp.float32
BF16 = jnp.bfloat16

NORM_EPS = 1e-6
GN_EPS = 64e-5
RWKV_HEAD = 64
ATT_HEAD = 128
ROPE_DIM = ATT_HEAD // 4
ROPE_THETA = 500000.0
DILATED_GROUPS = ((128, 1), (512, 4), (2048, 16))
N_GROUPS = len(DILATED_GROUPS)
ATT_HALF = 64
LANES = 128
MXU_COLS = 256
LOG2E = math.log2(math.e)
WKV_CHUNK = 64
WKV_PAIRS_PER_STEP = 16
VMEM_LIMIT = 56 * 1024 * 1024
BF16_ROWS = 16

FFN_ROWS, FFN_COLS = 768, 512
MM_ROWS, MM_COLS = 1024, 1024
QKV_ROWS, QKV_COLS = 1024, 2048
RWKV_PRE_ROWS = 256
RWKV_OUT_ROWS = 256
ATTN_OUT_ROWS = 256
ATTN_QUERY_ROWS = 128

_NT = (((1,), (1,)), ((), ()))
_TN = (((0,), (0,)), ((), ()))


def _tile(n, pref, mult=8):
    if n <= pref:
        return n
    t = (pref // mult) * mult
    while t >= mult:
        if n % t == 0:
            return t
        t -= mult
    return n


def _in_list(v, lst):
    r = v == lst[0]
    for s in lst[1:]:
        r = r | (v == s)
    return r


def _rms(x, g):
    return x * lax.rsqrt(jnp.mean(x * x, axis=-1, keepdims=True) + NORM_EPS) * g


NORM_ROWS = 32


def _rms_rows(src_ref, g, emit):
    for r0 in range(0, src_ref.shape[0], NORM_ROWS):
        rows = slice(r0, min(r0 + NORM_ROWS, src_ref.shape[0]))
        emit(rows, _rms(src_ref[rows, :], g))


def _bdot(a, b):
    return jnp.dot(a.astype(BF16), b.astype(BF16), preferred_element_type=F32)


def _ntdot(a, b):
    return lax.dot_general(a, b, _NT, preferred_element_type=F32)


def _params(sem):
    return pltpu.CompilerParams(dimension_semantics=sem, vmem_limit_bytes=VMEM_LIMIT)


def _ffn_kernel(x_ref, gpre_ref, wg_ref, wu_ref, wd_ref, gpost_ref, o_ref, xn_ref, acc_ref):
    j = pl.program_id(1)

    @pl.when(j == 0)
    def _():
        def put(rows, y):
            xn_ref[rows, :] = y.astype(BF16)
        _rms_rows(x_ref, gpre_ref[...], put)
        acc_ref[...] = jnp.zeros_like(acc_ref)

    xn = xn_ref[...]
    g = jnp.dot(xn, wg_ref[...], preferred_element_type=F32)
    u = jnp.dot(xn, wu_ref[...], preferred_element_type=F32)
    h = (g * jax.nn.sigmoid(g)) * u
    acc_ref[...] += jnp.dot(h.astype(BF16), wd_ref[...], preferred_element_type=F32)

    @pl.when(j == pl.num_programs(1) - 1)
    def _():
        def put(rows, y):
            o_ref[rows, :] = x_ref[rows, :] + y
        _rms_rows(acc_ref, 0.5 * gpost_ref[...], put)


def _ffn(x, gpre, wg, wu, wd, gpost, layer, k):
    T, D = x.shape
    F = wg.shape[-1]
    tm = _tile(T, FFN_ROWS, MXU_COLS)
    tf = _tile(F, FFN_COLS, LANES)
    return pl.pallas_call(
        _ffn_kernel,
        grid=(T // tm, F // tf),
        in_specs=[
            pl.BlockSpec((tm, D), lambda i, j: (i, 0)),
            pl.BlockSpec((1, D), lambda i, j: (0, 0)),
            pl.BlockSpec((None, None, D, tf), lambda i, j: (layer, k, 0, j)),
            pl.BlockSpec((None, None, D, tf), lambda i, j: (layer, k, 0, j)),
            pl.BlockSpec((None, None, tf, D), lambda i, j: (layer, k, j, 0)),
            pl.BlockSpec((1, D), lambda i, j: (0, 0)),
        ],
        out_specs=pl.BlockSpec((tm, D), lambda i, j: (i, 0)),
        out_shape=jax.ShapeDtypeStruct((T, D), F32),
        scratch_shapes=[pltpu.VMEM((tm, D), BF16), pltpu.VMEM((tm, D), F32)],
        compiler_params=_params(("parallel", "arbitrary")),
    )(x, gpre, wg, wu, wd, gpost)


def _mm_kernel(a_ref, w_ref, o_ref):
    a = a_ref[...]
    for c in range(0, o_ref.shape[1], MXU_COLS):
        o_ref[:, c:c + MXU_COLS] = jnp.dot(
            a, w_ref[:, c:c + MXU_COLS], preferred_element_type=F32).astype(o_ref.dtype)


def _mm(a, w, lead):
    M, K = a.shape
    N = w.shape[-1]
    tm = _tile(M, MM_ROWS, BF16_ROWS)
    tn = _tile(N, MM_COLS, MXU_COLS)
    return pl.pallas_call(
        _mm_kernel,
        grid=(M // tm, N // tn),
        in_specs=[
            pl.BlockSpec((tm, K), lambda i, j: (i, 0)),
            pl.BlockSpec((None,) * len(lead) + (K, tn), lambda i, j: tuple(lead) + (0, j)),
        ],
        out_specs=pl.BlockSpec((tm, tn), lambda i, j: (i, j)),
        out_shape=jax.ShapeDtypeStruct((M, N), BF16),
        compiler_params=_params(("parallel", "arbitrary")),
    )(a, w)


def _rwkv_pre_kernel(seq_starts, seq_ends, has_vres, tm, *refs):
    (x_ref, xp_ref, xn_ref, gpre_ref, mu_ref, w0_ref, w1_ref, w2_ref,
     a0_ref, a1_ref, a2_ref, g1_ref, g2_ref) = refs[:13]
    refs = refs[13:]
    if has_vres:
        v0_ref, v1_ref, v2_ref = refs[:3]
        refs = refs[3:]
    xr_ref, xk_ref, xv_ref, ld_ref, a_ref, g_ref = refs[:6]

    t0 = pl.program_id(0) * tm
    gpre = gpre_ref[...]
    h = _rms(x_ref[...], gpre)
    keep_p = 1.0 - _in_list(t0, seq_starts).astype(F32)
    keep_n = 1.0 - _in_list(t0 + tm, seq_ends).astype(F32)
    hp = _rms(xp_ref[7:8, :], gpre) * keep_p
    hn = _rms(xn_ref[0:1, :], gpre) * keep_n
    row = lax.broadcasted_iota(jnp.int32, h.shape, 0)
    prev = jnp.where(row == 0, hp, pltpu.roll(h, 1, 0))
    nxt = jnp.where(row == tm - 1, hn, pltpu.roll(h, tm - 1, 0))
    xx = 0.5 * (prev + nxt) - h

    def mix(i):
        return (h + xx * mu_ref[i:i + 1, :]).astype(BF16)

    xr_ref[...] = mix(0)
    xw = mix(1)
    xk_ref[...] = mix(2)
    xv = mix(3)
    xv_ref[...] = xv
    xa = mix(4)
    xg = mix(5)
    for d in range(2):
        tw = jnp.tanh(jnp.dot(xw, w1_ref[d], preferred_element_type=F32))
        z = w0_ref[d:d + 1, :] + _bdot(tw, w2_ref[d])
        ld_ref[d] = (-math.exp(-0.5) * jax.nn.sigmoid(z)).astype(BF16)
        ta = jnp.dot(xa, a1_ref[d], preferred_element_type=F32)
        a_ref[d] = jax.nn.sigmoid(a0_ref[d:d + 1, :] + _bdot(ta, a2_ref[d])).astype(BF16)
    tg = jax.nn.sigmoid(jnp.dot(xg, g1_ref[...], preferred_element_type=F32))
    g_ref[...] = _bdot(tg, g2_ref[...]).astype(BF16)
    if has_vres:
        vg_ref = refs[6]
        tv = jnp.dot(xv, v1_ref[...], preferred_element_type=F32)
        vg_ref[...] = jax.nn.sigmoid(v0_ref[...] + _bdot(tv, v2_ref[...])).astype(BF16)


def _pad_lora(w_in, w_out):
    r = w_in.shape[-1]
    rp = -(-r // LANES) * LANES
    pin = [(0, 0)] * (w_in.ndim - 1) + [(0, rp - r)]
    pout = [(0, 0)] * (w_out.ndim - 2) + [(0, rp - r), (0, 0)]
    return jnp.pad(w_in, pin).astype(BF16), jnp.pad(w_out, pout).astype(BF16)


def _rwkv_pre(x, seq_starts, seq_ends, gpre, mu, w0, w1, w2, a0, a1, a2, g1, g2, vres):
    T, D = x.shape
    tm = _tile(T, RWKV_PRE_ROWS, BF16_ROWS)
    nb8 = T // 8
    has_vres = vres is not None
    w1p, w2p = _pad_lora(w1, w2)
    a1p, a2p = _pad_lora(a1, a2)
    g1p, g2p = _pad_lora(g1, g2)

    def full(a):
        nd = a.ndim
        return pl.BlockSpec(a.shape, lambda i, _nd=nd: (0,) * _nd)

    args = [x, x, x, gpre, mu, w0, w1p, w2p, a0, a1p, a2p, g1p, g2p]
    in_specs = [
        pl.BlockSpec((tm, D), lambda i: (i, 0)),
        pl.BlockSpec((8, D), lambda i: (jnp.maximum(i * (tm // 8) - 1, 0), 0)),
        pl.BlockSpec((8, D), lambda i: (jnp.minimum((i + 1) * (tm // 8), nb8 - 1), 0)),
    ] + [full(a) for a in args[3:]]
    if has_vres:
        v0, v1, v2 = vres
        v1p, v2p = _pad_lora(v1, v2)
        extra = [v0.reshape(1, D), v1p, v2p]
        args += extra
        in_specs += [full(a) for a in extra]
    tok = pl.BlockSpec((tm, D), lambda i: (i, 0))
    tok2 = pl.BlockSpec((2, tm, D), lambda i: (0, i, 0))
    out_specs = [tok, tok, tok, tok2, tok2, tok]
    out_shape = [jax.ShapeDtypeStruct((T, D), BF16)] * 3 + [
        jax.ShapeDtypeStruct((2, T, D), BF16), jax.ShapeDtypeStruct((2, T, D), BF16),
        jax.ShapeDtypeStruct((T, D), BF16)]
    if has_vres:
        out_specs.append(tok)
        out_shape.append(jax.ShapeDtypeStruct((T, D), BF16))
    return pl.pallas_call(
        functools.partial(_rwkv_pre_kernel, seq_starts, seq_ends, has_vres, tm),
        grid=(T // tm,),
        in_specs=in_specs,
        out_specs=out_specs,
        out_shape=out_shape,
        compiler_params=_params(("parallel",)),
    )(*args)


def _wkv_kernel(L, G, nc, resets_f, resets_r, has_vres, *refs):
    r_ref, k_ref, v_ref, ld_ref, a_ref, kk_ref, ka_ref = refs[:7]
    refs = refs[7:]
    if has_vres:
        vf_ref, vg_ref = refs[:2]
        refs = refs[2:]
    y_ref, s_ref = refs

    d = pl.program_id(0)
    c = pl.program_id(2)
    ci = c + d * (nc - 1 - 2 * c)
    reset = ((d == 0) & _in_list(ci, resets_f)) | ((d == 1) & _in_list(ci, resets_r))

    @pl.when(reset)
    def _():
        s_ref[...] = jnp.zeros_like(s_ref)

    sgn = 1 - 2 * d
    tt = lax.broadcasted_iota(jnp.int32, (L, LANES), 0)
    ss = lax.broadcasted_iota(jnp.int32, (L, LANES), 1) % L
    delta = sgn * (tt - ss)
    incl = delta >= 0
    strict = delta > 0
    eye = (tt == ss).astype(F32)
    tri = (sgn * (lax.broadcasted_iota(jnp.int32, (L, L), 0)
                  - lax.broadcasted_iota(jnp.int32, (L, L), 1)) >= 0).astype(BF16)
    first_head = lax.broadcasted_iota(jnp.int32, (L, LANES), 1) < RWKV_HEAD
    bi = lax.broadcasted_iota(jnp.int32, (LANES, LANES), 0) // RWKV_HEAD
    bj = lax.broadcasted_iota(jnp.int32, (LANES, LANES), 1) // RWKV_HEAD
    same_head = bi == bj
    head_ones = same_head.astype(BF16)
    n_double = int(math.log2(L)) - 1

    def bd(x):
        xb = x.astype(BF16)
        zero = jnp.zeros_like(xb)
        return jnp.concatenate([jnp.where(first_head, xb, zero), jnp.where(first_head, zero, xb)], axis=0)

    kkcs = [k_ref[:, p * LANES:(p + 1) * LANES].astype(F32) * kk_ref[:, p * LANES:(p + 1) * LANES]
            for p in range(G)]
    n2_all = jnp.dot(jnp.concatenate([(kkc * kkc).astype(BF16) for kkc in kkcs], axis=0), head_ones,
                     preferred_element_type=F32)
    pairs = []
    for p in range(G):
        sl = slice(p * LANES, (p + 1) * LANES)
        r_ = r_ref[:, sl].astype(F32)
        k_ = k_ref[:, sl].astype(F32)
        v_ = v_ref[:, sl].astype(F32)
        ld_b = ld_ref[:, sl]
        ld_ = ld_b.astype(F32)
        a_ = a_ref[:, sl].astype(F32)
        if has_vres:
            v_ = v_ + (vf_ref[:, sl].astype(F32) - v_) * vg_ref[:, sl].astype(F32)
        kkn = kkcs[p] / jnp.maximum(jnp.sqrt(n2_all[p * L:(p + 1) * L]), 1e-12)
        kd = k_ * (1.0 + (a_ - 1.0) * ka_ref[:, sl])
        cs = jnp.dot(tri, ld_b, preferred_element_type=F32)
        w_inv = jnp.exp(-cs)
        bt = kkn * a_ * w_inv
        kt = kd * w_inv
        pairs.append(dict(
            sl=sl, v=v_, bt=bt, kt=kt, st=s_ref[p],
            lhs=jnp.concatenate([-kkn * jnp.exp(cs - ld_), r_ * jnp.exp(cs)], axis=0).astype(BF16),
            wend=jnp.exp(jnp.sum(ld_, axis=0, keepdims=True))))

    for pr in pairs:
        rhs_t = jnp.concatenate([bd(pr["bt"]), bd(pr["kt"]), pr["st"].astype(BF16)], axis=0)
        prod = _ntdot(pr["lhs"], rhs_t)
        pr["a_ab"] = jnp.where(strict, prod[:L, :LANES], 0.0)
        pr["a_rb"] = jnp.where(incl, prod[L:, :LANES], 0.0)
        a_ak = jnp.where(strict, prod[:L, LANES:2 * LANES], 0.0)
        a_rk = jnp.where(incl, prod[L:, LANES:2 * LANES], 0.0)
        pr["sp"] = prod[:, 2 * LANES:]
        pr["a_k"] = jnp.concatenate([a_ak, a_rk], axis=0)
    for pr in pairs:
        pr["spv"] = pr["sp"] + _bdot(pr["a_k"], bd(pr["v"]))
    for pr in pairs:
        pr["tinv"] = eye + pr["a_ab"]
        pr["pw"] = _bdot(pr["a_ab"], bd(pr["a_ab"]))
    for lvl in range(1, n_double + 1):
        last = lvl == n_double
        for pr in pairs:
            rows = pr["tinv"] if last else jnp.concatenate([pr["tinv"], pr["pw"]], axis=0)
            x = _bdot(rows, bd(pr["pw"]))
            pr["tinv"] = pr["tinv"] + x[:L]
            if not last:
                pr["pw"] = x[L:]
    for pr in pairs:
        pr["u"] = _bdot(pr["tinv"], bd(pr["spv"][:L]))
    for pr in pairs:
        pr["y"] = pr["spv"][L:] + _bdot(pr["a_rb"], bd(pr["u"]))
    for p, pr in enumerate(pairs):
        y_ref[:, pr["sl"]] = pr["y"].astype(BF16)
        uv = jnp.concatenate([pr["u"], pr["v"]], axis=0).astype(BF16)
        bk = jnp.concatenate([pr["bt"], pr["kt"]], axis=0).astype(BF16)
        upd = lax.dot_general(uv, bk, _TN, preferred_element_type=F32)
        s_ref[p] = (pr["st"] + jnp.where(same_head, upd, 0.0)) * pr["wend"]


def _wkv(r, k, v, ld, a, k_k, k_a, vres, seq_starts, seq_ends):
    T, D = r.shape
    L = WKV_CHUNK
    npairs = D // LANES
    assert 2 * L == LANES and RWKV_HEAD == L
    G = math.gcd(npairs, WKV_PAIRS_PER_STEP)
    nc = T // L
    W = G * LANES
    resets_f = tuple(s // L for s in seq_starts)
    resets_r = tuple(e // L - 1 for e in seq_ends)
    has_vres = vres is not None

    def cidx(d, c):
        return c + d * (nc - 1 - 2 * c)

    tok = pl.BlockSpec((L, W), lambda d, p, c: (cidx(d, c), p))
    tokd = pl.BlockSpec((None, L, W), lambda d, p, c: (d, cidx(d, c), p))
    par = pl.BlockSpec((1, W), lambda d, p, c: (0, p))
    args = [r, k, v, ld, a, k_k, k_a]
    in_specs = [tok, tok, tok, tokd, tokd, par, par]
    if has_vres:
        args += list(vres)
        in_specs += [tok, tok]
    return pl.pallas_call(
        functools.partial(_wkv_kernel, L, G, nc, resets_f, resets_r, has_vres),
        grid=(2, npairs // G, nc),
        in_specs=in_specs,
        out_specs=tokd,
        out_shape=jax.ShapeDtypeStruct((2, T, D), BF16),
        scratch_shapes=[pltpu.VMEM((G, LANES, LANES), F32)],
        compiler_params=_params(("arbitrary", "arbitrary", "arbitrary")),
    )(*args)


def _rwkv_out_kernel(has_vres, *refs):
    (x_ref, y_ref, r_ref, k_ref, v_ref, a_ref, g_ref, ka_ref, rk_ref, gnw_ref, gnb_ref,
     wo_ref, gpost_ref) = refs[:13]
    refs = refs[13:]
    if has_vres:
        vf_ref, vg_ref = refs[:2]
        refs = refs[2:]
    o_ref, act_ref = refs
    D = x_ref.shape[1]

    bi = lax.broadcasted_iota(jnp.int32, (LANES, LANES), 0) // RWKV_HEAD
    bj = lax.broadcasted_iota(jnp.int32, (LANES, LANES), 1) // RWKV_HEAD
    head_ones = (bi == bj).astype(BF16)
    inv_n = 1.0 / RWKV_HEAD

    slabs = [slice(s * LANES, (s + 1) * LANES) for s in range(D // LANES)]
    ys = [y_ref[0, :, sl].astype(F32) + y_ref[1, :, sl].astype(F32) for sl in slabs]
    means = [_bdot(y, head_ones) * inv_n for y in ys]
    dys = [y - mu for y, mu in zip(ys, means)]
    vars_ = [_bdot(dy * dy, head_ones) * inv_n for dy in dys]
    rks = []
    for sl in slabs:
        a_mean = 0.5 * (a_ref[0, :, sl].astype(F32) + a_ref[1, :, sl].astype(F32))
        k_mean = k_ref[:, sl].astype(F32) * (1.0 + (a_mean - 1.0) * ka_ref[:, sl])
        rks.append(_bdot(r_ref[:, sl].astype(F32) * k_mean * rk_ref[:, sl], head_ones))
    h = None
    slabs_per_dot = MXU_COLS // LANES
    for s, (sl, dy, var, rk) in enumerate(zip(slabs, dys, vars_, rks)):
        yn = dy * lax.rsqrt(var + GN_EPS) * gnw_ref[:, sl] + gnb_ref[:, sl]
        v_ = v_ref[:, sl].astype(F32)
        if has_vres:
            v_ = v_ + (vf_ref[:, sl].astype(F32) - v_) * vg_ref[:, sl].astype(F32)
        act_ref[:, sl] = ((yn + rk * v_) * g_ref[:, sl].astype(F32)).astype(BF16)
        if (s + 1) % slabs_per_dot == 0:
            ks = slice((s + 1 - slabs_per_dot) * LANES, (s + 1) * LANES)
            part = jnp.dot(act_ref[:, ks], wo_ref[ks, :], preferred_element_type=F32)
            h = part if h is None else h + part
    o_ref[...] = x_ref[...] + _rms(h, gpost_ref[...])


def _rwkv_out(x, y, r, k, v, a, g, k_a, r_k, gn_w, gn_b, w_o, gpost, vres):
    T, D = x.shape
    tm = _tile(T, RWKV_OUT_ROWS, BF16_ROWS)
    has_vres = vres is not None
    tok = pl.BlockSpec((tm, D), lambda i: (i, 0))
    tok2 = pl.BlockSpec((2, tm, D), lambda i: (0, i, 0))
    par = pl.BlockSpec((1, D), lambda i: (0, 0))
    args = [x, y, r, k, v, a, g, k_a, r_k, gn_w, gn_b, w_o, gpost]
    in_specs = [tok, tok2, tok, tok, tok, tok2, tok, par, par, par, par,
                pl.BlockSpec((D, D), lambda i: (0, 0)), par]
    if has_vres:
        args += list(vres)
        in_specs += [tok, tok]
    return pl.pallas_call(
        functools.partial(_rwkv_out_kernel, has_vres),
        grid=(T // tm,),
        in_specs=in_specs,
        out_specs=tok,
        out_shape=jax.ShapeDtypeStruct((T, D), F32),
        scratch_shapes=[pltpu.VMEM((tm, D), BF16)],
        compiler_params=_params(("parallel",)),
    )(*args)


def _qkv_kernel(tm, tn, deint, x_ref, gpre_ref, w_ref, cos_ref, sin_ref, o_ref, xn_ref, *acc):
    j = pl.program_id(1)

    @pl.when(j == 0)
    def _():
        def put(rows, y):
            xn_ref[rows, :] = y.astype(BF16)
        _rms_rows(x_ref, gpre_ref[...], put)

    cos = cos_ref[...]
    sin = sin_ref[...]
    xn = xn_ref[...]
    per_dot = MXU_COLS // ATT_HEAD
    for s in range(tn // ATT_HEAD):
        sl = slice(s * ATT_HEAD, (s + 1) * ATT_HEAD)
        if s % per_dot == 0:
            prod = jnp.dot(xn, w_ref[:, s * ATT_HEAD:s * ATT_HEAD + MXU_COLS],
                           preferred_element_type=F32)
        t = prod[:, (s % per_dot) * ATT_HEAD:(s % per_dot + 1) * ATT_HEAD]
        rot = t * cos + pltpu.roll(t, ATT_HEAD // 2, 1) * sin
        if deint == 1:
            o_ref[:, sl] = rot.astype(BF16)
        else:
            acc_ref, = acc
            slot = s % acc_ref.shape[0]
            acc_ref[slot] = rot
            for c in range(deint):
                o_ref[c, :, sl] = acc_ref[slot, pl.ds(c, tm // deint, stride=deint), :].astype(BF16)


def _rope_tables(pos):
    half = ROPE_DIM // 2
    inv = ROPE_THETA ** (-jnp.arange(half, dtype=F32) / half)
    ang = pos.astype(F32)[:, None] * inv[None, :]
    cos, sin = jnp.cos(ang), jnp.sin(ang)
    T = pos.shape[0]
    gap = ATT_HEAD // 2 - half
    ones = jnp.ones((T, gap), F32)
    zeros = jnp.zeros((T, gap), F32)
    cos_t = jnp.concatenate([cos, ones, cos, ones], axis=1)
    sin_t = jnp.concatenate([-sin, zeros, sin, zeros], axis=1)
    return (jnp.stack([cos_t, jnp.ones_like(cos_t)]), jnp.stack([sin_t, jnp.zeros_like(sin_t)]))


def _rope_perm_kernel(w_ref, o_ref):
    kind = pl.program_id(2) % 3

    @pl.when(kind == 2)
    def _():
        o_ref[...] = w_ref[...].astype(BF16)

    @pl.when(kind < 2)
    def _():
        half = ROPE_DIM // 2
        mid = ATT_HEAD // 2
        lane = lax.broadcasted_iota(jnp.int32, (w_ref.shape[0], ATT_HEAD), 1)
        from_hi = (lane >= half) & (lane < 2 * half)
        from_lo = (lane >= mid) & (lane < mid + half)
        for c0 in range(0, w_ref.shape[1], ATT_HEAD):
            w = w_ref[:, c0:c0 + ATT_HEAD]
            swapped = jnp.where(from_hi, pltpu.roll(w, ATT_HEAD - (mid - half), 1),
                                jnp.where(from_lo, pltpu.roll(w, mid - half, 1), w))
            o_ref[:, c0:c0 + ATT_HEAD] = swapped.astype(BF16)


def _rope_perm(w_qkv, D):
    n, K, N = w_qkv.shape
    tr = _tile(K, 512)
    spec = pl.BlockSpec((None, tr, D), lambda l, i, j: (l, i, j))
    return pl.pallas_call(
        _rope_perm_kernel,
        grid=(n, K // tr, N // D),
        in_specs=[spec],
        out_specs=spec,
        out_shape=jax.ShapeDtypeStruct(w_qkv.shape, BF16),
        compiler_params=_params(("parallel", "parallel", "arbitrary")),
    )(w_qkv)


def _qkv(x, gpre, w, layer, tables, gi, dil):
    T, D = x.shape
    tn = _tile(D, QKV_COLS, MXU_COLS)
    nj = 3 * D // tn

    def plane(j):
        return (j * tn) // (2 * D)

    tm = _tile(T, QKV_ROWS, BF16_ROWS * dil)
    tab = pl.BlockSpec((None, tm, ATT_HEAD), lambda i, j: (plane(j), i, 0))
    if dil == 1:
        out = pl.BlockSpec((None, tm, tn), lambda i, j: (0, i, j))
        deint_scratch = []
    else:
        out = pl.BlockSpec((dil, tm // dil, tn), lambda i, j: (0, i, j))
        deint_scratch = [pltpu.VMEM((2, tm, ATT_HEAD), F32)]
    return pl.pallas_call(
        functools.partial(_qkv_kernel, tm, tn, dil),
        grid=(T // tm, nj),
        in_specs=[
            pl.BlockSpec((tm, D), lambda i, j: (i, 0)),
            pl.BlockSpec((1, D), lambda i, j: (0, 0)),
            pl.BlockSpec((None, D, tn), lambda i, j: (layer, 0, gi * nj + j)),
            tab, tab,
        ],
        out_specs=out,
        out_shape=jax.ShapeDtypeStruct((dil, T // dil, 3 * D), BF16),
        scratch_shapes=[pltpu.VMEM((tm, D), BF16)] + deint_scratch,
        compiler_params=_params(("parallel", "arbitrary")),
    )(x, gpre, w, *tables)


def _attn_kernel(bq, n_heads, bounds, cur_ref, kp_ref, kn_ref, vp_ref, vn_ref, o_ref, lse_ref):
    D = n_heads * ATT_HEAD
    row0 = pl.program_id(0) * bq
    lo = jnp.int32(bounds[0])
    hi = jnp.int32(bounds[-1])
    for b in bounds[1:-1]:
        lo = jnp.where(row0 >= b, b, lo)
    for b in reversed(bounds[1:-1]):
        hi = jnp.where(row0 < b, b, hi)
    nk = bq + 2 * ATT_HALF
    qrow = row0 + lax.broadcasted_iota(jnp.int32, (bq, nk), 0)
    krow = row0 - ATT_HALF + lax.broadcasted_iota(jnp.int32, (bq, nk), 1)
    valid = (jnp.abs(krow - qrow) <= ATT_HALF) & (krow >= lo) & (krow < hi)
    scale = ATT_HEAD ** -0.5
    lane = lax.broadcasted_iota(jnp.int32, (bq, LANES), 1)
    lse_all = jnp.zeros((bq, LANES), F32)
    ones = jnp.ones((nk, ATT_HEAD), BF16)
    for h in range(n_heads):
        sl = slice(h * ATT_HEAD, (h + 1) * ATT_HEAD)
        ksl = slice(D + h * ATT_HEAD, D + (h + 1) * ATT_HEAD)
        vsl = slice(2 * D + h * ATT_HEAD, 2 * D + (h + 1) * ATT_HEAD)
        q = cur_ref[:, sl]
        kcat = jnp.concatenate([kp_ref[:, sl], cur_ref[:, ksl], kn_ref[:, sl]], axis=0)
        vcat = jnp.concatenate([vp_ref[:, sl], cur_ref[:, vsl], vn_ref[:, sl]], axis=0)
        s = lax.dot_general(q, kcat, _NT, preferred_element_type=F32) * (scale * LOG2E)
        s = jnp.where(valid, s, -1e30)
        m = jnp.max(s, axis=-1, keepdims=True)
        p = jnp.exp2(s - m).astype(BF16)
        ol = jnp.dot(p, jnp.concatenate([vcat, ones], axis=1), preferred_element_type=F32)
        l = ol[:, ATT_HEAD:]
        o_ref[:, sl] = (ol[:, :ATT_HEAD] / l).astype(BF16)
        lse_all = jnp.where(lane == h, (m + jnp.log2(l)) * (1.0 / LOG2E), lse_all)
    lse_ref[...] = lse_all


def _attn_group(qkv, D, seq_starts, seq_ends):
    dil, rows, _ = qkv.shape
    bounds = tuple(s // dil for s in seq_starts) + (seq_ends[-1] // dil,)
    seg = [b - a for a, b in zip(bounds[:-1], bounds[1:])]
    bq = ATTN_QUERY_ROWS
    for s in seg:
        bq = math.gcd(bq, s)
    assert bq >= ATT_HALF, "sequence too short for the dilated band blocks"
    nb = rows // bq

    per = bq // ATT_HALF
    nh = rows // ATT_HALF

    def halo(kind, after):
        def imap(i, c):
            blk = (i + 1) * per if after else i * per - 1
            return (c, jnp.clip(blk, 0, nh - 1), kind)
        return pl.BlockSpec((None, ATT_HALF, D), imap)

    return pl.pallas_call(
        functools.partial(_attn_kernel, bq, D // ATT_HEAD, bounds),
        grid=(nb, dil),
        in_specs=[pl.BlockSpec((None, bq, 3 * D), lambda i, c: (c, i, 0)),
                  halo(1, False), halo(1, True), halo(2, False), halo(2, True)],
        out_specs=[pl.BlockSpec((None, bq, D), lambda i, c: (c, i, 0)),
                   pl.BlockSpec((None, bq, LANES), lambda i, c: (c, i, 0))],
        out_shape=[jax.ShapeDtypeStruct((dil, rows, D), BF16),
                   jax.ShapeDtypeStruct((dil, rows, LANES), F32)],
        compiler_params=_params(("parallel", "arbitrary")),
    )(qkv, qkv, qkv, qkv, qkv)


def _attn_out_kernel(tm, x_ref, o0_ref, o1_ref, o2_ref, l0_ref, l1_ref, l2_ref, wo_ref, gpost_ref,
                     out_ref, act_ref, *scratch):
    D = x_ref.shape[1]

    def token_major(ref, sl, s):
        dil = ref.shape[0]
        if dil == 1:
            return ref[0, :, sl].astype(F32)
        for c in range(dil):
            s[pl.ds(c, tm // dil, stride=dil), :] = ref[c, :, sl].astype(F32)
        return s[...]

    free = list(scratch)
    l0, l1, l2 = [token_major(r, slice(0, LANES), free.pop() if r.shape[0] > 1 else None)
                  for r in (l0_ref, l1_ref, l2_ref)]
    m = jnp.maximum(jnp.maximum(l0, l1), l2)
    e0, e1, e2 = jnp.exp(l0 - m), jnp.exp(l1 - m), jnp.exp(l2 - m)
    inv = 1.0 / (e0 + e1 + e2)
    w0, w1, w2 = e0 * inv, e1 * inv, e2 * inv
    h = None
    heads_per_dot = MXU_COLS // ATT_HEAD
    for k in range(D // ATT_HEAD):
        sl = slice(k * ATT_HEAD, (k + 1) * ATT_HEAD)
        pool = list(free)
        o0, o1, o2 = [token_major(r, sl, pool.pop() if r.shape[0] > 1 else None)
                      for r in (o0_ref, o1_ref, o2_ref)]
        act_ref[:, sl] = (w0[:, k:k + 1] * o0 + w1[:, k:k + 1] * o1
                          + w2[:, k:k + 1] * o2).astype(BF16)
        if (k + 1) % heads_per_dot == 0:
            ks = slice((k + 1 - heads_per_dot) * ATT_HEAD, (k + 1) * ATT_HEAD)
            part = jnp.dot(act_ref[:, ks], wo_ref[ks, :], preferred_element_type=F32)
            h = part if h is None else h + part
    out_ref[...] = x_ref[...] + _rms(h, gpost_ref[...])


def _attn_out(x, os_, lses, w_o, gpost):
    T, D = x.shape
    tm = _tile(T, ATTN_OUT_ROWS, BF16_ROWS * max(o.shape[0] for o in os_))
    tok = pl.BlockSpec((tm, D), lambda i: (i, 0))

    def planes(a):
        dil = a.shape[0]
        return pl.BlockSpec((dil, tm // dil, a.shape[2]), lambda i: (0, i, 0))

    n_scratch = sum(2 for o in os_ if o.shape[0] > 1)
    return pl.pallas_call(
        functools.partial(_attn_out_kernel, tm),
        grid=(T // tm,),
        in_specs=[tok] + [planes(a) for a in os_] + [planes(a) for a in lses]
        + [pl.BlockSpec((D, D), lambda i: (0, 0)), pl.BlockSpec((1, D), lambda i: (0, 0))],
        out_specs=tok,
        out_shape=jax.ShapeDtypeStruct((T, D), F32),
        scratch_shapes=[pltpu.VMEM((tm, D), BF16)] + [pltpu.VMEM((tm, LANES), F32)] * n_scratch,
        compiler_params=_params(("parallel",)),
    )(x, *os_, *lses, w_o, gpost)


def kernel(x_prompt, x_sample, norm_pre, norm_post, ffn_w_gate, ffn_w_up, ffn_w_down, rwkv_mu, rwkv_w_rkv, rwkv_w0, rwkv_w1, rwkv_w2, rwkv_a0, rwkv_a1, rwkv_a2, rwkv_v0, rwkv_v1, rwkv_v2, rwkv_g1, rwkv_g2, rwkv_k_k, rwkv_k_a, rwkv_r_k, rwkv_gn_w, rwkv_gn_b, rwkv_w_o, attn_w_qkv, attn_w_o):
    D = x_prompt.shape[-1]
    depth = norm_pre.shape[0]
    seqs = [x_prompt.shape[1]] * x_prompt.shape[0] + [x_sample.shape[1]] * x_sample.shape[0]
    seq_starts, seq_ends, t = [], [], 0
    for s in seqs:
        seq_starts.append(t)
        t += s
        seq_ends.append(t)
    seq_starts, seq_ends = tuple(seq_starts), tuple(seq_ends)
    x = jnp.concatenate([x_prompt.reshape(-1, D), x_sample.reshape(-1, D)], axis=0)
    pos = jnp.concatenate([jnp.arange(s, dtype=jnp.int32) for s in seqs])
    tables = _rope_tables(pos)

    def vec(a):
        return a.reshape(1, D)

    wg, wu, wd = ffn_w_gate.astype(BF16), ffn_w_up.astype(BF16), ffn_w_down.astype(BF16)
    w_rkv = rwkv_w_rkv.astype(BF16)
    w_qkv = _rope_perm(attn_w_qkv, D)
    v_first = None
    for layer in range(depth):
        x = _ffn(x, vec(norm_pre[layer, 0]), wg, wu, wd, vec(norm_post[layer, 0]), layer, 0)
        j = layer // 2
        if layer % 2 == 0:
            vres_w = None if j == 0 else (rwkv_v0[j - 1], rwkv_v1[j - 1], rwkv_v2[j - 1])
            pre = _rwkv_pre(x, seq_starts, seq_ends, vec(norm_pre[layer, 1]), rwkv_mu[j],
                            rwkv_w0[j], rwkv_w1[j], rwkv_w2[j], rwkv_a0[j], rwkv_a1[j], rwkv_a2[j],
                            rwkv_g1[j], rwkv_g2[j], vres_w)
            xr, xk, xv, ld, a, g = pre[:6]
            r = _mm(xr, w_rkv, (j, 0))
            k = _mm(xk, w_rkv, (j, 1))
            v = _mm(xv, w_rkv, (j, 2))
            if j == 0:
                v_first, vres = v, None
            else:
                vres = (v_first, pre[6])
            y = _wkv(r, k, v, ld, a, vec(rwkv_k_k[j]), vec(rwkv_k_a[j]), vres, seq_starts, seq_ends)
            x = _rwkv_out(x, y, r, k, v, a, g, vec(rwkv_k_a[j]), vec(rwkv_r_k[j]),
                          vec(rwkv_gn_w[j]), vec(rwkv_gn_b[j]), rwkv_w_o[j].astype(BF16),
                          vec(norm_post[layer, 1]), vres)
        else:
            os_, lses = [], []
            for gi, (_, dil) in enumerate(DILATED_GROUPS):
                qkv = _qkv(x, vec(norm_pre[layer, 1]), w_qkv, j, tables, gi, dil)
                o, lse = _attn_group(qkv, D, seq_starts, seq_ends)
                os_.append(o)
                lses.append(lse)
            x = _attn_out(x, os_, lses, attn_w_o[j].astype(BF16), vec(norm_post[layer, 1]))
        x = _ffn(x, vec(norm_pre[layer, 2]), wg, wu, wd, vec(norm_post[layer, 2]), layer, 1)
    n_p = x_prompt.shape[0] * x_prompt.shape[1]
    return (x[:n_p].reshape(x_prompt.shape), x[n_p:].reshape(x_sample.shape))
```
